```python
import math
import jax, jax.numpy as jnp
from jax import lax
import numpy as np

D_MODEL = 1024
BATCH = 2
SEQ = 8192
DEPTH = 2
DEC_BATCH = 128
DEC_SEQ = 8
PAST_LEN = 2048
PAGE_SIZE = 128

D_RNN = D_MODEL
RNN_HEADS = 16
RNN_BLOCK = D_RNN // RNN_HEADS
CONV_WIDTH = 4
LRU_C = 8.0
D_SSM = D_MODEL // 2
SSM_GROUP = 16
SSM_GROUPS = D_SSM // SSM_GROUP
SSM_STATE = 64
ATTN_HEADS = 8
HEAD_DIM = 64
D_ATTN = ATTN_HEADS * HEAD_DIM
MOBA_BLOCK = 256
MOBA_TOPK = 3
Q_CHUNK = 128
ROPE_THETA = 10000.0
N_BRANCH = 3
D_FF = ((8 * D_MODEL // 3 + 255) // 256) * 256
LN_EPS = 1e-5
NEG_INF = -1e30
DN_ALPHA = (2.0 * DEPTH) ** 0.25
DN_BETA = (8.0 * DEPTH) ** -0.25
N_IN = D_RNN + D_SSM + 3 * D_ATTN + N_BRANCH * D_MODEL
SPLITS = (D_RNN, D_RNN + D_SSM, D_RNN + D_SSM + D_ATTN, D_RNN + D_SSM + 2 * D_ATTN, D_RNN + D_SSM + 3 * D_ATTN)

kernel_name = 'hybrid_rglru_s5_moba_step'


def layer_norm(x, g, b):
    xf = x.astype(jnp.float32)
    mu = jnp.mean(xf, axis=-1, keepdims=True)
    var = jnp.mean(jnp.square(xf - mu), axis=-1, keepdims=True)
    return ((xf - mu) * lax.rsqrt(var + LN_EPS) * g.astype(jnp.float32) + b.astype(jnp.float32))


def rope(x, pos):
    half = HEAD_DIM // 2
    inv = jnp.power(ROPE_THETA, -jnp.arange(half, dtype=jnp.float32) * (2.0 / HEAD_DIM))
    ang = pos.astype(jnp.float32)[:, None, None] * inv
    cos, sin = jnp.cos(ang), jnp.sin(ang)
    xf = x.astype(jnp.float32)
    x1, x2 = xf[..., :half], xf[..., half:]
    return jnp.concatenate([x1 * cos - x2 * sin, x2 * cos + x1 * sin], axis=-1).astype(x.dtype)


def causal_conv(x, buf, w, b):
    t = x.shape[1]
    xp = jnp.concatenate([buf, x], axis=1)
    y = b + w[0] * xp[:, 0:t]
    for j in range(1, CONV_WIDTH):
        y = y + w[j] * xp[:, j:j + t]
    return y, xp[:, t:]


def _lin_combine(e1, e2):
    a1, b1 = e1
    a2, b2 = e2
    return a1 * a2, a2 * b1 + b2


def _cplx_combine(e1, e2):
    ar1, ai1, br1, bi1 = e1
    ar2, ai2, br2, bi2 = e2
    return (ar1 * ar2 - ai1 * ai2, ar1 * ai2 + ai1 * ar2,
            ar2 * br1 - ai2 * bi1 + br2, ar2 * bi1 + ai2 * br1 + bi2)


def rg_lru(x, h0, w_a, b_a, w_x, b_x, lam):
    bsz, t, _ = x.shape
    f32 = jnp.float32
    xf = x.astype(f32)
    xh = xf.reshape(bsz, t, RNN_HEADS, RNN_BLOCK)
    r = jax.nn.sigmoid(jnp.einsum('bthi,hij->bthj', xh, w_a.astype(f32)).reshape(bsz, t, D_RNN) + b_a.astype(f32))
    i = jax.nn.sigmoid(jnp.einsum('bthi,hij->bthj', xh, w_x.astype(f32)).reshape(bsz, t, D_RNN) + b_x.astype(f32))
    log_a = -LRU_C * r * jax.nn.softplus(-lam.astype(f32))
    a = jnp.exp(log_a)
    mult = jnp.sqrt(jnp.maximum(-jnp.expm1(2.0 * log_a), 0.0))
    bt = mult * (i * xf)
    bt = bt.at[:, 0].add(a[:, 0] * h0.astype(f32))
    _, h = lax.associative_scan(_lin_combine, (a, bt), axis=1)
    return h.astype(x.dtype), h[:, -1].astype(x.dtype)


def s5_ssm(u, h0r, h0i, lam_re, lam_im, b_re, b_im, c_re, c_im, d, log_step, w_glu, b_glu):
    bsz, t, _ = u.shape
    f32 = jnp.float32
    dt = jnp.exp(log_step.astype(f32))[:, None]
    lr, li = lam_re.astype(f32), lam_im.astype(f32)
    mag = jnp.exp(lr * dt)
    abr, abi = mag * jnp.cos(li * dt), mag * jnp.sin(li * dt)
    nr, ni = abr - 1.0, abi
    den = lr * lr + li * li
    zr, zi = (nr * lr + ni * li) / den, (ni * lr - nr * li) / den
    br_, bi_ = b_re.astype(f32), b_im.astype(f32)
    bbr = zr[..., None] * br_ - zi[..., None] * bi_
    bbi = zr[..., None] * bi_ + zi[..., None] * br_
    ug = u.astype(f32).reshape(bsz, t, SSM_GROUPS, SSM_GROUP)
    bur = jnp.einsum('btgc,gpc->btgp', ug, bbr)
    bui = jnp.einsum('btgc,gpc->btgp', ug, bbi)
    h0r, h0i = h0r.astype(f32), h0i.astype(f32)
    bur = bur.at[:, 0].add(abr * h0r - abi * h0i)
    bui = bui.at[:, 0].add(abr * h0i + abi * h0r)
    ar = jnp.broadcast_to(abr, bur.shape)
    ai = jnp.broadcast_to(abi, bui.shape)
    _, _, hr, hi = lax.associative_scan(_cplx_combine, (ar, ai, bur, bui), axis=1)
    y = jnp.einsum('btgp,gcp->btgc', hr, c_re.astype(f32)) - jnp.einsum('btgp,gcp->btgc', hi, c_im.astype(f32))
    y = y.reshape(bsz, t, D_SSM) + d.astype(f32) * u.astype(f32)
    g = jax.nn.gelu(y)
    out = g * jax.nn.sigmoid(g @ w_glu.astype(f32) + b_glu.astype(f32))
    return out.astype(u.dtype), hr[:, -1].astype(u.dtype), hi[:, -1].astype(u.dtype)


def moba_head(q, k, v, pos0):
    f32 = jnp.float32
    bsz, tq, _ = q.shape
    length = k.shape[1]
    nb = -(-length // MOBA_BLOCK)
    pad = (nb + 1) * MOBA_BLOCK - length
    kp = jnp.pad(k, ((0, 0), (0, pad), (0, 0)))
    vp = jnp.pad(v, ((0, 0), (0, pad), (0, 0)))
    kblk = kp.reshape(bsz, nb + 1, MOBA_BLOCK, HEAD_DIM)
    vblk = vp.reshape(bsz, nb + 1, MOBA_BLOCK, HEAD_DIM)
    kmean = jnp.mean(kblk[:, :nb].astype(f32), axis=2)
    ksel = min(MOBA_TOPK, nb)
    qc = Q_CHUNK if tq % Q_CHUNK == 0 else tq
    nq = tq // qc
    qs = q.reshape(bsz, nq, qc, HEAD_DIM).transpose(1, 0, 2, 3)
    bidx = jnp.arange(bsz)[:, None, None]
    scale = HEAD_DIM ** -0.5

    def chunk(args):
        ci, qq = args
        qf = qq.astype(f32)
        start = pos0 + ci * qc
        p = start + jnp.arange(qc)
        qblk = p // MOBA_BLOCK
        gate = jnp.einsum('bqd,bnd->bqn', qf, kmean)
        past = jnp.arange(nb)[None, :] < qblk[:, None]
        gate = jnp.where(past[None], gate, -jnp.inf)
        _, sel = lax.top_k(gate, ksel)
        valid = sel < qblk[None, :, None]
        kg = kblk[bidx, sel]
        vg = vblk[bidx, sel]
        s_sel = jnp.einsum('bqd,bqjsd->bqjs', qf, kg.astype(f32)) * scale
        s_sel = jnp.where(valid[..., None], s_sel, NEG_INF).reshape(bsz, qc, ksel * MOBA_BLOCK)
        wstart = (start // MOBA_BLOCK) * MOBA_BLOCK
        kw = lax.dynamic_slice_in_dim(kp, wstart, 2 * MOBA_BLOCK, axis=1)
        vw = lax.dynamic_slice_in_dim(vp, wstart, 2 * MOBA_BLOCK, axis=1)
        kpos = wstart + jnp.arange(2 * MOBA_BLOCK)
        own = ((kpos[None, :] // MOBA_BLOCK) == qblk[:, None]) & (kpos[None, :] <= p[:, None])
        s_own = jnp.einsum('bqd,bkd->bqk', qf, kw.astype(f32)) * scale
        s_own = jnp.where(own[None], s_own, NEG_INF)
        prob = jax.nn.softmax(jnp.concatenate([s_sel, s_own], axis=-1), axis=-1)
        p_sel = prob[..., :ksel * MOBA_BLOCK].reshape(bsz, qc, ksel, MOBA_BLOCK)
        p_own = prob[..., ksel * MOBA_BLOCK:]
        o = jnp.einsum('bqjs,bqjsd->bqd', p_sel, vg.astype(f32)) + jnp.einsum('bqk,bkd->bqd', p_own, vw.astype(f32))
        return o.astype(q.dtype)

    out = lax.map(chunk, (jnp.arange(nq), qs))
    return out.transpose(1, 0, 2, 3).reshape(bsz, tq, HEAD_DIM)


def moba_attention(q, k, v, pos0):
    def one_head(args):
        qh, kh, vh = args
        return moba_head(qh, kh, vh, pos0)
    return lax.map(one_head, (q, k, v))


def trunk_layer(x, pos0, past_k, past_v, conv_buf, h0, s5_h0_re, s5_h0_im,
                w_in, conv_w, conv_b, rg_w_a, rg_b_a, rg_w_x, rg_b_x, rg_lambda,
                s5_lambda_re, s5_lambda_im, s5_b_re, s5_b_im, s5_c_re, s5_c_im, s5_d, s5_log_step, s5_w_glu, s5_b_glu,
                w_br_rnn, w_br_ssm, w_br_attn, w_out, ln1_g, ln1_b, w_ffn_in, w_ffn_out, ln2_g, ln2_b):
    bsz, t, _ = x.shape
    proj = x @ w_in
    xr, us, q, k, v, g = jnp.split(proj, SPLITS, axis=-1)
    xc, conv_new = causal_conv(xr, conv_buf, conv_w, conv_b)
    ya, h_new = rg_lru(xc, h0, rg_w_a, rg_b_a, rg_w_x, rg_b_x, rg_lambda)
    ys, s5r, s5i = s5_ssm(us, s5_h0_re, s5_h0_im, s5_lambda_re, s5_lambda_im, s5_b_re, s5_b_im,
                          s5_c_re, s5_c_im, s5_d, s5_log_step, s5_w_glu, s5_b_glu)
    pos = pos0 + jnp.arange(t)
    qr = rope(q.reshape(bsz, t, ATTN_HEADS, HEAD_DIM), pos)
    kr = rope(k.reshape(bsz, t, ATTN_HEADS, HEAD_DIM), pos)
    vr = v.reshape(bsz, t, ATTN_HEADS, HEAD_DIM)
    k_all = jnp.concatenate([past_k.astype(kr.dtype), kr], axis=1).transpose(2, 0, 1, 3)
    v_all = jnp.concatenate([past_v.astype(vr.dtype), vr], axis=1).transpose(2, 0, 1, 3)
    yc = moba_attention(qr.transpose(2, 0, 1, 3), k_all, v_all, pos0)
    yc = yc.transpose(1, 2, 0, 3).reshape(bsz, t, D_ATTN)
    gates = jax.nn.sigmoid(g.reshape(bsz, t, N_BRANCH, D_MODEL))
    merged = (gates[:, :, 0] * (ya @ w_br_rnn) + gates[:, :, 1] * (ys @ w_br_ssm)
              + gates[:, :, 2] * (yc @ w_br_attn))
    x = layer_norm(DN_ALPHA * x + merged @ w_out, ln1_g, ln1_b).astype(x.dtype)
    hg, hu = jnp.split(x @ w_ffn_in, 2, axis=-1)
    x = layer_norm(DN_ALPHA * x + (jax.nn.silu(hg) * hu) @ w_ffn_out, ln2_g, ln2_b).astype(x.dtype)
    return x, kr, vr, conv_new, h_new, s5r, s5i


def setup_inputs(seed: int = 0) -> dict:
    key = jax.random.key(seed)
    ks = iter(jax.random.split(key, 40))
    f32 = jnp.float32

    def nrm(shape, s=1.0):
        return jax.random.normal(next(ks), shape, f32) * s

    n_pages = PAST_LEN // PAGE_SIZE
    in_use = DEC_BATCH * n_pages
    n_pool = in_use + max(1, in_use // 4)
    out = {}
    out['x_prompt'] = nrm((BATCH, SEQ, D_MODEL))
    out['x_sample'] = nrm((DEC_BATCH, DEC_SEQ, D_MODEL))
    out['cache_k'] = nrm((DEPTH, n_pool, PAGE_SIZE, ATTN_HEADS, HEAD_DIM))
    out['cache_v'] = nrm((DEPTH, n_pool, PAGE_SIZE, ATTN_HEADS, HEAD_DIM))
    out['state_conv'] = nrm((DEPTH, DEC_BATCH, CONV_WIDTH - 1, D_RNN))
    out['state_rglru'] = nrm((DEPTH, DEC_BATCH, D_RNN), 0.5)
    out['state_s5_re'] = nrm((DEPTH, DEC_BATCH, SSM_GROUPS, SSM_STATE), 0.5)
    out['state_s5_im'] = nrm((DEPTH, DEC_BATCH, SSM_GROUPS, SSM_STATE), 0.5)
    out['page_table'] = jax.random.permutation(next(ks), n_pool)[:in_use].reshape(DEC_BATCH, n_pages).astype(jnp.int32)
    out['w_in'] = nrm((DEPTH, D_MODEL, N_IN), D_MODEL ** -0.5)
    out['conv_w'] = nrm((DEPTH, CONV_WIDTH, D_RNN), CONV_WIDTH ** -0.5)
    out['conv_b'] = nrm((DEPTH, D_RNN), 0.01)
    out['rg_w_a'] = nrm((DEPTH, RNN_HEADS, RNN_BLOCK, RNN_BLOCK), RNN_BLOCK ** -0.5)
    out['rg_b_a'] = nrm((DEPTH, D_RNN), 0.01)
    out['rg_w_x'] = nrm((DEPTH, RNN_HEADS, RNN_BLOCK, RNN_BLOCK), RNN_BLOCK ** -0.5)
    out['rg_b_x'] = nrm((DEPTH, D_RNN), 0.01)
    u = jax.random.uniform(next(ks), (DEPTH, D_RNN), f32, minval=0.9, maxval=0.999)
    s = jnp.power(u, 1.0 / LRU_C)
    out['rg_lambda'] = jnp.log(s) - jnp.log1p(-s)
    out['s5_lambda_re'] = -0.5 + nrm((DEPTH, SSM_GROUPS, SSM_STATE), 0.01)
    out['s5_lambda_im'] = np.pi * jnp.arange(SSM_STATE, dtype=f32) + nrm((DEPTH, SSM_GROUPS, SSM_STATE), 0.01)
    out['s5_b_re'] = nrm((DEPTH, SSM_GROUPS, SSM_STATE, SSM_GROUP), (2 * SSM_GROUP) ** -0.5)
    out['s5_b_im'] = nrm((DEPTH, SSM_GROUPS, SSM_STATE, SSM_GROUP), (2 * SSM_GROUP) ** -0.5)
    out['s5_c_re'] = nrm((DEPTH, SSM_GROUPS, SSM_GROUP, SSM_STATE), SSM_STATE ** -0.5)
    out['s5_c_im'] = nrm((DEPTH, SSM_GROUPS, SSM_GROUP, SSM_STATE), SSM_STATE ** -0.5)
    out['s5_d'] = nrm((DEPTH, D_SSM))
    out['s5_log_step'] = jax.random.uniform(next(ks), (DEPTH, SSM_GROUPS), f32, minval=math.log(1e-3), maxval=math.log(1e-1))
    out['s5_w_glu'] = nrm((DEPTH, D_SSM, D_SSM), D_SSM ** -0.5)
    out['s5_b_glu'] = nrm((DEPTH, D_SSM), 0.01)
    out['w_br_rnn'] = nrm((DEPTH, D_RNN, D_MODEL), D_RNN ** -0.5)
    out['w_br_ssm'] = nrm((DEPTH, D_SSM, D_MODEL), D_SSM ** -0.5)
    out['w_br_attn'] = nrm((DEPTH, D_ATTN, D_MODEL), D_ATTN ** -0.5)
    out['w_out'] = nrm((DEPTH, D_MODEL, D_MODEL), D_MODEL ** -0.5 * DN_BETA)
    out['ln1_g'] = 1.0 + nrm((DEPTH, D_MODEL), 0.01)
    out['ln1_b'] = nrm((DEPTH, D_MODEL), 0.01)
    out['w_ffn_in'] = nrm((DEPTH, D_MODEL, 2 * D_FF), D_MODEL ** -0.5)
    out['w_ffn_out'] = nrm((DEPTH, D_FF, D_MODEL), D_FF ** -0.5 * DN_BETA)
    out['ln2_g'] = 1.0 + nrm((DEPTH, D_MODEL), 0.01)
    out['ln2_b'] = nrm((DEPTH, D_MODEL), 0.01)
    return out


def reference(x_prompt, x_sample, cache_k, cache_v, state_conv, state_rglru, state_s5_re, state_s5_im, page_table,
              w_in, conv_w, conv_b, rg_w_a, rg_b_a, rg_w_x, rg_b_x, rg_lambda,
              s5_lambda_re, s5_lambda_im, s5_b_re, s5_b_im, s5_c_re, s5_c_im, s5_d, s5_log_step, s5_w_glu, s5_b_glu,
              w_br_rnn, w_br_ssm, w_br_attn, w_out, ln1_g, ln1_b, w_ffn_in, w_ffn_out, ln2_g, ln2_b):
    bp = x_prompt.shape[0]
    bs = x_sample.shape[0]
    n_pages = page_table.shape[1]
    dt = x_prompt.dtype
    yp, ys = x_prompt, x_sample
    kp_l, vp_l, cp_l, hp_l, srp_l, sip_l = [], [], [], [], [], []
    ks_l, vs_l, cs_l, hs_l, srs_l, sis_l = [], [], [], [], [], []
    for l in range(DEPTH):
        lw = (w_in[l], conv_w[l], conv_b[l], rg_w_a[l], rg_b_a[l], rg_w_x[l], rg_b_x[l], rg_lambda[l],
              s5_lambda_re[l], s5_lambda_im[l], s5_b_re[l], s5_b_im[l], s5_c_re[l], s5_c_im[l], s5_d[l],
              s5_log_step[l], s5_w_glu[l], s5_b_glu[l], w_br_rnn[l], w_br_ssm[l], w_br_attn[l], w_out[l],
              ln1_g[l], ln1_b[l], w_ffn_in[l], w_ffn_out[l], ln2_g[l], ln2_b[l])
        yp, kr, vr, cn, hn, s5r, s5i = trunk_layer(
            yp, 0,
            jnp.zeros((bp, 0, ATTN_HEADS, HEAD_DIM), dt), jnp.zeros((bp, 0, ATTN_HEADS, HEAD_DIM), dt),
            jnp.zeros((bp, CONV_WIDTH - 1, D_RNN), dt), jnp.zeros((bp, D_RNN), dt),
            jnp.zeros((bp, SSM_GROUPS, SSM_STATE), dt), jnp.zeros((bp, SSM_GROUPS, SSM_STATE), dt), *lw)
        kp_l.append(kr); vp_l.append(vr); cp_l.append(cn); hp_l.append(hn); srp_l.append(s5r); sip_l.append(s5i)
        pk = cache_k[l][page_table].reshape(bs, n_pages * PAGE_SIZE, ATTN_HEADS, HEAD_DIM)
        pv = cache_v[l][page_table].reshape(bs, n_pages * PAGE_SIZE, ATTN_HEADS, HEAD_DIM)
        ys, kr, vr, cn, hn, s5r, s5i = trunk_layer(
            ys, PAST_LEN, pk, pv, state_conv[l], state_rglru[l], state_s5_re[l], state_s5_im[l], *lw)
        ks_l.append(kr); vs_l.append(vr); cs_l.append(cn); hs_l.append(hn); srs_l.append(s5r); sis_l.append(s5i)
    return (yp, ys,
            jnp.stack(kp_l), jnp.stack(vp_l), jnp.stack(cp_l), jnp.stack(hp_l), jnp.stack(srp_l), jnp.stack(sip_l),
            jnp.stack(ks_l), jnp.stack(vs_l), jnp.stack(cs_l), jnp.stack(hs_l), jnp.stack(srs_l), jnp.stack(sis_l))
```

```python
import functools

import jax
import jax.numpy as jnp
import numpy as np
from jax import lax
from jax.experimental import pallas as pl
from jax.experimental.pallas import tpu as pltpu

F32 = jnp.float32
BF16 = jnp.bfloat16

ATTN_HEADS = 8
HEAD_DIM = 64
RNN_HEADS = 16
CONV_WIDTH = 4
LRU_C = 8.0
SSM_GROUP = 16
SSM_STATE = 64
MOBA_BLOCK = 256
MOBA_TOPK = 3
ROPE_THETA = 10000.0
LN_EPS = 1e-5
NEG_INF = -1e30

LANES = 128
SUBLANES = 8
MXU_DIM = 256
VMEM_LIMIT_BYTES = 56 * 1024 * 1024

GATE_SLOTS = 32
_NT = (((1,), (1,)), ((), ()))


def _cparams(n_axes):
    return pltpu.CompilerParams(dimension_semantics=("arbitrary",) * n_axes,
                                vmem_limit_bytes=VMEM_LIMIT_BYTES)


def _div_pow2(x, d):
    assert d & (d - 1) == 0
    return lax.shift_right_logical(x, d.bit_length() - 1)


def _layer_norm(z, g, b):
    mu = jnp.mean(z, axis=-1, keepdims=True)
    zc = z - mu
    var = jnp.mean(zc * zc, axis=-1, keepdims=True)
    return zc * lax.rsqrt(var + LN_EPS) * g + b


def _proj_kernel(x_ref, w_ref, o_ref):
    o_ref[...] = jnp.dot(x_ref[...].astype(BF16), w_ref[...], preferred_element_type=F32)


def _proj(x, w, tm, tn):
    n, d = x.shape
    nout = w.shape[1]
    return pl.pallas_call(
        _proj_kernel,
        grid=(n // tm, nout // tn),
        in_specs=[pl.BlockSpec((tm, d), lambda i, j: (i, 0)),
                  pl.BlockSpec((d, tn), lambda i, j: (0, j))],
        out_specs=pl.BlockSpec((tm, tn), lambda i, j: (i, j)),
        out_shape=jax.ShapeDtypeStruct((n, nout), F32),
        compiler_params=_cparams(2),
        name="proj",
    )(x, w)


def _rglru_kernel(x_ref, cbuf_ref, h0_ref, cw_ref, cb_ref, wbd_ref, ba_ref, bx_ref, lam_ref,
                  ya_ref, cnew_ref, hnew_ref, xbuf, hcar, *, bb, tc):
    ti = pl.program_id(1)
    nt = pl.num_programs(1)
    c_dim = x_ref.shape[-1]
    m = bb * tc

    @pl.when(ti == 0)
    def _():
        xbuf[:, 0:SUBLANES, :] = cbuf_ref[...]
        hcar[...] = h0_ref[...]

    x = x_ref[...]
    xbuf[:, SUBLANES:SUBLANES + tc, :] = x
    cw = cw_ref[...]
    xc = cb_ref[...] + cw[0:1] * xbuf[:, 5:5 + tc, :]
    xc = xc + cw[1:2] * xbuf[:, 6:6 + tc, :]
    xc = xc + cw[2:3] * xbuf[:, 7:7 + tc, :]
    xc = xc + cw[3:4] * x
    xc2 = xc.reshape(m, c_dim)
    xcb = xc2.astype(BF16)

    nl = -lam_ref[...]
    softplus = jnp.maximum(nl, 0.0) + jnp.log1p(jnp.exp(-jnp.abs(nl)))
    c_row = -LRU_C * softplus
    row = lax.broadcasted_iota(jnp.int32, (m, MXU_DIM), 0) & (tc - 1)
    hc = hcar[...]
    for g in range(c_dim // MXU_DIM):
        sl = slice(g * MXU_DIM, (g + 1) * MXU_DIM)
        ga = jnp.dot(xcb[:, sl], wbd_ref[g], preferred_element_type=F32)
        r = jax.nn.sigmoid(ga[:, :MXU_DIM] + ba_ref[:, sl])
        i = jax.nn.sigmoid(ga[:, MXU_DIM:] + bx_ref[:, sl])
        log_a = c_row[:, sl] * r
        a = jnp.exp(log_a)
        mult = jnp.sqrt(jnp.maximum(-jnp.tanh(log_a) * (a * a + 1.0), 0.0))
        b = mult * (i * xc2[:, sl])
        s = 1
        while s < tc:
            msk = row >= s
            a_sh = jnp.where(msk, pltpu.roll(a, s, 0), 1.0)
            b_sh = jnp.where(msk, pltpu.roll(b, s, 0), 0.0)
            b = a * b_sh + b
            a = a * a_sh
            s *= 2
        hcg = jnp.broadcast_to(hc[:, :, sl], (bb, tc, MXU_DIM)).reshape(m, MXU_DIM)
        h3 = (b + a * hcg).reshape(bb, tc, MXU_DIM)
        ya_ref[:, :, sl] = h3.astype(ya_ref.dtype)
        hcar[:, :, sl] = h3[:, tc - 1:tc, :]

    xbuf[:, 0:SUBLANES, :] = xbuf[:, tc:tc + SUBLANES, :]

    @pl.when(ti == nt - 1)
    def _():
        cnew_ref[...] = xbuf[:, 5:8, :]
        hnew_ref[...] = hcar[...]


def _rglru(proj3, cbuf8, h0, lw, bb, tc):
    bsz, t, _ = proj3.shape
    c_dim = h0.shape[-1]
    full = lambda *shape: pl.BlockSpec(shape, lambda bi, ti: (0,) * len(shape))
    kern = functools.partial(_rglru_kernel, bb=bb, tc=tc)
    return pl.pallas_call(
        kern,
        grid=(bsz // bb, t // tc),
        in_specs=[pl.BlockSpec((bb, tc, c_dim), lambda bi, ti: (bi, ti, 0)),
                  pl.BlockSpec((bb, SUBLANES, c_dim), lambda bi, ti: (bi, 0, 0)),
                  pl.BlockSpec((bb, 1, c_dim), lambda bi, ti: (bi, 0, 0)),
                  full(CONV_WIDTH, c_dim), full(1, c_dim),
                  full(c_dim // MXU_DIM, MXU_DIM, 2 * MXU_DIM),
                  full(1, c_dim), full(1, c_dim), full(1, c_dim)],
        out_specs=[pl.BlockSpec((bb, tc, c_dim), lambda bi, ti: (bi, ti, 0)),
                   pl.BlockSpec((bb, CONV_WIDTH - 1, c_dim), lambda bi, ti: (bi, 0, 0)),
                   pl.BlockSpec((bb, 1, c_dim), lambda bi, ti: (bi, 0, 0))],
        out_shape=[jax.ShapeDtypeStruct((bsz, t, c_dim), BF16),
                   jax.ShapeDtypeStruct((bsz, CONV_WIDTH - 1, c_dim), F32),
                   jax.ShapeDtypeStruct((bsz, 1, c_dim), F32)],
        scratch_shapes=[pltpu.VMEM((bb, tc + SUBLANES, c_dim), F32),
                        pltpu.VMEM((bb, 1, c_dim), F32)],
        compiler_params=_cparams(2),
        name="rglru",
    )(proj3, cbuf8, h0, lw["conv_w"], lw["conv_b"], lw["rg_wbd"], lw["rg_b_a"], lw["rg_b_x"],
      lw["rg_lambda"])


def _cmul(ar, ai, br, bi):
    return ar * br - ai * bi, ar * bi + ai * br


def _s5_kernel(u_ref, h0r_ref, h0i_ref, lamr_ref, lami_ref, ldt_ref, wbr_ref, wbi_ref,
               cre_ref, cim_ref, d_ref, wglu_ref, bglu_ref,
               ys_ref, sr_ref, si_ref,
               wb_sc, apr_sc, api_sc, hr_sc, hi_sc, cr_sc, ci_sc, *, bb, tc, lane_chunk):
    bi = pl.program_id(0)
    ti = pl.program_id(1)
    nt = pl.num_programs(1)
    m = bb * tc
    n_state = hr_sc.shape[-1]
    d_ssm = u_ref.shape[-1]
    half = n_state // 2

    @pl.when((bi == 0) & (ti == 0))
    def _():
        dt = jnp.exp(ldt_ref[...])
        lr = lamr_ref[...]
        li = lami_ref[...]
        mag = jnp.exp(lr * dt)
        abr = mag * jnp.cos(li * dt)
        abi = mag * jnp.sin(li * dt)
        nr = abr - 1.0
        den = lr * lr + li * li
        zr = (nr * lr + abi * li) / den
        zi = (abi * lr - nr * li) / den
        for kb in range(2):
            ks = slice(kb * half, (kb + 1) * half)
            br = wbr_ref[kb]
            bim = wbi_ref[kb]
            wb_sc[kb, :, 0:half] = (zr[:, ks] * br - zi[:, ks] * bim).astype(BF16)
            wb_sc[kb, :, half:n_state] = (zr[:, ks] * bim + zi[:, ks] * br).astype(BF16)
        row8 = lax.broadcasted_iota(jnp.int32, (SUBLANES, n_state), 0)
        pr = jnp.broadcast_to(abr, (SUBLANES, n_state))
        pi = jnp.broadcast_to(abi, (SUBLANES, n_state))
        for s in (1, 2, 4):
            msk = row8 >= s
            qr = jnp.where(msk, pltpu.roll(pr, s, 0), 1.0)
            qi = jnp.where(msk, pltpu.roll(pi, s, 0), 0.0)
            pr, pi = _cmul(pr, pi, qr, qi)
        apr_sc[...] = pr
        api_sc[...] = pi

    @pl.when(ti == 0)
    def _():
        cr_sc[...] = h0r_ref[...]
        ci_sc[...] = h0i_ref[...]

    u = u_ref[...].reshape(m, d_ssm)
    ub = u.astype(BF16)
    k_half = d_ssm // 2
    for kb in range(2):
        bu = jnp.dot(ub[:, kb * k_half:(kb + 1) * k_half], wb_sc[kb], preferred_element_type=F32)
        hr_sc[:, :, kb * half:(kb + 1) * half] = bu[:, :half].reshape(bb, tc, half)
        hi_sc[:, :, kb * half:(kb + 1) * half] = bu[:, half:].reshape(bb, tc, half)

    rows8 = bb * SUBLANES
    row = lax.broadcasted_iota(jnp.int32, (rows8, lane_chunk), 0) & (SUBLANES - 1)
    for lc in range(n_state // lane_chunk):
        ls = slice(lc * lane_chunk, (lc + 1) * lane_chunk)
        pr = apr_sc[:, ls]
        pi = api_sc[:, ls]
        steps = ((1, pr[0:1], pi[0:1]), (2, pr[1:2], pi[1:2]), (4, pr[3:4], pi[3:4]))
        prt = jnp.broadcast_to(pr[None], (bb, SUBLANES, lane_chunk)).reshape(rows8, lane_chunk)
        pit = jnp.broadcast_to(pi[None], (bb, SUBLANES, lane_chunk)).reshape(rows8, lane_chunk)

        def body(j, carry, ls=ls, steps=steps, prt=prt, pit=pit):
            cr, ci = carry
            r0 = pl.multiple_of(j * SUBLANES, SUBLANES)
            xr = hr_sc[:, pl.ds(r0, SUBLANES), ls].reshape(rows8, lane_chunk)
            xi = hi_sc[:, pl.ds(r0, SUBLANES), ls].reshape(rows8, lane_chunk)
            for s, ar, ai in steps:
                msk = row >= s
                sr = jnp.where(msk, pltpu.roll(xr, s, 0), 0.0)
                si = jnp.where(msk, pltpu.roll(xi, s, 0), 0.0)
                dr, di = _cmul(ar, ai, sr, si)
                xr = xr + dr
                xi = xi + di
            crb = jnp.broadcast_to(cr, (bb, SUBLANES, lane_chunk)).reshape(rows8, lane_chunk)
            cib = jnp.broadcast_to(ci, (bb, SUBLANES, lane_chunk)).reshape(rows8, lane_chunk)
            dr, di = _cmul(prt, pit, crb, cib)
            xr3 = (xr + dr).reshape(bb, SUBLANES, lane_chunk)
            xi3 = (xi + di).reshape(bb, SUBLANES, lane_chunk)
            hr_sc[:, pl.ds(r0, SUBLANES), ls] = xr3
            hi_sc[:, pl.ds(r0, SUBLANES), ls] = xi3
            return xr3[:, SUBLANES - 1:SUBLANES, :], xi3[:, SUBLANES - 1:SUBLANES, :]

        cr, ci = lax.fori_loop(0, tc // SUBLANES, body, (cr_sc[:, :, ls], ci_sc[:, :, ls]))
        cr_sc[:, :, ls] = cr
        ci_sc[:, :, ls] = ci

    n_pack = cre_ref.shape[0]
    k_pack = n_state // n_pack
    parts = []
    for p4 in range(n_pack):
        ks = slice(p4 * k_pack, (p4 + 1) * k_pack)
        hrb = hr_sc[:, :, ks].reshape(m, k_pack).astype(BF16)
        hib = hi_sc[:, :, ks].reshape(m, k_pack).astype(BF16)
        parts.append(jnp.dot(hrb, cre_ref[p4], preferred_element_type=F32)
                     - jnp.dot(hib, cim_ref[p4], preferred_element_type=F32))
    y = jnp.concatenate(parts, axis=1) + d_ref[...] * u
    g = y * (0.5 * (1.0 + jnp.tanh(np.sqrt(2.0 / np.pi).astype(np.float32) * (y + 0.044715 * (y * y * y)))))
    z = jnp.dot(g.astype(BF16), wglu_ref[...], preferred_element_type=F32) + bglu_ref[...]
    out = g * jax.nn.sigmoid(z)
    ys_ref[...] = out.reshape(bb, tc, d_ssm).astype(ys_ref.dtype)

    @pl.when(ti == nt - 1)
    def _():
        sr_ref[...] = cr_sc[...]
        si_ref[...] = ci_sc[...]


def _s5(proj3, h0r, h0i, lw, bb, tc, col_block):
    bsz, t, _ = proj3.shape
    n_state = h0r.shape[-1]
    d_ssm = lw["s5_d"].shape[-1]
    n_pack = lw["s5_cre"].shape[0]
    full = lambda *shape: pl.BlockSpec(shape, lambda bi, ti: (0,) * len(shape))
    lane_chunk = max(LANES, 4 * LANES // bb)
    kern = functools.partial(_s5_kernel, bb=bb, tc=tc, lane_chunk=lane_chunk)
    state_spec = pl.BlockSpec((bb, 1, n_state), lambda bi, ti: (bi, 0, 0))
    return pl.pallas_call(
        kern,
        grid=(bsz // bb, t // tc),
        in_specs=[pl.BlockSpec((bb, tc, d_ssm), lambda bi, ti: (bi, ti, col_block)),
                  state_spec, state_spec,
                  full(1, n_state), full(1, n_state), full(1, n_state),
                  full(2, d_ssm // 2, n_state // 2), full(2, d_ssm // 2, n_state // 2),
                  full(n_pack, n_state // n_pack, d_ssm // n_pack),
                  full(n_pack, n_state // n_pack, d_ssm // n_pack),
                  full(1, d_ssm), full(d_ssm, d_ssm), full(1, d_ssm)],
        out_specs=[pl.BlockSpec((bb, tc, d_ssm), lambda bi, ti: (bi, ti, 0)), state_spec, state_spec],
        out_shape=[jax.ShapeDtypeStruct((bsz, t, d_ssm), BF16),
                   jax.ShapeDtypeStruct((bsz, 1, n_state), F32),
                   jax.ShapeDtypeStruct((bsz, 1, n_state), F32)],
        scratch_shapes=[pltpu.VMEM((2, d_ssm // 2, n_state), BF16),
                        pltpu.VMEM((SUBLANES, n_state), F32), pltpu.VMEM((SUBLANES, n_state), F32),
                        pltpu.VMEM((bb, tc, n_state), F32), pltpu.VMEM((bb, tc, n_state), F32),
                        pltpu.VMEM((bb, 1, n_state), F32), pltpu.VMEM((bb, 1, n_state), F32)],
        compiler_params=_cparams(2),
        name="s5",
    )(proj3, h0r, h0i, lw["s5_lam_re"], lw["s5_lam_im"], lw["s5_log_dt"], lw["s5_wbr"], lw["s5_wbi"],
      lw["s5_cre"], lw["s5_cim"], lw["s5_d"], lw["s5_w_glu"], lw["s5_b_glu"])


def _rope_kernel(q_ref, k_ref, v_ref, cos_ref, sin_ref, *rest, gate, blocks_per_seq):
    if gate:
        qs_ref, kr_ref, vr_ref, kb_ref, vb_ref, sel_ref, km_sc = rest
    else:
        qs_ref, kr_ref, vr_ref = rest
    tm, d_attn = q_ref.shape
    reps = d_attn // LANES
    cos = jnp.concatenate([cos_ref[...]] * reps, axis=1)
    sin = jnp.concatenate([sin_ref[...]] * reps, axis=1)
    lane = lax.broadcasted_iota(jnp.int32, (tm, d_attn), 1)
    first_half = (lane & (HEAD_DIM - 1)) < HEAD_DIM // 2

    def rot(x):
        partner = jnp.where(first_half, pltpu.roll(x, d_attn - HEAD_DIM // 2, 1),
                            pltpu.roll(x, HEAD_DIM // 2, 1))
        return x * cos + partner * sin

    qr = rot(q_ref[...])
    kr = rot(k_ref[...])
    v = v_ref[...]
    kr_ref[...] = kr
    vr_ref[...] = v
    if not gate:
        qs_ref[...] = qr
        return
    qs_ref[...] = (qr * (HEAD_DIM ** -0.5)).astype(BF16)
    kb_ref[...] = kr.astype(BF16)
    vb_ref[...] = v.astype(BF16)

    tb = pl.program_id(0) % blocks_per_seq

    @pl.when(tb == 0)
    def _():
        km_sc[...] = jnp.zeros_like(km_sc)

    gates = lax.dot_general(qr, km_sc[...], _NT, precision=lax.Precision.HIGHEST,
                            preferred_element_type=F32)
    n_col = ATTN_HEADS * GATE_SLOTS
    col = lax.broadcasted_iota(jnp.int32, (tm, n_col), 1)
    colf = col.astype(F32)
    g = jnp.where((col & (GATE_SLOTS - 1)) < tb, gates, -jnp.inf)
    head_of_col = _div_pow2(col, GATE_SLOTS)
    sel = jnp.zeros((tm, n_col), F32)
    for h in range(ATTN_HEADS):
        gh = jnp.where(head_of_col == h, g, -jnp.inf)
        for _ in range(MOBA_TOPK):
            mx = jnp.max(gh, axis=1, keepdims=True)
            cand = (gh == mx) & (mx > -jnp.inf)
            first = jnp.min(jnp.where(cand, colf, float(n_col)), axis=1, keepdims=True)
            pick = colf == first
            sel = jnp.where(pick, 1.0, sel)
            gh = jnp.where(pick, -jnp.inf, gh)
    sel_ref[...] = sel

    km = jnp.sum(kr, axis=0, keepdims=True) * (1.0 / MOBA_BLOCK)
    lane_row = lax.broadcasted_iota(jnp.int32, (1, d_attn), 1)
    for h in range(ATTN_HEADS):
        km_sc[pl.ds(h * GATE_SLOTS + tb, 1), :] = jnp.where(_div_pow2(lane_row, HEAD_DIM) == h, km, 0.0)


def _rope(proj2, cos_t, sin_t, tm, gate, blocks_per_seq, col0):
    n = proj2.shape[0]
    d_attn = ATTN_HEADS * HEAD_DIM
    n_tab = cos_t.shape[0] // tm
    qkv_spec = lambda c: pl.BlockSpec((tm, d_attn), lambda i, c=c: (i, c))
    tab_spec = pl.BlockSpec((tm, LANES), lambda i: (i % n_tab, 0))
    row_spec = pl.BlockSpec((tm, d_attn), lambda i: (i, 0))
    kern = functools.partial(_rope_kernel, gate=gate, blocks_per_seq=blocks_per_seq)
    if gate:
        n_col = ATTN_HEADS * GATE_SLOTS
        out_specs = [row_spec] * 5 + [pl.BlockSpec((tm, n_col), lambda i: (i, 0))]
        out_shape = [jax.ShapeDtypeStruct((n, d_attn), BF16), jax.ShapeDtypeStruct((n, d_attn), F32),
                     jax.ShapeDtypeStruct((n, d_attn), F32), jax.ShapeDtypeStruct((n, d_attn), BF16),
                     jax.ShapeDtypeStruct((n, d_attn), BF16), jax.ShapeDtypeStruct((n, n_col), F32)]
        scratch = [pltpu.VMEM((n_col, d_attn), F32)]
    else:
        out_specs = [row_spec] * 3
        out_shape = [jax.ShapeDtypeStruct((n, d_attn), F32)] * 3
        scratch = []
    return pl.pallas_call(
        kern,
        grid=(n // tm,),
        in_specs=[qkv_spec(col0), qkv_spec(col0 + 1), qkv_spec(col0 + 2), tab_spec, tab_spec],
        out_specs=out_specs,
        out_shape=out_shape,
        scratch_shapes=scratch,
        compiler_params=_cparams(1),
        name="rope_gate" if gate else "rope",
    )(proj2, proj2, proj2, cos_t, sin_t)


def _attn_kernel(q_ref, k_ref, v_ref, sel_ref, o_ref, acc_sc):
    hp = pl.program_id(1)
    qi = pl.program_id(2)
    blk = q_ref.shape[0]
    q = q_ref[...]
    lane = lax.broadcasted_iota(jnp.int32, (blk, LANES), 1)
    lo = lane < HEAD_DIM
    zero = jnp.zeros_like(q)
    qe = (jnp.where(lo, q, zero), jnp.where(lo, zero, q))
    selm = sel_ref[...]
    base = (hp % 2) * (2 * GATE_SLOTS)
    rowi = lax.broadcasted_iota(jnp.int32, (blk, blk), 0)
    coli = lax.broadcasted_iota(jnp.int32, (blk, blk), 1)
    causal = coli <= rowi

    r0 = pl.multiple_of(qi * blk, blk)
    ko = k_ref[pl.ds(r0, blk), :]
    vo = v_ref[pl.ds(r0, blk), :]
    stats = []
    for e in range(2):
        s = lax.dot_general(qe[e], ko, _NT, preferred_element_type=F32)
        s = jnp.where(causal, s, NEG_INF)
        mx = jnp.max(s, axis=1, keepdims=True)
        p = jnp.exp(s - mx)
        stats += [mx, jnp.sum(p, axis=1, keepdims=True)]
        acc_sc[e] = jnp.dot(p.astype(BF16), vo, preferred_element_type=F32)

    def body(n, carry):
        rn = pl.multiple_of(n * blk, blk)
        kn = k_ref[pl.ds(rn, blk), :]
        vn = v_ref[pl.ds(rn, blk), :]
        out = []
        for e in range(2):
            mx, l = carry[2 * e], carry[2 * e + 1]
            picked = jnp.sum(jnp.where(lane == base + e * GATE_SLOTS + n, selm, 0.0), axis=1, keepdims=True)
            s = lax.dot_general(qe[e], kn, _NT, preferred_element_type=F32)
            s = jnp.where(picked > 0.5, s, NEG_INF)
            mn = jnp.maximum(mx, jnp.max(s, axis=1, keepdims=True))
            alpha = jnp.exp(mx - mn)
            p = jnp.exp(s - mn)
            l = alpha * l + jnp.sum(p, axis=1, keepdims=True)
            acc_sc[e] = alpha * acc_sc[e] + jnp.dot(p.astype(BF16), vn, preferred_element_type=F32)
            out += [mn, l]
        return tuple(out)

    m0, l0, m1, l1 = lax.fori_loop(0, qi, body, tuple(stats))
    o_ref[...] = jnp.where(lo, acc_sc[0] / l0, acc_sc[1] / l1).astype(o_ref.dtype)


def _attn_prompt(qs, kb, vb, selm, bsz, t):
    d_attn = ATTN_HEADS * HEAD_DIM
    nq = t // MOBA_BLOCK
    kb3 = kb.reshape(bsz, t, d_attn)
    vb3 = vb.reshape(bsz, t, d_attn)
    kv_spec = pl.BlockSpec((None, t, LANES), lambda b, hp, qi: (b, 0, hp))
    return pl.pallas_call(
        _attn_kernel,
        grid=(bsz, d_attn // LANES, nq),
        in_specs=[pl.BlockSpec((MOBA_BLOCK, LANES), lambda b, hp, qi: (b * nq + qi, hp)),
                  kv_spec, kv_spec,
                  pl.BlockSpec((MOBA_BLOCK, LANES), lambda b, hp, qi: (b * nq + qi, hp // 2))],
        out_specs=pl.BlockSpec((MOBA_BLOCK, LANES), lambda b, hp, qi: (b * nq + qi, hp)),
        out_shape=jax.ShapeDtypeStruct((bsz * t, d_attn), BF16),
        scratch_shapes=[pltpu.VMEM((2, MOBA_BLOCK, LANES), F32)],
        compiler_params=_cparams(3),
        name="attn_prompt",
    )(qs, kb3, vb3, selm)


def _sattn_kernel(pt_ref, q_ref, kn_ref, vn_ref, k0_ref, k1_ref, v0_ref, v1_ref, o_ref,
                  g_sc, m_sc, l_sc, o_sc, *, nblk):
    n = pl.program_id(1)
    tq, d_attn = q_ref.shape
    rows = ATTN_HEADS * tq
    qr = q_ref[...]
    qt = jnp.concatenate([qr] * ATTN_HEADS, axis=0)
    row_head = _div_pow2(lax.broadcasted_iota(jnp.int32, (rows, d_attn), 0), tq)
    lane_head = _div_pow2(lax.broadcasted_iota(jnp.int32, (rows, d_attn), 1), HEAD_DIM)
    own = row_head == lane_head
    qbd = jnp.where(own, qt, 0.0)
    qs = (qbd * (HEAD_DIM ** -0.5)).astype(BF16)
    lane = lax.broadcasted_iota(jnp.int32, (rows, LANES), 1)

    kblk = jnp.concatenate([k0_ref[...], k1_ref[...]], axis=0)
    vblk = jnp.concatenate([v0_ref[...], v1_ref[...]], axis=0)
    kmean = jnp.sum(kblk, axis=0, keepdims=True) * (1.0 / MOBA_BLOCK)
    gate = jnp.sum(qbd * kmean, axis=1, keepdims=True)
    s = lax.dot_general(qs, kblk.astype(BF16), _NT, preferred_element_type=F32)
    mx = jnp.max(s, axis=1, keepdims=True)
    p = jnp.exp(s - mx)
    l = jnp.sum(p, axis=1, keepdims=True)
    o = jnp.dot(p.astype(BF16), vblk.astype(BF16), preferred_element_type=F32)

    @pl.when(n == 0)
    def _():
        g_sc[...] = jnp.full_like(g_sc, -jnp.inf)
        m_sc[...] = jnp.zeros_like(m_sc)
        l_sc[...] = jnp.zeros_like(l_sc)

    hit = lane == n
    g_sc[...] = jnp.where(hit, gate, g_sc[...])
    m_sc[...] = jnp.where(hit, mx, m_sc[...])
    l_sc[...] = jnp.where(hit, l, l_sc[...])
    o_sc[n] = jnp.where(own, o, 0.0)

    @pl.when(n == nblk - 1)
    def _():
        gh = jnp.where(lane < nblk, g_sc[...], -jnp.inf)
        lanef = lane.astype(F32)
        chosen = jnp.zeros((rows, LANES), F32)
        for _ in range(MOBA_TOPK):
            gmx = jnp.max(gh, axis=1, keepdims=True)
            cand = (gh == gmx) & (gmx > -jnp.inf)
            first = jnp.min(jnp.where(cand, lanef, float(LANES)), axis=1, keepdims=True)
            pick = lanef == first
            chosen = jnp.where(pick, 1.0, chosen)
            gh = jnp.where(pick, -jnp.inf, gh)
        selw = chosen > 0.5
        m_all = m_sc[...]
        l_all = l_sc[...]
        pad = jnp.zeros((LANES - tq, d_attn), F32)
        knew = jnp.concatenate([kn_ref[...], pad], axis=0).astype(BF16)
        vnew = jnp.concatenate([vn_ref[...], pad], axis=0).astype(BF16)
        s_own = lax.dot_general(qs, knew, _NT, preferred_element_type=F32)
        qpos = lax.broadcasted_iota(jnp.int32, (rows, LANES), 0) & (tq - 1)
        s_own = jnp.where((lane <= qpos) & (lane < tq), s_own, NEG_INF)
        m_sel = jnp.max(jnp.where(selw, m_all, -jnp.inf), axis=1, keepdims=True)
        mf = jnp.maximum(jnp.max(s_own, axis=1, keepdims=True), m_sel)
        w = jnp.where(selw, jnp.exp(m_all - mf), 0.0)
        p_own = jnp.exp(s_own - mf)
        lf = jnp.sum(w * l_all, axis=1, keepdims=True) + jnp.sum(p_own, axis=1, keepdims=True)
        of = jnp.where(own, jnp.dot(p_own.astype(BF16), vnew, preferred_element_type=F32), 0.0)
        for nn in range(nblk):
            of = of + w[:, nn:nn + 1] * o_sc[nn]
        out = of / lf
        res = out[0:tq]
        for h in range(1, ATTN_HEADS):
            res = res + out[h * tq:(h + 1) * tq]
        o_ref[...] = res.astype(o_ref.dtype)


def _attn_sample(qr3, kr3, vr3, cache_k4, cache_v4, page_table, layer):
    bsz, tq, d_attn = qr3.shape
    page = cache_k4.shape[2]
    pages_per_blk = MOBA_BLOCK // page
    assert pages_per_blk == 2
    nblk = page_table.shape[1] // pages_per_blk
    rows = ATTN_HEADS * tq
    tok_spec = pl.BlockSpec((None, tq, d_attn), lambda b, n, pt: (b, 0, 0))

    def page_spec(j):
        return pl.BlockSpec((None, None, page, d_attn),
                            lambda b, n, pt, j=j: (layer, pt[b, pages_per_blk * n + j], 0, 0))

    grid_spec = pltpu.PrefetchScalarGridSpec(
        num_scalar_prefetch=1,
        grid=(bsz, nblk),
        in_specs=[tok_spec, tok_spec, tok_spec, page_spec(0), page_spec(1), page_spec(0), page_spec(1)],
        out_specs=pl.BlockSpec((None, tq, d_attn), lambda b, n, pt: (b, 0, 0)),
        scratch_shapes=[pltpu.VMEM((rows, LANES), F32), pltpu.VMEM((rows, LANES), F32),
                        pltpu.VMEM((rows, LANES), F32), pltpu.VMEM((nblk, rows, d_attn), F32)],
    )
    return pl.pallas_call(
        functools.partial(_sattn_kernel, nblk=nblk),
        grid_spec=grid_spec,
        out_shape=jax.ShapeDtypeStruct((bsz, tq, d_attn), BF16),
        compiler_params=_cparams(2),
        name="attn_sample",
    )(page_table, qr3, kr3, vr3, cache_k4, cache_k4, cache_v4, cache_v4)


def _merge_kernel(x_ref, ya_ref, ys_ref, yc_ref, g0_ref, g1_ref, g2_ref,
                  wr_ref, ws_ref, wa_ref, wo_ref, lg_ref, lb_ref, o_ref, *, alpha):
    merged = jax.nn.sigmoid(g0_ref[...]) * jnp.dot(ya_ref[...], wr_ref[...], preferred_element_type=F32)
    merged = merged + jax.nn.sigmoid(g1_ref[...]) * jnp.dot(ys_ref[...], ws_ref[...],
                                                           preferred_element_type=F32)
    merged = merged + jax.nn.sigmoid(g2_ref[...]) * jnp.dot(yc_ref[...], wa_ref[...],
                                                           preferred_element_type=F32)
    z = alpha * x_ref[...] + jnp.dot(merged.astype(BF16), wo_ref[...], preferred_element_type=F32)
    o_ref[...] = _layer_norm(z, lg_ref[...], lb_ref[...])


def _merge(x, ya, ys, yc, proj2, lw, tm, gate_col0, alpha):
    n, d = x.shape
    row = lambda w: pl.BlockSpec((tm, w), lambda i: (i, 0))
    full = lambda a: pl.BlockSpec(a.shape, lambda i: (0,) * a.ndim)
    gspec = lambda c: pl.BlockSpec((tm, d), lambda i, c=c: (i, c))
    ws = [lw["w_br_rnn"], lw["w_br_ssm"], lw["w_br_attn"], lw["w_out"], lw["ln1_g"], lw["ln1_b"]]
    return pl.pallas_call(
        functools.partial(_merge_kernel, alpha=alpha),
        grid=(n // tm,),
        in_specs=[row(d), row(ya.shape[1]), row(ys.shape[1]), row(yc.shape[1]),
                  gspec(gate_col0), gspec(gate_col0 + 1), gspec(gate_col0 + 2)] + [full(a) for a in ws],
        out_specs=row(d),
        out_shape=jax.ShapeDtypeStruct((n, d), F32),
        compiler_params=_cparams(1),
        name="merge_ln",
    )(x, ya, ys, yc, proj2, proj2, proj2, *ws)


def _ffn_kernel(x_ref, wg_ref, wu_ref, wo_ref, lg_ref, lb_ref, o_ref, xb_sc, acc_sc, *, alpha):
    j = pl.program_id(1)

    @pl.when(j == 0)
    def _():
        xb_sc[...] = x_ref[...].astype(BF16)
        acc_sc[...] = jnp.zeros_like(acc_sc)

    xb = xb_sc[...]
    hg = jnp.dot(xb, wg_ref[...], preferred_element_type=F32)
    hu = jnp.dot(xb, wu_ref[...], preferred_element_type=F32)
    h = (hg * jax.nn.sigmoid(hg)) * hu
    acc_sc[...] += jnp.dot(h.astype(BF16), wo_ref[...], preferred_element_type=F32)

    @pl.when(j == pl.num_programs(1) - 1)
    def _():
        o_ref[...] = _layer_norm(alpha * x_ref[...] + acc_sc[...], lg_ref[...], lb_ref[...])


def _ffn(x, lw, tm, tf, alpha):
    n, d = x.shape
    d_ff = lw["w_ffn_out"].shape[0]
    nj = d_ff // tf
    return pl.pallas_call(
        functools.partial(_ffn_kernel, alpha=alpha),
        grid=(n // tm, nj),
        in_specs=[pl.BlockSpec((tm, d), lambda i, j: (i, 0)),
                  pl.BlockSpec((d, tf), lambda i, j: (0, j)),
                  pl.BlockSpec((d, tf), lambda i, j: (0, nj + j)),
                  pl.BlockSpec((tf, d), lambda i, j: (j, 0)),
                  pl.BlockSpec((1, d), lambda i, j: (0, 0)),
                  pl.BlockSpec((1, d), lambda i, j: (0, 0))],
        out_specs=pl.BlockSpec((tm, d), lambda i, j: (i, 0)),
        out_shape=jax.ShapeDtypeStruct((n, d), F32),
        scratch_shapes=[pltpu.VMEM((tm, d), BF16), pltpu.VMEM((tm, d), F32)],
        compiler_params=_cparams(2),
        name="ffn_ln",
    )(x, lw["w_ffn_in"], lw["w_ffn_in"], lw["w_ffn_out"], lw["ln2_g"], lw["ln2_b"])


def _block_diag(w, per_block):
    n, r, c = w.shape
    eye = jnp.eye(per_block, dtype=w.dtype)
    out = jnp.einsum("kgrc,gh->kgrhc", w.reshape(n // per_block, per_block, r, c), eye)
    return out.reshape(n // per_block, per_block * r, per_block * c)


def _prep_layer(l, p):
    heads_per_tile = MXU_DIM // (p["rg_w_a"].shape[-1])
    row = lambda a: a[l].reshape(1, -1)
    groups = p["s5_b_re"].shape[1]
    lw = {
        "w_in": p["w_in"][l].astype(BF16),
        "conv_w": p["conv_w"][l],
        "conv_b": row(p["conv_b"]),
        "rg_wbd": jnp.concatenate([_block_diag(p["rg_w_a"][l], heads_per_tile),
                                   _block_diag(p["rg_w_x"][l], heads_per_tile)], axis=2).astype(BF16),
        "rg_b_a": row(p["rg_b_a"]), "rg_b_x": row(p["rg_b_x"]), "rg_lambda": row(p["rg_lambda"]),
        "s5_lam_re": row(p["s5_lambda_re"]), "s5_lam_im": row(p["s5_lambda_im"]),
        "s5_log_dt": jnp.repeat(p["s5_log_step"][l], SSM_STATE).reshape(1, -1),
        "s5_wbr": _block_diag(jnp.swapaxes(p["s5_b_re"][l], 1, 2), groups // 2),
        "s5_wbi": _block_diag(jnp.swapaxes(p["s5_b_im"][l], 1, 2), groups // 2),
        "s5_cre": _block_diag(jnp.swapaxes(p["s5_c_re"][l], 1, 2), LANES // SSM_GROUP).astype(BF16),
        "s5_cim": _block_diag(jnp.swapaxes(p["s5_c_im"][l], 1, 2), LANES // SSM_GROUP).astype(BF16),
        "s5_d": row(p["s5_d"]),
        "s5_w_glu": p["s5_w_glu"][l].astype(BF16),
        "s5_b_glu": row(p["s5_b_glu"]),
        "w_br_rnn": p["w_br_rnn"][l].astype(BF16),
        "w_br_ssm": p["w_br_ssm"][l].astype(BF16),
        "w_br_attn": p["w_br_attn"][l].astype(BF16),
        "w_out": p["w_out"][l].astype(BF16),
        "ln1_g": row(p["ln1_g"]), "ln1_b": row(p["ln1_b"]),
        "w_ffn_in": p["w_ffn_in"][l].astype(BF16),
        "w_ffn_out": p["w_ffn_out"][l].astype(BF16),
        "ln2_g": row(p["ln2_g"]), "ln2_b": row(p["ln2_b"]),
    }
    return lw


def _rope_tables(pos0, t):
    half = HEAD_DIM // 2
    inv = jnp.power(ROPE_THETA, -jnp.arange(half, dtype=F32) * (2.0 / HEAD_DIM))
    ang = (pos0 + jnp.arange(t)).astype(F32)[:, None] * inv
    cos, sin = jnp.cos(ang), jnp.sin(ang)
    reps = LANES // HEAD_DIM
    cos_t = jnp.tile(jnp.concatenate([cos, cos], axis=1), (1, reps))
    sin_t = jnp.tile(jnp.concatenate([-sin, sin], axis=1), (1, reps))
    return cos_t, sin_t


def _largest_tile(n, cap):
    t = min(n, cap)
    while n % t:
        t //= 2
    return t


def _trunk_layer(x3, lw, alpha, cbuf8, h0, s5r0, s5i0, *, prompt, pos0=0, cache=None):
    bsz, t, d = x3.shape
    n = bsz * t
    d_rnn = h0.shape[-1]
    d_ssm = lw["s5_d"].shape[-1]
    d_attn = ATTN_HEADS * HEAD_DIM
    x2 = x3.reshape(n, d)
    proj2 = _proj(x2, lw["w_in"], _largest_tile(n, 512), 1536)
    proj3 = proj2.reshape(bsz, t, -1)

    if prompt:
        bb, tc = 1, MOBA_BLOCK
    else:
        bb, tc = _largest_tile(bsz, 32), t
    assert tc % SUBLANES == 0 and tc & (tc - 1) == 0 and t % tc == 0
    ya, conv_new, h_new = _rglru(proj3, cbuf8, h0, lw, bb, tc)
    ys, s5r, s5i = _s5(proj3, s5r0, s5i0, lw, bb, tc, d_rnn // d_ssm)

    col0 = (d_rnn + d_ssm) // d_attn
    tm_rope = MOBA_BLOCK
    assert n % tm_rope == 0
    if prompt:
        assert t % MOBA_BLOCK == 0 and t // MOBA_BLOCK <= GATE_SLOTS
        cos_t, sin_t = _rope_tables(0, t)
        qs, kr, vr, kb, vb, selm = _rope(proj2, cos_t, sin_t, tm_rope, True, t // MOBA_BLOCK, col0)
        yc = _attn_prompt(qs, kb, vb, selm, bsz, t)
    else:
        assert tm_rope % t == 0
        cos_t, sin_t = _rope_tables(pos0, t)
        cos_t = jnp.tile(cos_t, (tm_rope // t, 1))
        sin_t = jnp.tile(sin_t, (tm_rope // t, 1))
        qr, kr, vr = _rope(proj2, cos_t, sin_t, tm_rope, False, 1, col0)
        cache_k4, cache_v4, page_table, layer = cache
        yc = _attn_sample(qr.reshape(bsz, t, d_attn), kr.reshape(bsz, t, d_attn), vr.reshape(bsz, t, d_attn),
                          cache_k4, cache_v4, page_table, layer).reshape(n, d_attn)

    gate_col0 = (d_rnn + d_ssm + 3 * d_attn) // d
    x1 = _merge(x2, ya.reshape(n, d_rnn), ys.reshape(n, d_ssm), yc, proj2, lw, _largest_tile(n, 256),
                gate_col0, alpha)
    x_out = _ffn(x1, lw, _largest_tile(n, 1024), MXU_DIM, alpha)
    return (x_out.reshape(bsz, t, d), kr.reshape(bsz, t, ATTN_HEADS, HEAD_DIM),
            vr.reshape(bsz, t, ATTN_HEADS, HEAD_DIM), conv_new, h_new.reshape(bsz, d_rnn), s5r, s5i)


def kernel(x_prompt, x_sample, cache_k, cache_v, state_conv, state_rglru, state_s5_re, state_s5_im, page_table,
           w_in, conv_w, conv_b, rg_w_a, rg_b_a, rg_w_x, rg_b_x, rg_lambda,
           s5_lambda_re, s5_lambda_im, s5_b_re, s5_b_im, s5_c_re, s5_c_im, s5_d, s5_log_step, s5_w_glu, s5_b_glu,
           w_br_rnn, w_br_ssm, w_br_attn, w_out, ln1_g, ln1_b, w_ffn_in, w_ffn_out, ln2_g, ln2_b):
    params = dict(w_in=w_in, conv_w=conv_w, conv_b=conv_b, rg_w_a=rg_w_a, rg_b_a=rg_b_a, rg_w_x=rg_w_x,
                  rg_b_x=rg_b_x, rg_lambda=rg_lambda, s5_lambda_re=s5_lambda_re, s5_lambda_im=s5_lambda_im,
                  s5_b_re=s5_b_re, s5_b_im=s5_b_im, s5_c_re=s5_c_re, s5_c_im=s5_c_im, s5_d=s5_d,
                  s5_log_step=s5_log_step, s5_w_glu=s5_w_glu, s5_b_glu=s5_b_glu, w_br_rnn=w_br_rnn,
                  w_br_ssm=w_br_ssm, w_br_attn=w_br_attn, w_out=w_out, ln1_g=ln1_g, ln1_b=ln1_b,
                  w_ffn_in=w_ffn_in, w_ffn_out=w_ffn_out, ln2_g=ln2_g, ln2_b=ln2_b)
    depth = w_in.shape[0]
    alpha = (2.0 * depth) ** 0.25
    bp = x_prompt.shape[0]
    bs = x_sample.shape[0]
    d_rnn = state_rglru.shape[-1]
    groups, n_p = state_s5_re.shape[-2:]
    n_state = groups * n_p
    n_pages = page_table.shape[1]
    page = cache_k.shape[2]
    past_len = n_pages * page
    d_attn = ATTN_HEADS * HEAD_DIM
    assert past_len % MOBA_BLOCK == 0 and x_sample.shape[1] <= MOBA_BLOCK
    cache_k4 = cache_k.reshape(depth, cache_k.shape[1], page, d_attn)
    cache_v4 = cache_v.reshape(depth, cache_v.shape[1], page, d_attn)

    zeros_p = lambda *s: jnp.zeros((bp,) + s, F32)
    yp, ys = x_prompt, x_sample
    outs_p, outs_s = [], []
    for l in range(depth):
        lw = _prep_layer(l, params)
        res = _trunk_layer(yp, lw, alpha, zeros_p(SUBLANES, d_rnn), zeros_p(1, d_rnn),
                           zeros_p(1, n_state), zeros_p(1, n_state), prompt=True)
        yp = res[0]
        outs_p.append(res[1:])
        cbuf8 = jnp.pad(state_conv[l], ((0, 0), (SUBLANES - (CONV_WIDTH - 1), 0), (0, 0)))
        res = _trunk_layer(ys, lw, alpha, cbuf8, state_rglru[l].reshape(bs, 1, d_rnn),
                           state_s5_re[l].reshape(bs, 1, n_state), state_s5_im[l].reshape(bs, 1, n_state),
                           prompt=False, pos0=past_len, cache=(cache_k4, cache_v4, page_table, l))
        ys = res[0]
        outs_s.append(res[1:])

    def stack(outs, i, shape=None):
        arrs = [o[i] if shape is None else o[i].reshape(shape) for o in outs]
        return jnp.stack(arrs)

    return (yp, ys,
            stack(outs_p, 0), stack(outs_p, 1), stack(outs_p, 2), stack(outs_p, 3),
            stack(outs_p, 4, (bp, groups, n_p)), stack(outs_p, 5, (bp, groups, n_p)),
            stack(outs_s, 0), stack(outs_s, 1), stack(outs_s, 2), stack(outs_s, 3),
            stack(outs_s, 4, (bs, groups, n_p)), stack(outs_s, 5, (bs, groups, n_p)))
```

```python
import functools

import jax
import jax.numpy as jnp
import numpy as np
from jax import lax
from jax.experimental import pallas as pl
from jax.experimental.pallas import tpu as pltpu

F32 = jnp.float32
BF16 = jnp.bfloat16

ATTN_HEADS = 8
HEAD_DIM = 64
RNN_HEADS = 16
CONV_WIDTH = 4
LRU_C = 8.0
SSM_GROUP = 16
SSM_STATE = 64
MOBA_BLOCK = 256
MOBA_TOPK = 3
ROPE_THETA = 10000.0
LN_EPS = 1e-5
NEG_INF = -1e30

LANES = 128
SUBLANES = 8
MXU_DIM = 256
VMEM_LIMIT_BYTES = 56 * 1024 * 1024

GATE_SLOTS = 32
_NT = (((1,), (1,)), ((), ()))


def _cparams(n_axes):
    return pltpu.CompilerParams(dimension_semantics=("arbitrary",) * n_axes,
                                vmem_limit_bytes=VMEM_LIMIT_BYTES)


def _div_pow2(x, d):
    assert d & (d - 1) == 0
    return lax.shift_right_logical(x, d.bit_length() - 1)


def _layer_norm(z, g, b):
    mu = jnp.mean(z, axis=-1, keepdims=True)
    zc = z - mu
    var = jnp.mean(zc * zc, axis=-1, keepdims=True)
    return zc * lax.rsqrt(var + LN_EPS) * g + b


def _proj_kernel(x_ref, w_ref, o_ref, *, tn):
    xb = x_ref[...].astype(BF16)
    for j in range(o_ref.shape[1] // tn):
        cs = slice(j * tn, (j + 1) * tn)
        o_ref[:, cs] = jnp.dot(xb, w_ref[:, cs], preferred_element_type=F32)


def _proj(x, w, tm, tn):
    n, d = x.shape
    nout = w.shape[1]
    return pl.pallas_call(
        functools.partial(_proj_kernel, tn=tn),
        grid=(n // tm,),
        in_specs=[pl.BlockSpec((tm, d), lambda i: (i, 0)),
                  pl.BlockSpec((d, nout), lambda i: (0, 0))],
        out_specs=pl.BlockSpec((tm, nout), lambda i: (i, 0)),
        out_shape=jax.ShapeDtypeStruct((n, nout), F32),
        compiler_params=_cparams(1),
        name="proj",
    )(x, w)


def _rglru_kernel(x_ref, cbuf_ref, h0_ref, cw_ref, cb_ref, wbd_ref, ba_ref, bx_ref, lam_ref,
                  ya_ref, cnew_ref, hnew_ref, xbuf, hcar, *, bb, tc):
    ti = pl.program_id(1)
    nt = pl.num_programs(1)
    c_dim = x_ref.shape[-1]
    m = bb * tc

    @pl.when(ti == 0)
    def _():
        xbuf[:, 0:SUBLANES, :] = cbuf_ref[...]
        hcar[...] = h0_ref[...]

    x = x_ref[...]
    xbuf[:, SUBLANES:SUBLANES + tc, :] = x
    cw = cw_ref[...]
    xc = cb_ref[...] + cw[0:1] * xbuf[:, 5:5 + tc, :]
    xc = xc + cw[1:2] * xbuf[:, 6:6 + tc, :]
    xc = xc + cw[2:3] * xbuf[:, 7:7 + tc, :]
    xc = xc + cw[3:4] * x
    xc2 = xc.reshape(m, c_dim)
    xcb = xc2.astype(BF16)

    nl = -lam_ref[...]
    softplus = jnp.maximum(nl, 0.0) + jnp.log1p(jnp.exp(-jnp.abs(nl)))
    c_row = -LRU_C * softplus
    row = lax.broadcasted_iota(jnp.int32, (m, MXU_DIM), 0) & (tc - 1)
    hc = hcar[...]
    for g in range(c_dim // MXU_DIM):
        sl = slice(g * MXU_DIM, (g + 1) * MXU_DIM)
        ga = jnp.dot(xcb[:, sl], wbd_ref[g], preferred_element_type=F32)
        r = jax.nn.sigmoid(ga[:, :MXU_DIM] + ba_ref[:, sl])
        i = jax.nn.sigmoid(ga[:, MXU_DIM:] + bx_ref[:, sl])
        log_a = c_row[:, sl] * r
        a = jnp.exp(log_a)
        mult = jnp.sqrt(jnp.maximum(-jnp.tanh(log_a) * (a * a + 1.0), 0.0))
        b = mult * (i * xc2[:, sl])
        s = 1
        while s < tc:
            msk = row >= s
            a_sh = jnp.where(msk, pltpu.roll(a, s, 0), 1.0)
            b_sh = jnp.where(msk, pltpu.roll(b, s, 0), 0.0)
            b = a * b_sh + b
            a = a * a_sh
            s *= 2
        hcg = jnp.broadcast_to(hc[:, :, sl], (bb, tc, MXU_DIM)).reshape(m, MXU_DIM)
        h3 = (b + a * hcg).reshape(bb, tc, MXU_DIM)
        ya_ref[:, :, sl] = h3.astype(ya_ref.dtype)
        hcar[:, :, sl] = h3[:, tc - 1:tc, :]

    xbuf[:, 0:SUBLANES, :] = xbuf[:, tc:tc + SUBLANES, :]

    @pl.when(ti == nt - 1)
    def _():
        cnew_ref[...] = xbuf[:, 5:8, :]
        hnew_ref[...] = hcar[...]


def _rglru(proj3, cbuf8, h0, lw, bb, tc):
    bsz, t, _ = proj3.shape
    c_dim = h0.shape[-1]
    full = lambda *shape: pl.BlockSpec(shape, lambda bi, ti: (0,) * len(shape))
    kern = functools.partial(_rglru_kernel, bb=bb, tc=tc)
    return pl.pallas_call(
        kern,
        grid=(bsz // bb, t // tc),
        in_specs=[pl.BlockSpec((bb, tc, c_dim), lambda bi, ti: (bi, ti, 0)),
                  pl.BlockSpec((bb, SUBLANES, c_dim), lambda bi, ti: (bi, 0, 0)),
                  pl.BlockSpec((bb, 1, c_dim), lambda bi, ti: (bi, 0, 0)),
                  full(CONV_WIDTH, c_dim), full(1, c_dim),
                  full(c_dim // MXU_DIM, MXU_DIM, 2 * MXU_DIM),
                  full(1, c_dim), full(1, c_dim), full(1, c_dim)],
        out_specs=[pl.BlockSpec((bb, tc, c_dim), lambda bi, ti: (bi, ti, 0)),
                   pl.BlockSpec((bb, CONV_WIDTH - 1, c_dim), lambda bi, ti: (bi, 0, 0)),
                   pl.BlockSpec((bb, 1, c_dim), lambda bi, ti: (bi, 0, 0))],
        out_shape=[jax.ShapeDtypeStruct((bsz, t, c_dim), BF16),
                   jax.ShapeDtypeStruct((bsz, CONV_WIDTH - 1, c_dim), F32),
                   jax.ShapeDtypeStruct((bsz, 1, c_dim), F32)],
        scratch_shapes=[pltpu.VMEM((bb, tc + SUBLANES, c_dim), F32),
                        pltpu.VMEM((bb, 1, c_dim), F32)],
        compiler_params=_cparams(2),
        name="rglru",
    )(proj3, cbuf8, h0, lw["conv_w"], lw["conv_b"], lw["rg_wbd"], lw["rg_b_a"], lw["rg_b_x"],
      lw["rg_lambda"])


def _cmul(ar, ai, br, bi):
    return ar * br - ai * bi, ar * bi + ai * br


def _s5_kernel(u_ref, h0r_ref, h0i_ref, lamr_ref, lami_ref, ldt_ref, wbr_ref, wbi_ref,
               cre_ref, cim_ref, d_ref, wglu_ref, bglu_ref,
               ys_ref, sr_ref, si_ref,
               wb_sc, apr_sc, api_sc, hr_sc, hi_sc, cr_sc, ci_sc, *, bb, tc, lane_chunk):
    bi = pl.program_id(0)
    ti = pl.program_id(1)
    nt = pl.num_programs(1)
    m = bb * tc
    n_state = hr_sc.shape[-1]
    d_ssm = u_ref.shape[-1]
    half = n_state // 2

    @pl.when((bi == 0) & (ti == 0))
    def _():
        dt = jnp.exp(ldt_ref[...])
        lr = lamr_ref[...]
        li = lami_ref[...]
        mag = jnp.exp(lr * dt)
        abr = mag * jnp.cos(li * dt)
        abi = mag * jnp.sin(li * dt)
        nr = abr - 1.0
        den = lr * lr + li * li
        zr = (nr * lr + abi * li) / den
        zi = (abi * lr - nr * li) / den
        for kb in range(2):
            ks = slice(kb * half, (kb + 1) * half)
            br = wbr_ref[kb]
            bim = wbi_ref[kb]
            wb_sc[kb, :, 0:half] = (zr[:, ks] * br - zi[:, ks] * bim).astype(BF16)
            wb_sc[kb, :, half:n_state] = (zr[:, ks] * bim + zi[:, ks] * br).astype(BF16)
        row8 = lax.broadcasted_iota(jnp.int32, (SUBLANES, n_state), 0)
        pr = jnp.broadcast_to(abr, (SUBLANES, n_state))
        pi = jnp.broadcast_to(abi, (SUBLANES, n_state))
        for s in (1, 2, 4):
            msk = row8 >= s
            qr = jnp.where(msk, pltpu.roll(pr, s, 0), 1.0)
            qi = jnp.where(msk, pltpu.roll(pi, s, 0), 0.0)
            pr, pi = _cmul(pr, pi, qr, qi)
        apr_sc[...] = pr
        api_sc[...] = pi

    @pl.when(ti == 0)
    def _():
        cr_sc[...] = h0r_ref[...]
        ci_sc[...] = h0i_ref[...]

    u = u_ref[...].reshape(m, d_ssm)
    ub = u.astype(BF16)
    k_half = d_ssm // 2
    for kb in range(2):
        bu = jnp.dot(ub[:, kb * k_half:(kb + 1) * k_half], wb_sc[kb], preferred_element_type=F32)
        hr_sc[:, :, kb * half:(kb + 1) * half] = bu[:, :half].reshape(bb, tc, half)
        hi_sc[:, :, kb * half:(kb + 1) * half] = bu[:, half:].reshape(bb, tc, half)

    rows8 = bb * SUBLANES
    row = lax.broadcasted_iota(jnp.int32, (rows8, lane_chunk), 0) & (SUBLANES - 1)
    for lc in range(n_state // lane_chunk):
        ls = slice(lc * lane_chunk, (lc + 1) * lane_chunk)
        pr = apr_sc[:, ls]
        pi = api_sc[:, ls]
        steps = ((1, pr[0:1], pi[0:1]), (2, pr[1:2], pi[1:2]), (4, pr[3:4], pi[3:4]))
        prt = jnp.broadcast_to(pr[None], (bb, SUBLANES, lane_chunk)).reshape(rows8, lane_chunk)
        pit = jnp.broadcast_to(pi[None], (bb, SUBLANES, lane_chunk)).reshape(rows8, lane_chunk)

        def body(j, carry, ls=ls, steps=steps, prt=prt, pit=pit):
            cr, ci = carry
            r0 = pl.multiple_of(j * SUBLANES, SUBLANES)
            xr = hr_sc[:, pl.ds(r0, SUBLANES), ls].reshape(rows8, lane_chunk)
            xi = hi_sc[:, pl.ds(r0, SUBLANES), ls].reshape(rows8, lane_chunk)
            for s, ar, ai in steps:
                msk = row >= s
                sr = jnp.where(msk, pltpu.roll(xr, s, 0), 0.0)
                si = jnp.where(msk, pltpu.roll(xi, s, 0), 0.0)
                dr, di = _cmul(ar, ai, sr, si)
                xr = xr + dr
                xi = xi + di
            crb = jnp.broadcast_to(cr, (bb, SUBLANES, lane_chunk)).reshape(rows8, lane_chunk)
            cib = jnp.broadcast_to(ci, (bb, SUBLANES, lane_chunk)).reshape(rows8, lane_chunk)
            dr, di = _cmul(prt, pit, crb, cib)
            xr3 = (xr + dr).reshape(bb, SUBLANES, lane_chunk)
            xi3 = (xi + di).reshape(bb, SUBLANES, lane_chunk)
            hr_sc[:, pl.ds(r0, SUBLANES), ls] = xr3
            hi_sc[:, pl.ds(r0, SUBLANES), ls] = xi3
            return xr3[:, SUBLANES - 1:SUBLANES, :], xi3[:, SUBLANES - 1:SUBLANES, :]

        cr, ci = lax.fori_loop(0, tc // SUBLANES, body, (cr_sc[:, :, ls], ci_sc[:, :, ls]))
        cr_sc[:, :, ls] = cr
        ci_sc[:, :, ls] = ci

    n_pack = cre_ref.shape[0]
    k_pack = n_state // n_pack
    parts = []
    for p4 in range(n_pack):
        ks = slice(p4 * k_pack, (p4 + 1) * k_pack)
        hrb = hr_sc[:, :, ks].reshape(m, k_pack).astype(BF16)
        hib = hi_sc[:, :, ks].reshape(m, k_pack).astype(BF16)
        parts.append(jnp.dot(hrb, cre_ref[p4], preferred_element_type=F32)
                     - jnp.dot(hib, cim_ref[p4], preferred_element_type=F32))
    y = jnp.concatenate(parts, axis=1) + d_ref[...] * u
    g = y * (0.5 * (1.0 + jnp.tanh(np.sqrt(2.0 / np.pi).astype(np.float32) * (y + 0.044715 * (y * y * y)))))
    z = jnp.dot(g.astype(BF16), wglu_ref[...], preferred_element_type=F32) + bglu_ref[...]
    out = g * jax.nn.sigmoid(z)
    ys_ref[...] = out.reshape(bb, tc, d_ssm).astype(ys_ref.dtype)

    @pl.when(ti == nt - 1)
    def _():
        sr_ref[...] = cr_sc[...]
        si_ref[...] = ci_sc[...]


def _s5(proj3, h0r, h0i, lw, bb, tc, col_block):
    bsz, t, _ = proj3.shape
    n_state = h0r.shape[-1]
    d_ssm = lw["s5_d"].shape[-1]
    n_pack = lw["s5_cre"].shape[0]
    full = lambda *shape: pl.BlockSpec(shape, lambda bi, ti: (0,) * len(shape))
    lane_chunk = max(LANES, 4 * LANES // bb)
    kern = functools.partial(_s5_kernel, bb=bb, tc=tc, lane_chunk=lane_chunk)
    state_spec = pl.BlockSpec((bb, 1, n_state), lambda bi, ti: (bi, 0, 0))
    return pl.pallas_call(
        kern,
        grid=(bsz // bb, t // tc),
        in_specs=[pl.BlockSpec((bb, tc, d_ssm), lambda bi, ti: (bi, ti, col_block)),
                  state_spec, state_spec,
                  full(1, n_state), full(1, n_state), full(1, n_state),
                  full(2, d_ssm // 2, n_state // 2), full(2, d_ssm // 2, n_state // 2),
                  full(n_pack, n_state // n_pack, d_ssm // n_pack),
                  full(n_pack, n_state // n_pack, d_ssm // n_pack),
                  full(1, d_ssm), full(d_ssm, d_ssm), full(1, d_ssm)],
        out_specs=[pl.BlockSpec((bb, tc, d_ssm), lambda bi, ti: (bi, ti, 0)), state_spec, state_spec],
        out_shape=[jax.ShapeDtypeStruct((bsz, t, d_ssm), BF16),
                   jax.ShapeDtypeStruct((bsz, 1, n_state), F32),
                   jax.ShapeDtypeStruct((bsz, 1, n_state), F32)],
        scratch_shapes=[pltpu.VMEM((2, d_ssm // 2, n_state), BF16),
                        pltpu.VMEM((SUBLANES, n_state), F32), pltpu.VMEM((SUBLANES, n_state), F32),
                        pltpu.VMEM((bb, tc, n_state), F32), pltpu.VMEM((bb, tc, n_state), F32),
                        pltpu.VMEM((bb, 1, n_state), F32), pltpu.VMEM((bb, 1, n_state), F32)],
        compiler_params=_cparams(2),
        name="s5",
    )(proj3, h0r, h0i, lw["s5_lam_re"], lw["s5_lam_im"], lw["s5_log_dt"], lw["s5_wbr"], lw["s5_wbi"],
      lw["s5_cre"], lw["s5_cim"], lw["s5_d"], lw["s5_w_glu"], lw["s5_b_glu"])


def _rope_kernel(q_ref, k_ref, v_ref, cos_ref, sin_ref, *rest, gate, blocks_per_seq):
    if gate:
        qs_ref, kr_ref, vr_ref, kb_ref, vb_ref, sel_ref, km_sc = rest
    else:
        qs_ref, kr_ref, vr_ref = rest
    tm, d_attn = q_ref.shape
    reps = d_attn // LANES
    cos = jnp.concatenate([cos_ref[...]] * reps, axis=1)
    sin = jnp.concatenate([sin_ref[...]] * reps, axis=1)
    lane = lax.broadcasted_iota(jnp.int32, (tm, d_attn), 1)
    first_half = (lane & (HEAD_DIM - 1)) < HEAD_DIM // 2

    def rot(x):
        partner = jnp.where(first_half, pltpu.roll(x, d_attn - HEAD_DIM // 2, 1),
                            pltpu.roll(x, HEAD_DIM // 2, 1))
        return x * cos + partner * sin

    qr = rot(q_ref[...])
    kr = rot(k_ref[...])
    v = v_ref[...]
    kr_ref[...] = kr
    vr_ref[...] = v
    if not gate:
        qs_ref[...] = qr
        return
    qs_ref[...] = (qr * (HEAD_DIM ** -0.5)).T.astype(BF16)
    kb_ref[...] = kr.astype(BF16)
    vb_ref[...] = v.T.astype(BF16)

    tb = pl.program_id(0) % blocks_per_seq

    @pl.when(tb == 0)
    def _():
        km_sc[...] = jnp.zeros_like(km_sc)

    gates = lax.dot_general(km_sc[...], qr, _NT, precision=lax.Precision.HIGHEST,
                            preferred_element_type=F32)
    slot = lax.broadcasted_iota(jnp.int32, (GATE_SLOTS, tm), 0)
    slotf = slot.astype(F32)
    for h in range(ATTN_HEADS):
        hs = slice(h * GATE_SLOTS, (h + 1) * GATE_SLOTS)
        gh = jnp.where(slot < tb, gates[hs, :], -jnp.inf)
        sel = jnp.zeros((GATE_SLOTS, tm), F32)
        for _ in range(MOBA_TOPK):
            mx = jnp.max(gh, axis=0, keepdims=True)
            cand = (gh == mx) & (mx > -jnp.inf)
            first = jnp.min(jnp.where(cand, slotf, float(GATE_SLOTS)), axis=0, keepdims=True)
            pick = slotf == first
            sel = jnp.where(pick, 1.0, sel)
            gh = jnp.where(pick, -jnp.inf, gh)
        sel_ref[hs, :] = sel

    km = jnp.sum(kr, axis=0, keepdims=True) * (1.0 / MOBA_BLOCK)
    lane_row = lax.broadcasted_iota(jnp.int32, (1, d_attn), 1)
    for h in range(ATTN_HEADS):
        km_sc[pl.ds(h * GATE_SLOTS + tb, 1), :] = jnp.where(_div_pow2(lane_row, HEAD_DIM) == h, km, 0.0)


def _rope(proj2, cos_t, sin_t, tm, gate, blocks_per_seq, col0):
    n = proj2.shape[0]
    d_attn = ATTN_HEADS * HEAD_DIM
    n_tab = cos_t.shape[0] // tm
    qkv_spec = lambda c: pl.BlockSpec((tm, d_attn), lambda i, c=c: (i, c))
    tab_spec = pl.BlockSpec((tm, LANES), lambda i: (i % n_tab, 0))
    row_spec = pl.BlockSpec((tm, d_attn), lambda i: (i, 0))
    kern = functools.partial(_rope_kernel, gate=gate, blocks_per_seq=blocks_per_seq)
    if gate:
        n_col = ATTN_HEADS * GATE_SLOTS
        bsz = n // (tm * blocks_per_seq)
        t = tm * blocks_per_seq
        nb = blocks_per_seq
        col_spec = lambda rows: pl.BlockSpec((None, rows, tm), lambda i: (i // nb, 0, i % nb))
        out_specs = [col_spec(d_attn), row_spec, row_spec, row_spec, col_spec(d_attn), col_spec(n_col)]
        out_shape = [jax.ShapeDtypeStruct((bsz, d_attn, t), BF16), jax.ShapeDtypeStruct((n, d_attn), F32),
                     jax.ShapeDtypeStruct((n, d_attn), F32), jax.ShapeDtypeStruct((n, d_attn), BF16),
                     jax.ShapeDtypeStruct((bsz, d_attn, t), BF16), jax.ShapeDtypeStruct((bsz, n_col, t), F32)]
        scratch = [pltpu.VMEM((n_col, d_attn), F32)]
    else:
        out_specs = [row_spec] * 3
        out_shape = [jax.ShapeDtypeStruct((n, d_attn), F32)] * 3
        scratch = []
    return pl.pallas_call(
        kern,
        grid=(n // tm,),
        in_specs=[qkv_spec(col0), qkv_spec(col0 + 1), qkv_spec(col0 + 2), tab_spec, tab_spec],
        out_specs=out_specs,
        out_shape=out_shape,
        scratch_shapes=scratch,
        compiler_params=_cparams(1),
        name="rope_gate" if gate else "rope",
    )(proj2, proj2, proj2, cos_t, sin_t)


def _attn_kernel(qt_ref, k_ref, vt_ref, sel_ref, o_ref, acc_sc):
    qi = pl.program_id(2)
    width, blk = qt_ref.shape
    nh = width // HEAD_DIM
    qt = qt_ref[...]
    row_head = _div_pow2(lax.broadcasted_iota(jnp.int32, qt.shape, 0), HEAD_DIM)
    zero = jnp.zeros_like(qt)
    q_heads = jnp.concatenate([jnp.where(row_head == e, qt, zero) for e in range(nh)], axis=1)
    keyi = lax.broadcasted_iota(jnp.int32, (blk, blk), 0)
    qcol = lax.broadcasted_iota(jnp.int32, (blk, blk), 1)
    causal = keyi <= qcol

    r0 = pl.multiple_of(qi * blk, blk)
    ko = k_ref[pl.ds(r0, blk), :]
    vo = vt_ref[:, pl.ds(r0, blk)]
    s_own = jnp.dot(ko, q_heads, preferred_element_type=F32)
    stats = []
    for e in range(nh):
        s = jnp.where(causal, s_own[:, e * blk:(e + 1) * blk], NEG_INF)
        mx = jnp.max(s, axis=0, keepdims=True)
        p = jnp.exp(s - mx)
        stats += [mx, jnp.sum(p, axis=0, keepdims=True)]
        acc_sc[e] = jnp.dot(vo[e * HEAD_DIM:(e + 1) * HEAD_DIM, :], p.astype(BF16), preferred_element_type=F32)

    def body(j, carry):
        rn = pl.multiple_of(j * (2 * blk), 2 * blk)
        kn = k_ref[pl.ds(rn, 2 * blk), :]
        vn = vt_ref[:, pl.ds(rn, 2 * blk)]
        s_all = jnp.dot(kn, q_heads, preferred_element_type=F32)
        out = []
        for e in range(nh):
            mx, l = carry[2 * e], carry[2 * e + 1]
            s = s_all[:, e * blk:(e + 1) * blk]
            keep_a = sel_ref[pl.ds(e * GATE_SLOTS + 2 * j, 1), :] > 0.5
            keep_b = sel_ref[pl.ds(e * GATE_SLOTS + 2 * j + 1, 1), :] > 0.5
            sa = jnp.where(keep_a, s[:blk], NEG_INF)
            sb = jnp.where(keep_b, s[blk:], NEG_INF)
            mn = jnp.maximum(mx, jnp.maximum(jnp.max(sa, axis=0, keepdims=True),
                                             jnp.max(sb, axis=0, keepdims=True)))
            alpha = jnp.exp(mx - mn)
            pa = jnp.exp(sa - mn)
            pb = jnp.exp(sb - mn)
            l = alpha * l + jnp.sum(pa, axis=0, keepdims=True) + jnp.sum(pb, axis=0, keepdims=True)
            p = jnp.concatenate([pa, pb], axis=0).astype(BF16)
            acc_sc[e] = alpha * acc_sc[e] + jnp.dot(vn[e * HEAD_DIM:(e + 1) * HEAD_DIM, :], p,
                                                    preferred_element_type=F32)
            out += [mn, l]
        return tuple(out)

    res = lax.fori_loop(0, (qi + 1) // 2, body, tuple(stats))
    ot = jnp.concatenate([acc_sc[e] / res[2 * e + 1] for e in range(nh)], axis=0)
    o_ref[...] = ot.T.astype(o_ref.dtype)


def _attn_prompt(qt, kb, vt, selt, bsz, t):
    d_attn = ATTN_HEADS * HEAD_DIM
    nq = t // MOBA_BLOCK
    kb3 = kb.reshape(bsz, t, d_attn)
    width = MXU_DIM
    nh = width // HEAD_DIM
    return pl.pallas_call(
        _attn_kernel,
        grid=(bsz, d_attn // width, nq),
        in_specs=[pl.BlockSpec((None, width, MOBA_BLOCK), lambda b, hg, qi: (b, hg, qi)),
                  pl.BlockSpec((None, t, width), lambda b, hg, qi: (b, 0, hg)),
                  pl.BlockSpec((None, width, t), lambda b, hg, qi: (b, hg, 0)),
                  pl.BlockSpec((None, nh * GATE_SLOTS, MOBA_BLOCK), lambda b, hg, qi: (b, hg, qi))],
        out_specs=pl.BlockSpec((MOBA_BLOCK, width), lambda b, hg, qi: (b * nq + qi, hg)),
        out_shape=jax.ShapeDtypeStruct((bsz * t, d_attn), BF16),
        scratch_shapes=[pltpu.VMEM((nh, HEAD_DIM, MOBA_BLOCK), F32)],
        compiler_params=_cparams(3),
        name="attn_prompt",
    )(qt, kb3, vt, selt)


def _sattn_kernel(pt_ref, q_ref, kn_ref, vn_ref, k0_ref, k1_ref, v0_ref, v1_ref, o_ref,
                  g_sc, m_sc, l_sc, o_sc, *, nblk):
    n = pl.program_id(1)
    tq, d_attn = q_ref.shape
    page = k0_ref.shape[0]
    rows = ATTN_HEADS * tq

    def heads_to_rows(x):
        return jnp.concatenate([x[:, h * HEAD_DIM:(h + 1) * HEAD_DIM] for h in range(ATTN_HEADS)], axis=0)

    qall = heads_to_rows(q_ref[...])
    qs = (qall * (HEAD_DIM ** -0.5)).astype(BF16)
    lane = lax.broadcasted_iota(jnp.int32, (rows, LANES), 1)
    row_head = _div_pow2(lax.broadcasted_iota(jnp.int32, (rows, 1), 0), tq)

    k0 = k0_ref[...]
    k1 = k1_ref[...]
    kf = jnp.concatenate([k0.reshape(page * ATTN_HEADS, HEAD_DIM), k1.reshape(page * ATTN_HEADS, HEAD_DIM)],
                         axis=0).astype(BF16)
    vf = jnp.concatenate([v0_ref[...].reshape(page * ATTN_HEADS, HEAD_DIM),
                          v1_ref[...].reshape(page * ATTN_HEADS, HEAD_DIM)], axis=0).astype(BF16)
    kmean = (jnp.sum(k0, axis=0) + jnp.sum(k1, axis=0)) * (1.0 / MOBA_BLOCK)
    kmean_rows = jnp.broadcast_to(kmean[:, None, :], (ATTN_HEADS, tq, HEAD_DIM)).reshape(rows, HEAD_DIM)
    gate = jnp.sum(qall * kmean_rows, axis=1, keepdims=True)
    s = lax.dot_general(qs, kf, _NT, preferred_element_type=F32)
    key_head = lax.broadcasted_iota(jnp.int32, s.shape, 1) & (ATTN_HEADS - 1)
    s = jnp.where(key_head == row_head, s, NEG_INF)
    mx = jnp.max(s, axis=1, keepdims=True)
    p = jnp.exp(s - mx)
    l = jnp.sum(p, axis=1, keepdims=True)
    o = jnp.dot(p.astype(BF16), vf, preferred_element_type=F32)

    @pl.when(n == 0)
    def _():
        g_sc[...] = jnp.full_like(g_sc, -jnp.inf)
        m_sc[...] = jnp.zeros_like(m_sc)
        l_sc[...] = jnp.zeros_like(l_sc)

    hit = lane == n
    g_sc[...] = jnp.where(hit, gate, g_sc[...])
    m_sc[...] = jnp.where(hit, mx, m_sc[...])
    l_sc[...] = jnp.where(hit, l, l_sc[...])
    o_sc[n] = o

    @pl.when(n == nblk - 1)
    def _():
        gh = jnp.where(lane < nblk, g_sc[...], -jnp.inf)
        lanef = lane.astype(F32)
        chosen = jnp.zeros((rows, LANES), F32)
        for _ in range(MOBA_TOPK):
            gmx = jnp.max(gh, axis=1, keepdims=True)
            cand = (gh == gmx) & (gmx > -jnp.inf)
            first = jnp.min(jnp.where(cand, lanef, float(LANES)), axis=1, keepdims=True)
            pick = lanef == first
            chosen = jnp.where(pick, 1.0, chosen)
            gh = jnp.where(pick, -jnp.inf, gh)
        selw = chosen > 0.5
        m_all = m_sc[...]
        l_all = l_sc[...]
        knew = heads_to_rows(kn_ref[...]).astype(BF16)
        vnew = heads_to_rows(vn_ref[...]).astype(BF16)
        s_own = lax.dot_general(qs, knew, _NT, preferred_element_type=F32)
        ri = lax.broadcasted_iota(jnp.int32, (rows, rows), 0)
        ci = lax.broadcasted_iota(jnp.int32, (rows, rows), 1)
        ok = (_div_pow2(ri, tq) == _div_pow2(ci, tq)) & ((ci & (tq - 1)) <= (ri & (tq - 1)))
        s_own = jnp.where(ok, s_own, NEG_INF)
        m_sel = jnp.max(jnp.where(selw, m_all, -jnp.inf), axis=1, keepdims=True)
        mf = jnp.maximum(jnp.max(s_own, axis=1, keepdims=True), m_sel)
        w = jnp.where(selw, jnp.exp(m_all - mf), 0.0)
        p_own = jnp.exp(s_own - mf)
        lf = jnp.sum(w * l_all, axis=1, keepdims=True) + jnp.sum(p_own, axis=1, keepdims=True)
        of = jnp.dot(p_own.astype(BF16), vnew, preferred_element_type=F32)
        for nn in range(nblk):
            of = of + w[:, nn:nn + 1] * o_sc[nn]
        out = of / lf
        o_ref[...] = jnp.concatenate([out[h * tq:(h + 1) * tq, :] for h in range(ATTN_HEADS)],
                                     axis=1).astype(o_ref.dtype)


def _attn_sample(qr3, kr3, vr3, cache_k, cache_v, page_table, layer):
    bsz, tq, d_attn = qr3.shape
    page = cache_k.shape[2]
    pages_per_blk = MOBA_BLOCK // page
    assert pages_per_blk == 2 and cache_k.shape[3:] == (ATTN_HEADS, HEAD_DIM) and ATTN_HEADS == SUBLANES
    nblk = page_table.shape[1] // pages_per_blk
    rows = ATTN_HEADS * tq
    tok_spec = pl.BlockSpec((None, tq, d_attn), lambda b, n, pt: (b, 0, 0))

    def page_spec(j):
        return pl.BlockSpec((None, None, page, ATTN_HEADS, HEAD_DIM),
                            lambda b, n, pt, j=j: (layer, pt[b, pages_per_blk * n + j], 0, 0, 0))

    grid_spec = pltpu.PrefetchScalarGridSpec(
        num_scalar_prefetch=1,
        grid=(bsz, nblk),
        in_specs=[tok_spec, tok_spec, tok_spec, page_spec(0), page_spec(1), page_spec(0), page_spec(1)],
        out_specs=pl.BlockSpec((None, tq, d_attn), lambda b, n, pt: (b, 0, 0)),
        scratch_shapes=[pltpu.VMEM((rows, LANES), F32), pltpu.VMEM((rows, LANES), F32),
                        pltpu.VMEM((rows, LANES), F32), pltpu.VMEM((nblk, rows, HEAD_DIM), F32)],
    )
    return pl.pallas_call(
        functools.partial(_sattn_kernel, nblk=nblk),
        grid_spec=grid_spec,
        out_shape=jax.ShapeDtypeStruct((bsz, tq, d_attn), BF16),
        compiler_params=_cparams(2),
        name="attn_sample",
    )(page_table, qr3, kr3, vr3, cache_k, cache_k, cache_v, cache_v)


def _merge_kernel(x_ref, ya_ref, ys_ref, yc_ref, g0_ref, g1_ref, g2_ref,
                  wr_ref, ws_ref, wa_ref, wo_ref, lg_ref, lb_ref, o_ref, *, alpha):
    merged = jax.nn.sigmoid(g0_ref[...]) * jnp.dot(ya_ref[...], wr_ref[...], preferred_element_type=F32)
    merged = merged + jax.nn.sigmoid(g1_ref[...]) * jnp.dot(ys_ref[...], ws_ref[...],
                                                           preferred_element_type=F32)
    merged = merged + jax.nn.sigmoid(g2_ref[...]) * jnp.dot(yc_ref[...], wa_ref[...],
                                                           preferred_element_type=F32)
    z = alpha * x_ref[...] + jnp.dot(merged.astype(BF16), wo_ref[...], preferred_element_type=F32)
    o_ref[...] = _layer_norm(z, lg_ref[...], lb_ref[...])


def _merge(x, ya, ys, yc, proj2, lw, tm, gate_col0, alpha):
    n, d = x.shape
    row = lambda w: pl.BlockSpec((tm, w), lambda i: (i, 0))
    full = lambda a: pl.BlockSpec(a.shape, lambda i: (0,) * a.ndim)
    gspec = lambda c: pl.BlockSpec((tm, d), lambda i, c=c: (i, c))
    ws = [lw["w_br_rnn"], lw["w_br_ssm"], lw["w_br_attn"], lw["w_out"], lw["ln1_g"], lw["ln1_b"]]
    return pl.pallas_call(
        functools.partial(_merge_kernel, alpha=alpha),
        grid=(n // tm,),
        in_specs=[row(d), row(ya.shape[1]), row(ys.shape[1]), row(yc.shape[1]),
                  gspec(gate_col0), gspec(gate_col0 + 1), gspec(gate_col0 + 2)] + [full(a) for a in ws],
        out_specs=row(d),
        out_shape=jax.ShapeDtypeStruct((n, d), F32),
        compiler_params=_cparams(1),
        name="merge_ln",
    )(x, ya, ys, yc, proj2, proj2, proj2, *ws)


def _ffn_kernel(x_ref, wg_ref, wu_ref, wo_ref, lg_ref, lb_ref, o_ref, xb_sc, acc_sc, *, alpha):
    j = pl.program_id(1)

    @pl.when(j == 0)
    def _():
        xb_sc[...] = x_ref[...].astype(BF16)
        acc_sc[...] = jnp.zeros_like(acc_sc)

    xb = xb_sc[...]
    hg = jnp.dot(xb, wg_ref[...], preferred_element_type=F32)
    hu = jnp.dot(xb, wu_ref[...], preferred_element_type=F32)
    h = (hg * jax.nn.sigmoid(hg)) * hu
    acc_sc[...] += jnp.dot(h.astype(BF16), wo_ref[...], preferred_element_type=F32)

    @pl.when(j == pl.num_programs(1) - 1)
    def _():
        o_ref[...] = _layer_norm(alpha * x_ref[...] + acc_sc[...], lg_ref[...], lb_ref[...])


def _ffn(x, lw, tm, tf, alpha):
    n, d = x.shape
    d_ff = lw["w_ffn_out"].shape[0]
    nj = d_ff // tf
    return pl.pallas_call(
        functools.partial(_ffn_kernel, alpha=alpha),
        grid=(n // tm, nj),
        in_specs=[pl.BlockSpec((tm, d), lambda i, j: (i, 0)),
                  pl.BlockSpec((d, tf), lambda i, j: (0, j)),
                  pl.BlockSpec((d, tf), lambda i, j: (0, nj + j)),
                  pl.BlockSpec((tf, d), lambda i, j: (j, 0)),
                  pl.BlockSpec((1, d), lambda i, j: (0, 0)),
                  pl.BlockSpec((1, d), lambda i, j: (0, 0))],
        out_specs=pl.BlockSpec((tm, d), lambda i, j: (i, 0)),
        out_shape=jax.ShapeDtypeStruct((n, d), F32),
        scratch_shapes=[pltpu.VMEM((tm, d), BF16), pltpu.VMEM((tm, d), F32)],
        compiler_params=_cparams(2),
        name="ffn_ln",
    )(x, lw["w_ffn_in"], lw["w_ffn_in"], lw["w_ffn_out"], lw["ln2_g"], lw["ln2_b"])


def _block_diag(w, per_block):
    n, r, c = w.shape
    eye = jnp.eye(per_block, dtype=w.dtype)
    out = jnp.einsum("kgrc,gh->kgrhc", w.reshape(n // per_block, per_block, r, c), eye)
    return out.reshape(n // per_block, per_block * r, per_block * c)


def _prep_layer(l, p):
    heads_per_tile = MXU_DIM // (p["rg_w_a"].shape[-1])
    row = lambda a: a[l].reshape(1, -1)
    groups = p["s5_b_re"].shape[1]
    lw = {
        "w_in": p["w_in"][l].astype(BF16),
        "conv_w": p["conv_w"][l],
        "conv_b": row(p["conv_b"]),
        "rg_wbd": jnp.concatenate([_block_diag(p["rg_w_a"][l], heads_per_tile),
                                   _block_diag(p["rg_w_x"][l], heads_per_tile)], axis=2).astype(BF16),
        "rg_b_a": row(p["rg_b_a"]), "rg_b_x": row(p["rg_b_x"]), "rg_lambda": row(p["rg_lambda"]),
        "s5_lam_re": row(p["s5_lambda_re"]), "s5_lam_im": row(p["s5_lambda_im"]),
        "s5_log_dt": jnp.repeat(p["s5_log_step"][l], SSM_STATE).reshape(1, -1),
        "s5_wbr": _block_diag(jnp.swapaxes(p["s5_b_re"][l], 1, 2), groups // 2),
        "s5_wbi": _block_diag(jnp.swapaxes(p["s5_b_im"][l], 1, 2), groups // 2),
        "s5_cre": _block_diag(jnp.swapaxes(p["s5_c_re"][l], 1, 2), LANES // SSM_GROUP).astype(BF16),
        "s5_cim": _block_diag(jnp.swapaxes(p["s5_c_im"][l], 1, 2), LANES // SSM_GROUP).astype(BF16),
        "s5_d": row(p["s5_d"]),
        "s5_w_glu": p["s5_w_glu"][l].astype(BF16),
        "s5_b_glu": row(p["s5_b_glu"]),
        "w_br_rnn": p["w_br_rnn"][l].astype(BF16),
        "w_br_ssm": p["w_br_ssm"][l].astype(BF16),
        "w_br_attn": p["w_br_attn"][l].astype(BF16),
        "w_out": p["w_out"][l].astype(BF16),
        "ln1_g": row(p["ln1_g"]), "ln1_b": row(p["ln1_b"]),
        "w_ffn_in": p["w_ffn_in"][l].astype(BF16),
        "w_ffn_out": p["w_ffn_out"][l].astype(BF16),
        "ln2_g": row(p["ln2_g"]), "ln2_b": row(p["ln2_b"]),
    }
    return lw


def _rope_tables(pos0, t):
    half = HEAD_DIM // 2
    inv = jnp.power(ROPE_THETA, -jnp.arange(half, dtype=F32) * (2.0 / HEAD_DIM))
    ang = (pos0 + jnp.arange(t)).astype(F32)[:, None] * inv
    cos, sin = jnp.cos(ang), jnp.sin(ang)
    reps = LANES // HEAD_DIM
    cos_t = jnp.tile(jnp.concatenate([cos, cos], axis=1), (1, reps))
    sin_t = jnp.tile(jnp.concatenate([-sin, sin], axis=1), (1, reps))
    return cos_t, sin_t


def _largest_tile(n, cap):
    t = min(n, cap)
    while n % t:
        t //= 2
    return t


def _trunk_layer(x3, lw, alpha, cbuf8, h0, s5r0, s5i0, *, prompt, pos0=0, cache=None):
    bsz, t, d = x3.shape
    n = bsz * t
    d_rnn = h0.shape[-1]
    d_ssm = lw["s5_d"].shape[-1]
    d_attn = ATTN_HEADS * HEAD_DIM
    x2 = x3.reshape(n, d)
    proj2 = _proj(x2, lw["w_in"], _largest_tile(n, 256), 1536)
    proj3 = proj2.reshape(bsz, t, -1)

    if prompt:
        bb, tc = 1, MOBA_BLOCK
    else:
        bb, tc = _largest_tile(bsz, 32), t
    assert tc % SUBLANES == 0 and tc & (tc - 1) == 0 and t % tc == 0
    ya, conv_new, h_new = _rglru(proj3, cbuf8, h0, lw, bb, tc)
    ys, s5r, s5i = _s5(proj3, s5r0, s5i0, lw, bb, tc, d_rnn // d_ssm)

    col0 = (d_rnn + d_ssm) // d_attn
    tm_rope = MOBA_BLOCK
    assert n % tm_rope == 0
    if prompt:
        assert t % MOBA_BLOCK == 0 and t // MOBA_BLOCK <= GATE_SLOTS
        cos_t, sin_t = _rope_tables(0, t)
        qt, kr, vr, kb, vt, selt = _rope(proj2, cos_t, sin_t, tm_rope, True, t // MOBA_BLOCK, col0)
        yc = _attn_prompt(qt, kb, vt, selt, bsz, t)
    else:
        assert tm_rope % t == 0
        cos_t, sin_t = _rope_tables(pos0, t)
        cos_t = jnp.tile(cos_t, (tm_rope // t, 1))
        sin_t = jnp.tile(sin_t, (tm_rope // t, 1))
        qr, kr, vr = _rope(proj2, cos_t, sin_t, tm_rope, False, 1, col0)
        cache_k, cache_v, page_table, layer = cache
        yc = _attn_sample(qr.reshape(bsz, t, d_attn), kr.reshape(bsz, t, d_attn), vr.reshape(bsz, t, d_attn),
                          cache_k, cache_v, page_table, layer).reshape(n, d_attn)

    gate_col0 = (d_rnn + d_ssm + 3 * d_attn) // d
    x1 = _merge(x2, ya.reshape(n, d_rnn), ys.reshape(n, d_ssm), yc, proj2, lw, _largest_tile(n, 256),
                gate_col0, alpha)
    x_out = _ffn(x1, lw, _largest_tile(n, 1024), MXU_DIM, alpha)
    return (x_out.reshape(bsz, t, d), kr.reshape(bsz, t, ATTN_HEADS, HEAD_DIM),
            vr.reshape(bsz, t, ATTN_HEADS, HEAD_DIM), conv_new, h_new.reshape(bsz, d_rnn), s5r, s5i)


def kernel(x_prompt, x_sample, cache_k, cache_v, state_conv, state_rglru, state_s5_re, state_s5_im, page_table,
           w_in, conv_w, conv_b, rg_w_a, rg_b_a, rg_w_x, rg_b_x, rg_lambda,
           s5_lambda_re, s5_lambda_im, s5_b_re, s5_b_im, s5_c_re, s5_c_im, s5_d, s5_log_step, s5_w_glu, s5_b_glu,
           w_br_rnn, w_br_ssm, w_br_attn, w_out, ln1_g, ln1_b, w_ffn_in, w_ffn_out, ln2_g, ln2_b):
    params = dict(w_in=w_in, conv_w=conv_w, conv_b=conv_b, rg_w_a=rg_w_a, rg_b_a=rg_b_a, rg_w_x=rg_w_x,
                  rg_b_x=rg_b_x, rg_lambda=rg_lambda, s5_lambda_re=s5_lambda_re, s5_lambda_im=s5_lambda_im,
                  s5_b_re=s5_b_re, s5_b_im=s5_b_im, s5_c_re=s5_c_re, s5_c_im=s5_c_im, s5_d=s5_d,
                  s5_log_step=s5_log_step, s5_w_glu=s5_w_glu, s5_b_glu=s5_b_glu, w_br_rnn=w_br_rnn,
                  w_br_ssm=w_br_ssm, w_br_attn=w_br_attn, w_out=w_out, ln1_g=ln1_g, ln1_b=ln1_b,
                  w_ffn_in=w_ffn_in, w_ffn_out=w_ffn_out, ln2_g=ln2_g, ln2_b=ln2_b)
    depth = w_in.shape[0]
    alpha = (2.0 * depth) ** 0.25
    bp = x_prompt.shape[0]
    bs = x_sample.shape[0]
    d_rnn = state_rglru.shape[-1]
    groups, n_p = state_s5_re.shape[-2:]
    n_state = groups * n_p
    n_pages = page_table.shape[1]
    page = cache_k.shape[2]
    past_len = n_pages * page
    d_attn = ATTN_HEADS * HEAD_DIM
    assert past_len % MOBA_BLOCK == 0 and x_sample.shape[1] <= MOBA_BLOCK

    zeros_p = lambda *s: jnp.zeros((bp,) + s, F32)
    yp, ys = x_prompt, x_sample
    outs_p, outs_s = [], []
    for l in range(depth):
        lw = _prep_layer(l, params)
        res = _trunk_layer(yp, lw, alpha, zeros_p(SUBLANES, d_rnn), zeros_p(1, d_rnn),
                           zeros_p(1, n_state), zeros_p(1, n_state), prompt=True)
        yp = res[0]
        outs_p.append(res[1:])
        cbuf8 = jnp.pad(state_conv[l], ((0, 0), (SUBLANES - (CONV_WIDTH - 1), 0), (0, 0)))
        res = _trunk_layer(ys, lw, alpha, cbuf8, state_rglru[l].reshape(bs, 1, d_rnn),
                           state_s5_re[l].reshape(bs, 1, n_state), state_s5_im[l].reshape(bs, 1, n_state),
                           prompt=False, pos0=past_len, cache=(cache_k, cache_v, page_table, l))
        ys = res[0]
        outs_s.append(res[1:])

    def stack(outs, i, shape=None):
        arrs = [o[i] if shape is None else o[i].reshape(shape) for o in outs]
        return jnp.stack(arrs)

    return (yp, ys,
            stack(outs_p, 0), stack(outs_p, 1), stack(outs_p, 2), stack(outs_p, 3),
            stack(outs_p, 4, (bp, groups, n_p)), stack(outs_p, 5, (bp, groups, n_p)),
            stack(outs_s, 0), stack(outs_s, 1), stack(outs_s, 2), stack(outs_s, 3),
            stack(outs_s, 4, (bs, groups, n_p)), stack(outs_s, 5, (bs, groups, n_p)))
```

```python
import functools

import jax
import jax.numpy as jnp
import numpy as np
from jax import lax
from jax.experimental import pallas as pl
from jax.experimental.pallas import tpu as pltpu

F32 = jnp.float32
BF16 = jnp.bfloat16

ATTN_HEADS = 8
HEAD_DIM = 64
RNN_HEADS = 16
CONV_WIDTH = 4
LRU_C = 8.0
SSM_GROUP = 16
SSM_STATE = 64
MOBA_BLOCK = 256
MOBA_TOPK = 3
ROPE_THETA = 10000.0
LN_EPS = 1e-5
NEG_INF = -1e30

LANES = 128
SUBLANES = 8
MXU_DIM = 256
VMEM_LIMIT_BYTES = 56 * 1024 * 1024

GATE_SLOTS = 32
_NT = (((1,), (1,)), ((), ()))


def _cparams(n_axes):
    return pltpu.CompilerParams(dimension_semantics=("arbitrary",) * n_axes,
                                vmem_limit_bytes=VMEM_LIMIT_BYTES)


def _div_pow2(x, d):
    assert d & (d - 1) == 0
    return lax.shift_right_logical(x, d.bit_length() - 1)


def _layer_norm(z, g, b):
    mu = jnp.mean(z, axis=-1, keepdims=True)
    zc = z - mu
    var = jnp.mean(zc * zc, axis=-1, keepdims=True)
    return zc * lax.rsqrt(var + LN_EPS) * g + b


def _proj_kernel(x_ref, w_ref, o_ref, *, tn):
    xb = x_ref[...].astype(BF16)
    for j in range(o_ref.shape[1] // tn):
        cs = slice(j * tn, (j + 1) * tn)
        o_ref[:, cs] = jnp.dot(xb, w_ref[:, cs], preferred_element_type=F32)


def _proj(x, w, tm, tn):
    n, d = x.shape
    nout = w.shape[1]
    return pl.pallas_call(
        functools.partial(_proj_kernel, tn=tn),
        grid=(n // tm,),
        in_specs=[pl.BlockSpec((tm, d), lambda i: (i, 0)),
                  pl.BlockSpec((d, nout), lambda i: (0, 0))],
        out_specs=pl.BlockSpec((tm, nout), lambda i: (i, 0)),
        out_shape=jax.ShapeDtypeStruct((n, nout), F32),
        compiler_params=_cparams(1),
        name="proj",
    )(x, w)


def _rglru_kernel(x_ref, cbuf_ref, h0_ref, cw_ref, cb_ref, wbd_ref, ba_ref, bx_ref, lam_ref,
                  ya_ref, cnew_ref, hnew_ref, xbuf, hcar, *, bb, tc):
    ti = pl.program_id(1)
    nt = pl.num_programs(1)
    c_dim = x_ref.shape[-1]
    m = bb * tc

    @pl.when(ti == 0)
    def _():
        xbuf[:, 0:SUBLANES, :] = cbuf_ref[...]
        hcar[...] = h0_ref[...]

    x = x_ref[...]
    xbuf[:, SUBLANES:SUBLANES + tc, :] = x
    cw = cw_ref[...]
    xc = cb_ref[...] + cw[0:1] * xbuf[:, 5:5 + tc, :]
    xc = xc + cw[1:2] * xbuf[:, 6:6 + tc, :]
    xc = xc + cw[2:3] * xbuf[:, 7:7 + tc, :]
    xc = xc + cw[3:4] * x
    xc2 = xc.reshape(m, c_dim)
    xcb = xc2.astype(BF16)

    nl = -lam_ref[...]
    softplus = jnp.maximum(nl, 0.0) + jnp.log1p(jnp.exp(-jnp.abs(nl)))
    c_row = -LRU_C * softplus
    row = lax.broadcasted_iota(jnp.int32, (m, MXU_DIM), 0) & (tc - 1)
    hc = hcar[...]
    for g in range(c_dim // MXU_DIM):
        sl = slice(g * MXU_DIM, (g + 1) * MXU_DIM)
        ga = jnp.dot(xcb[:, sl], wbd_ref[g], preferred_element_type=F32)
        r = jax.nn.sigmoid(ga[:, :MXU_DIM] + ba_ref[:, sl])
        i = jax.nn.sigmoid(ga[:, MXU_DIM:] + bx_ref[:, sl])
        log_a = c_row[:, sl] * r
        a = jnp.exp(log_a)
        mult = jnp.sqrt(jnp.maximum(-jnp.tanh(log_a) * (a * a + 1.0), 0.0))
        b = mult * (i * xc2[:, sl])
        s = 1
        while s < tc:
            msk = row >= s
            a_sh = jnp.where(msk, pltpu.roll(a, s, 0), 1.0)
            b_sh = jnp.where(msk, pltpu.roll(b, s, 0), 0.0)
            b = a * b_sh + b
            a = a * a_sh
            s *= 2
        hcg = jnp.broadcast_to(hc[:, :, sl], (bb, tc, MXU_DIM)).reshape(m, MXU_DIM)
        h3 = (b + a * hcg).reshape(bb, tc, MXU_DIM)
        ya_ref[:, :, sl] = h3.astype(ya_ref.dtype)
        hcar[:, :, sl] = h3[:, tc - 1:tc, :]

    xbuf[:, 0:SUBLANES, :] = xbuf[:, tc:tc + SUBLANES, :]

    @pl.when(ti == nt - 1)
    def _():
        cnew_ref[...] = xbuf[:, 5:8, :]
        hnew_ref[...] = hcar[...]


def _rglru(proj3, cbuf8, h0, lw, bb, tc):
    bsz, t, _ = proj3.shape
    c_dim = h0.shape[-1]
    full = lambda *shape: pl.BlockSpec(shape, lambda bi, ti: (0,) * len(shape))
    kern = functools.partial(_rglru_kernel, bb=bb, tc=tc)
    return pl.pallas_call(
        kern,
        grid=(bsz // bb, t // tc),
        in_specs=[pl.BlockSpec((bb, tc, c_dim), lambda bi, ti: (bi, ti, 0)),
                  pl.BlockSpec((bb, SUBLANES, c_dim), lambda bi, ti: (bi, 0, 0)),
                  pl.BlockSpec((bb, 1, c_dim), lambda bi, ti: (bi, 0, 0)),
                  full(CONV_WIDTH, c_dim), full(1, c_dim),
                  full(c_dim // MXU_DIM, MXU_DIM, 2 * MXU_DIM),
                  full(1, c_dim), full(1, c_dim), full(1, c_dim)],
        out_specs=[pl.BlockSpec((bb, tc, c_dim), lambda bi, ti: (bi, ti, 0)),
                   pl.BlockSpec((bb, CONV_WIDTH - 1, c_dim), lambda bi, ti: (bi, 0, 0)),
                   pl.BlockSpec((bb, 1, c_dim), lambda bi, ti: (bi, 0, 0))],
        out_shape=[jax.ShapeDtypeStruct((bsz, t, c_dim), BF16),
                   jax.ShapeDtypeStruct((bsz, CONV_WIDTH - 1, c_dim), F32),
                   jax.ShapeDtypeStruct((bsz, 1, c_dim), F32)],
        scratch_shapes=[pltpu.VMEM((bb, tc + SUBLANES, c_dim), F32),
                        pltpu.VMEM((bb, 1, c_dim), F32)],
        compiler_params=_cparams(2),
        name="rglru",
    )(proj3, cbuf8, h0, lw["conv_w"], lw["conv_b"], lw["rg_wbd"], lw["rg_b_a"], lw["rg_b_x"],
      lw["rg_lambda"])


def _cmul(ar, ai, br, bi):
    return ar * br - ai * bi, ar * bi + ai * br


def _s5_kernel(u_ref, h0r_ref, h0i_ref, lamr_ref, lami_ref, ldt_ref, wbr_ref, wbi_ref,
               cre_ref, cim_ref, d_ref, wglu_ref, bglu_ref,
               ys_ref, sr_ref, si_ref,
               wb_sc, apr_sc, api_sc, hr_sc, hi_sc, cr_sc, ci_sc, *, bb, tc, lane_chunk):
    bi = pl.program_id(0)
    ti = pl.program_id(1)
    nt = pl.num_programs(1)
    m = bb * tc
    n_state = hr_sc.shape[-1]
    d_ssm = u_ref.shape[-1]
    half = n_state // 2

    @pl.when((bi == 0) & (ti == 0))
    def _():
        dt = jnp.exp(ldt_ref[...])
        lr = lamr_ref[...]
        li = lami_ref[...]
        mag = jnp.exp(lr * dt)
        abr = mag * jnp.cos(li * dt)
        abi = mag * jnp.sin(li * dt)
        nr = abr - 1.0
        den = lr * lr + li * li
        zr = (nr * lr + abi * li) / den
        zi = (abi * lr - nr * li) / den
        for kb in range(2):
            ks = slice(kb * half, (kb + 1) * half)
            br = wbr_ref[kb]
            bim = wbi_ref[kb]
            wb_sc[kb, :, 0:half] = (zr[:, ks] * br - zi[:, ks] * bim).astype(BF16)
            wb_sc[kb, :, half:n_state] = (zr[:, ks] * bim + zi[:, ks] * br).astype(BF16)
        row8 = lax.broadcasted_iota(jnp.int32, (SUBLANES, n_state), 0)
        pr = jnp.broadcast_to(abr, (SUBLANES, n_state))
        pi = jnp.broadcast_to(abi, (SUBLANES, n_state))
        for s in (1, 2, 4):
            msk = row8 >= s
            qr = jnp.where(msk, pltpu.roll(pr, s, 0), 1.0)
            qi = jnp.where(msk, pltpu.roll(pi, s, 0), 0.0)
            pr, pi = _cmul(pr, pi, qr, qi)
        apr_sc[...] = pr
        api_sc[...] = pi

    @pl.when(ti == 0)
    def _():
        cr_sc[...] = h0r_ref[...]
        ci_sc[...] = h0i_ref[...]

    u = u_ref[...].reshape(m, d_ssm)
    ub = u.astype(BF16)
    k_half = d_ssm // 2
    for kb in range(2):
        bu = jnp.dot(ub[:, kb * k_half:(kb + 1) * k_half], wb_sc[kb], preferred_element_type=F32)
        hr_sc[:, :, kb * half:(kb + 1) * half] = bu[:, :half].reshape(bb, tc, half)
        hi_sc[:, :, kb * half:(kb + 1) * half] = bu[:, half:].reshape(bb, tc, half)

    rows8 = bb * SUBLANES
    row = lax.broadcasted_iota(jnp.int32, (rows8, lane_chunk), 0) & (SUBLANES - 1)
    for lc in range(n_state // lane_chunk):
        ls = slice(lc * lane_chunk, (lc + 1) * lane_chunk)
        pr = apr_sc[:, ls]
        pi = api_sc[:, ls]
        steps = ((1, pr[0:1], pi[0:1]), (2, pr[1:2], pi[1:2]), (4, pr[3:4], pi[3:4]))
        prt = jnp.broadcast_to(pr[None], (bb, SUBLANES, lane_chunk)).reshape(rows8, lane_chunk)
        pit = jnp.broadcast_to(pi[None], (bb, SUBLANES, lane_chunk)).reshape(rows8, lane_chunk)

        def body(j, carry, ls=ls, steps=steps, prt=prt, pit=pit):
            cr, ci = carry
            r0 = pl.multiple_of(j * SUBLANES, SUBLANES)
            xr = hr_sc[:, pl.ds(r0, SUBLANES), ls].reshape(rows8, lane_chunk)
            xi = hi_sc[:, pl.ds(r0, SUBLANES), ls].reshape(rows8, lane_chunk)
            for s, ar, ai in steps:
                msk = row >= s
                sr = jnp.where(msk, pltpu.roll(xr, s, 0), 0.0)
                si = jnp.where(msk, pltpu.roll(xi, s, 0), 0.0)
                dr, di = _cmul(ar, ai, sr, si)
                xr = xr + dr
                xi = xi + di
            crb = jnp.broadcast_to(cr, (bb, SUBLANES, lane_chunk)).reshape(rows8, lane_chunk)
            cib = jnp.broadcast_to(ci, (bb, SUBLANES, lane_chunk)).reshape(rows8, lane_chunk)
            dr, di = _cmul(prt, pit, crb, cib)
            xr3 = (xr + dr).reshape(bb, SUBLANES, lane_chunk)
            xi3 = (xi + di).reshape(bb, SUBLANES, lane_chunk)
            hr_sc[:, pl.ds(r0, SUBLANES), ls] = xr3
            hi_sc[:, pl.ds(r0, SUBLANES), ls] = xi3
            return xr3[:, SUBLANES - 1:SUBLANES, :], xi3[:, SUBLANES - 1:SUBLANES, :]

        cr, ci = lax.fori_loop(0, tc // SUBLANES, body, (cr_sc[:, :, ls], ci_sc[:, :, ls]))
        cr_sc[:, :, ls] = cr
        ci_sc[:, :, ls] = ci

    n_pack = cre_ref.shape[0]
    k_pack = n_state // n_pack
    parts = []
    for p4 in range(n_pack):
        ks = slice(p4 * k_pack, (p4 + 1) * k_pack)
        hrb = hr_sc[:, :, ks].reshape(m, k_pack).astype(BF16)
        hib = hi_sc[:, :, ks].reshape(m, k_pack).astype(BF16)
        parts.append(jnp.dot(hrb, cre_ref[p4], preferred_element_type=F32)
                     - jnp.dot(hib, cim_ref[p4], preferred_element_type=F32))
    y = jnp.concatenate(parts, axis=1) + d_ref[...] * u
    g = y * (0.5 * (1.0 + jnp.tanh(np.sqrt(2.0 / np.pi).astype(np.float32) * (y + 0.044715 * (y * y * y)))))
    z = jnp.dot(g.astype(BF16), wglu_ref[...], preferred_element_type=F32) + bglu_ref[...]
    out = g * jax.nn.sigmoid(z)
    ys_ref[...] = out.reshape(bb, tc, d_ssm).astype(ys_ref.dtype)

    @pl.when(ti == nt - 1)
    def _():
        sr_ref[...] = cr_sc[...]
        si_ref[...] = ci_sc[...]


def _s5(proj3, h0r, h0i, lw, bb, tc, col_block):
    bsz, t, _ = proj3.shape
    n_state = h0r.shape[-1]
    d_ssm = lw["s5_d"].shape[-1]
    n_pack = lw["s5_cre"].shape[0]
    full = lambda *shape: pl.BlockSpec(shape, lambda bi, ti: (0,) * len(shape))
    lane_chunk = max(LANES, 4 * LANES // bb)
    kern = functools.partial(_s5_kernel, bb=bb, tc=tc, lane_chunk=lane_chunk)
    state_spec = pl.BlockSpec((bb, 1, n_state), lambda bi, ti: (bi, 0, 0))
    return pl.pallas_call(
        kern,
        grid=(bsz // bb, t // tc),
        in_specs=[pl.BlockSpec((bb, tc, d_ssm), lambda bi, ti: (bi, ti, col_block)),
                  state_spec, state_spec,
                  full(1, n_state), full(1, n_state), full(1, n_state),
                  full(2, d_ssm // 2, n_state // 2), full(2, d_ssm // 2, n_state // 2),
                  full(n_pack, n_state // n_pack, d_ssm // n_pack),
                  full(n_pack, n_state // n_pack, d_ssm // n_pack),
                  full(1, d_ssm), full(d_ssm, d_ssm), full(1, d_ssm)],
        out_specs=[pl.BlockSpec((bb, tc, d_ssm), lambda bi, ti: (bi, ti, 0)), state_spec, state_spec],
        out_shape=[jax.ShapeDtypeStruct((bsz, t, d_ssm), BF16),
                   jax.ShapeDtypeStruct((bsz, 1, n_state), F32),
                   jax.ShapeDtypeStruct((bsz, 1, n_state), F32)],
        scratch_shapes=[pltpu.VMEM((2, d_ssm // 2, n_state), BF16),
                        pltpu.VMEM((SUBLANES, n_state), F32), pltpu.VMEM((SUBLANES, n_state), F32),
                        pltpu.VMEM((bb, tc, n_state), F32), pltpu.VMEM((bb, tc, n_state), F32),
                        pltpu.VMEM((bb, 1, n_state), F32), pltpu.VMEM((bb, 1, n_state), F32)],
        compiler_params=_cparams(2),
        name="s5",
    )(proj3, h0r, h0i, lw["s5_lam_re"], lw["s5_lam_im"], lw["s5_log_dt"], lw["s5_wbr"], lw["s5_wbi"],
      lw["s5_cre"], lw["s5_cim"], lw["s5_d"], lw["s5_w_glu"], lw["s5_b_glu"])


def _rope_kernel(q_ref, k_ref, v_ref, cos_ref, sin_ref, *rest, gate, blocks_per_seq):
    if gate:
        qs_ref, kr_ref, vr_ref, kb_ref, vb_ref, sel_ref, km_sc = rest
    else:
        qs_ref, kr_ref, vr_ref = rest
    tm, d_attn = q_ref.shape
    reps = d_attn // LANES
    cos = jnp.concatenate([cos_ref[...]] * reps, axis=1)
    sin = jnp.concatenate([sin_ref[...]] * reps, axis=1)
    lane = lax.broadcasted_iota(jnp.int32, (tm, d_attn), 1)
    first_half = (lane & (HEAD_DIM - 1)) < HEAD_DIM // 2

    def rot(x):
        partner = jnp.where(first_half, pltpu.roll(x, d_attn - HEAD_DIM // 2, 1),
                            pltpu.roll(x, HEAD_DIM // 2, 1))
        return x * cos + partner * sin

    qr = rot(q_ref[...])
    kr = rot(k_ref[...])
    v = v_ref[...]
    if not gate:
        qs_ref[...] = qr
        kr_ref[...] = kr
        vr_ref[...] = v
        return
    krt = kr.T
    vt = v.T
    kr_ref[...] = krt
    vr_ref[...] = vt
    qs_ref[...] = (qr * (HEAD_DIM ** -0.5)).T.astype(BF16)
    kb_ref[...] = kr.astype(BF16)
    vb_ref[...] = vt.astype(BF16)

    tb = pl.program_id(0) % blocks_per_seq

    @pl.when(tb == 0)
    def _():
        km_sc[...] = jnp.zeros_like(km_sc)

    gates = lax.dot_general(km_sc[...], qr, _NT, precision=lax.Precision.HIGHEST,
                            preferred_element_type=F32)
    slot = lax.broadcasted_iota(jnp.int32, (GATE_SLOTS, tm), 0)
    slotf = slot.astype(F32)
    for h in range(ATTN_HEADS):
        hs = slice(h * GATE_SLOTS, (h + 1) * GATE_SLOTS)
        gh = jnp.where(slot < tb, gates[hs, :], -jnp.inf)
        sel = jnp.zeros((GATE_SLOTS, tm), F32)
        for _ in range(MOBA_TOPK):
            mx = jnp.max(gh, axis=0, keepdims=True)
            cand = (gh == mx) & (mx > -jnp.inf)
            first = jnp.min(jnp.where(cand, slotf, float(GATE_SLOTS)), axis=0, keepdims=True)
            pick = slotf == first
            sel = jnp.where(pick, 1.0, sel)
            gh = jnp.where(pick, -jnp.inf, gh)
        sel_ref[hs, :] = sel

    km = jnp.sum(kr, axis=0, keepdims=True) * (1.0 / MOBA_BLOCK)
    lane_row = lax.broadcasted_iota(jnp.int32, (1, d_attn), 1)
    for h in range(ATTN_HEADS):
        km_sc[pl.ds(h * GATE_SLOTS + tb, 1), :] = jnp.where(_div_pow2(lane_row, HEAD_DIM) == h, km, 0.0)


def _rope(proj2, cos_t, sin_t, tm, gate, blocks_per_seq, col0):
    n = proj2.shape[0]
    d_attn = ATTN_HEADS * HEAD_DIM
    n_tab = cos_t.shape[0] // tm
    qkv_spec = lambda c: pl.BlockSpec((tm, d_attn), lambda i, c=c: (i, c))
    tab_spec = pl.BlockSpec((tm, LANES), lambda i: (i % n_tab, 0))
    row_spec = pl.BlockSpec((tm, d_attn), lambda i: (i, 0))
    kern = functools.partial(_rope_kernel, gate=gate, blocks_per_seq=blocks_per_seq)
    if gate:
        n_col = ATTN_HEADS * GATE_SLOTS
        bsz = n // (tm * blocks_per_seq)
        t = tm * blocks_per_seq
        nb = blocks_per_seq
        col_spec = lambda rows: pl.BlockSpec((None, rows, tm), lambda i: (i // nb, 0, i % nb))
        out_specs = [col_spec(d_attn), col_spec(d_attn), col_spec(d_attn), row_spec, col_spec(d_attn),
                     col_spec(n_col)]
        out_shape = [jax.ShapeDtypeStruct((bsz, d_attn, t), BF16), jax.ShapeDtypeStruct((bsz, d_attn, t), F32),
                     jax.ShapeDtypeStruct((bsz, d_attn, t), F32), jax.ShapeDtypeStruct((n, d_attn), BF16),
                     jax.ShapeDtypeStruct((bsz, d_attn, t), BF16), jax.ShapeDtypeStruct((bsz, n_col, t), F32)]
        scratch = [pltpu.VMEM((n_col, d_attn), F32)]
    else:
        out_specs = [row_spec] * 3
        out_shape = [jax.ShapeDtypeStruct((n, d_attn), F32)] * 3
        scratch = []
    return pl.pallas_call(
        kern,
        grid=(n // tm,),
        in_specs=[qkv_spec(col0), qkv_spec(col0 + 1), qkv_spec(col0 + 2), tab_spec, tab_spec],
        out_specs=out_specs,
        out_shape=out_shape,
        scratch_shapes=scratch,
        compiler_params=_cparams(1),
        name="rope_gate" if gate else "rope",
    )(proj2, proj2, proj2, cos_t, sin_t)


def _attn_kernel(qt_ref, k_ref, vt_ref, sel_ref, o_ref, acc_sc):
    qi = pl.program_id(2)
    width, blk = qt_ref.shape
    nh = width // HEAD_DIM
    qt = qt_ref[...]
    row_head = _div_pow2(lax.broadcasted_iota(jnp.int32, qt.shape, 0), HEAD_DIM)
    zero = jnp.zeros_like(qt)
    q_heads = jnp.concatenate([jnp.where(row_head == e, qt, zero) for e in range(nh)], axis=1)
    keyi = lax.broadcasted_iota(jnp.int32, (blk, blk), 0)
    qcol = lax.broadcasted_iota(jnp.int32, (blk, blk), 1)
    causal = keyi <= qcol

    r0 = pl.multiple_of(qi * blk, blk)
    ko = k_ref[pl.ds(r0, blk), :]
    vo = vt_ref[:, pl.ds(r0, blk)]
    s_own = jnp.dot(ko, q_heads, preferred_element_type=F32)
    stats = []
    for e in range(nh):
        s = jnp.where(causal, s_own[:, e * blk:(e + 1) * blk], NEG_INF)
        mx = jnp.max(s, axis=0, keepdims=True)
        p = jnp.exp(s - mx)
        stats += [mx, jnp.sum(p, axis=0, keepdims=True)]
        acc_sc[e] = jnp.dot(vo[e * HEAD_DIM:(e + 1) * HEAD_DIM, :], p.astype(BF16), preferred_element_type=F32)

    def body(j, carry):
        rn = pl.multiple_of(j * (2 * blk), 2 * blk)
        kn = k_ref[pl.ds(rn, 2 * blk), :]
        vn = vt_ref[:, pl.ds(rn, 2 * blk)]
        s_all = jnp.dot(kn, q_heads, preferred_element_type=F32)
        out = []
        for e in range(nh):
            mx, l = carry[2 * e], carry[2 * e + 1]
            s = s_all[:, e * blk:(e + 1) * blk]
            keep_a = sel_ref[pl.ds(e * GATE_SLOTS + 2 * j, 1), :] > 0.5
            keep_b = sel_ref[pl.ds(e * GATE_SLOTS + 2 * j + 1, 1), :] > 0.5
            sa = jnp.where(keep_a, s[:blk], NEG_INF)
            sb = jnp.where(keep_b, s[blk:], NEG_INF)
            mn = jnp.maximum(mx, jnp.maximum(jnp.max(sa, axis=0, keepdims=True),
                                             jnp.max(sb, axis=0, keepdims=True)))
            alpha = jnp.exp(mx - mn)
            pa = jnp.exp(sa - mn)
            pb = jnp.exp(sb - mn)
            l = alpha * l + jnp.sum(pa, axis=0, keepdims=True) + jnp.sum(pb, axis=0, keepdims=True)
            p = jnp.concatenate([pa, pb], axis=0).astype(BF16)
            acc_sc[e] = alpha * acc_sc[e] + jnp.dot(vn[e * HEAD_DIM:(e + 1) * HEAD_DIM, :], p,
                                                    preferred_element_type=F32)
            out += [mn, l]
        return tuple(out)

    res = lax.fori_loop(0, (qi + 1) // 2, body, tuple(stats))
    ot = jnp.concatenate([acc_sc[e] / res[2 * e + 1] for e in range(nh)], axis=0)
    o_ref[...] = ot.T.astype(o_ref.dtype)


def _attn_prompt(qt, kb, vt, selt, bsz, t):
    d_attn = ATTN_HEADS * HEAD_DIM
    nq = t // MOBA_BLOCK
    kb3 = kb.reshape(bsz, t, d_attn)
    width = MXU_DIM
    nh = width // HEAD_DIM
    return pl.pallas_call(
        _attn_kernel,
        grid=(bsz, d_attn // width, nq),
        in_specs=[pl.BlockSpec((None, width, MOBA_BLOCK), lambda b, hg, qi: (b, hg, qi)),
                  pl.BlockSpec((None, t, width), lambda b, hg, qi: (b, 0, hg)),
                  pl.BlockSpec((None, width, t), lambda b, hg, qi: (b, hg, 0)),
                  pl.BlockSpec((None, nh * GATE_SLOTS, MOBA_BLOCK), lambda b, hg, qi: (b, hg, qi))],
        out_specs=pl.BlockSpec((MOBA_BLOCK, width), lambda b, hg, qi: (b * nq + qi, hg)),
        out_shape=jax.ShapeDtypeStruct((bsz * t, d_attn), BF16),
        scratch_shapes=[pltpu.VMEM((nh, HEAD_DIM, MOBA_BLOCK), F32)],
        compiler_params=_cparams(3),
        name="attn_prompt",
    )(qt, kb3, vt, selt)


def _sattn_kernel(pt_ref, q_ref, kn_ref, vn_ref, *rest, nblk, blocks_per_step):
    n_pages = 2 * blocks_per_step
    k_refs = rest[:n_pages]
    v_refs = rest[n_pages:2 * n_pages]
    o_ref, g_sc, m_sc, l_sc, o_sc = rest[2 * n_pages:]
    step = pl.program_id(1)
    tq, d_attn = q_ref.shape
    page = k_refs[0].shape[-1]
    rows = ATTN_HEADS * tq
    scale = HEAD_DIM ** -0.5

    def heads_to_rows(x):
        return jnp.concatenate([x[:, h * HEAD_DIM:(h + 1) * HEAD_DIM] for h in range(ATTN_HEADS)], axis=0)

    qr = q_ref[...]
    qs = (heads_to_rows(qr) * scale).astype(BF16)
    lane = lax.broadcasted_iota(jnp.int32, (rows, LANES), 1)
    row_head = _div_pow2(lax.broadcasted_iota(jnp.int32, (rows, d_attn), 0), tq)
    lane_head = _div_pow2(lax.broadcasted_iota(jnp.int32, (rows, d_attn), 1), HEAD_DIM)
    qbd = jnp.where(row_head == lane_head, jnp.concatenate([qr] * ATTN_HEADS, axis=0), 0.0)
    qbs = (qbd * scale).astype(BF16)

    @pl.when(step == 0)
    def _():
        g_sc[...] = jnp.full_like(g_sc, -jnp.inf)
        m_sc[...] = jnp.zeros_like(m_sc)
        l_sc[...] = jnp.zeros_like(l_sc)

    g_acc, m_acc, l_acc = g_sc[...], m_sc[...], l_sc[...]
    for c in range(blocks_per_step):
        n = step * blocks_per_step + c
        k0 = k_refs[2 * c][...].reshape(d_attn, page)
        k1 = k_refs[2 * c + 1][...].reshape(d_attn, page)
        v0 = v_refs[2 * c][...].reshape(d_attn, page).astype(BF16)
        v1 = v_refs[2 * c + 1][...].reshape(d_attn, page).astype(BF16)
        kmean = jnp.sum(k0 + k1, axis=1, keepdims=True) * (1.0 / MOBA_BLOCK)
        gate = jnp.dot(qbd, jnp.broadcast_to(kmean, (d_attn, LANES)), precision=lax.Precision.HIGHEST,
                       preferred_element_type=F32)
        s = jnp.concatenate([jnp.dot(qbs, k0.astype(BF16), preferred_element_type=F32),
                             jnp.dot(qbs, k1.astype(BF16), preferred_element_type=F32)], axis=1)
        mx = jnp.max(s, axis=1, keepdims=True)
        p = jnp.exp(s - mx)
        l = jnp.sum(p, axis=1, keepdims=True)
        pb = p.astype(BF16)
        o_full = (lax.dot_general(pb[:, :page], v0, _NT, preferred_element_type=F32)
                  + lax.dot_general(pb[:, page:], v1, _NT, preferred_element_type=F32))
        o_sc[n] = jnp.concatenate([o_full[h * tq:(h + 1) * tq, h * HEAD_DIM:(h + 1) * HEAD_DIM]
                                   for h in range(ATTN_HEADS)], axis=0)
        hit = lane == n
        g_acc = jnp.where(hit, gate, g_acc)
        m_acc = jnp.where(hit, mx, m_acc)
        l_acc = jnp.where(hit, l, l_acc)
    g_sc[...] = g_acc
    m_sc[...] = m_acc
    l_sc[...] = l_acc

    @pl.when(step == pl.num_programs(1) - 1)
    def _():
        gh = jnp.where(lane < nblk, g_sc[...], -jnp.inf)
        lanef = lane.astype(F32)
        chosen = jnp.zeros((rows, LANES), F32)
        for _ in range(MOBA_TOPK):
            gmx = jnp.max(gh, axis=1, keepdims=True)
            cand = (gh == gmx) & (gmx > -jnp.inf)
            first = jnp.min(jnp.where(cand, lanef, float(LANES)), axis=1, keepdims=True)
            pick = lanef == first
            chosen = jnp.where(pick, 1.0, chosen)
            gh = jnp.where(pick, -jnp.inf, gh)
        selw = chosen > 0.5
        m_all = m_sc[...]
        l_all = l_sc[...]
        knew = heads_to_rows(kn_ref[...]).astype(BF16)
        vnew = heads_to_rows(vn_ref[...]).astype(BF16)
        s_own = lax.dot_general(qs, knew, _NT, preferred_element_type=F32)
        ri = lax.broadcasted_iota(jnp.int32, (rows, rows), 0)
        ci = lax.broadcasted_iota(jnp.int32, (rows, rows), 1)
        ok = (_div_pow2(ri, tq) == _div_pow2(ci, tq)) & ((ci & (tq - 1)) <= (ri & (tq - 1)))
        s_own = jnp.where(ok, s_own, NEG_INF)
        m_sel = jnp.max(jnp.where(selw, m_all, -jnp.inf), axis=1, keepdims=True)
        mf = jnp.maximum(jnp.max(s_own, axis=1, keepdims=True), m_sel)
        w = jnp.where(selw, jnp.exp(m_all - mf), 0.0)
        p_own = jnp.exp(s_own - mf)
        lf = jnp.sum(w * l_all, axis=1, keepdims=True) + jnp.sum(p_own, axis=1, keepdims=True)
        of = jnp.dot(p_own.astype(BF16), vnew, preferred_element_type=F32)
        for nn in range(nblk):
            of = of + w[:, nn:nn + 1] * o_sc[nn]
        out = of / lf
        o_ref[...] = jnp.concatenate([out[h * tq:(h + 1) * tq, :] for h in range(ATTN_HEADS)],
                                     axis=1).astype(o_ref.dtype)


def _attn_sample(qr3, kr3, vr3, cache_kt, cache_vt, page_table, layer):
    bsz, tq, d_attn = qr3.shape
    page = cache_kt.shape[4]
    pages_per_blk = MOBA_BLOCK // page
    assert pages_per_blk == 2 and cache_kt.shape[2:4] == (ATTN_HEADS, HEAD_DIM) and page == LANES
    nblk = page_table.shape[1] // pages_per_blk
    rows = ATTN_HEADS * tq
    blocks_per_step = 2 if nblk % 2 == 0 else 1
    pages_per_step = pages_per_blk * blocks_per_step
    tok_spec = pl.BlockSpec((None, tq, d_attn), lambda b, n, pt: (b, 0, 0))

    def page_spec(j):
        return pl.BlockSpec((None, None, ATTN_HEADS, HEAD_DIM, page),
                            lambda b, n, pt, j=j: (layer, pt[b, pages_per_step * n + j], 0, 0, 0))

    page_specs = [page_spec(j) for j in range(pages_per_step)]
    grid_spec = pltpu.PrefetchScalarGridSpec(
        num_scalar_prefetch=1,
        grid=(bsz, nblk // blocks_per_step),
        in_specs=[tok_spec, tok_spec, tok_spec] + page_specs + page_specs,
        out_specs=pl.BlockSpec((None, tq, d_attn), lambda b, n, pt: (b, 0, 0)),
        scratch_shapes=[pltpu.VMEM((rows, LANES), F32), pltpu.VMEM((rows, LANES), F32),
                        pltpu.VMEM((rows, LANES), F32), pltpu.VMEM((nblk, rows, HEAD_DIM), F32)],
    )
    return pl.pallas_call(
        functools.partial(_sattn_kernel, nblk=nblk, blocks_per_step=blocks_per_step),
        grid_spec=grid_spec,
        out_shape=jax.ShapeDtypeStruct((bsz, tq, d_attn), BF16),
        compiler_params=_cparams(2),
        name="attn_sample",
    )(page_table, qr3, kr3, vr3, *([cache_kt] * pages_per_step), *([cache_vt] * pages_per_step))


def _merge_kernel(x_ref, ya_ref, ys_ref, yc_ref, g0_ref, g1_ref, g2_ref,
                  wr_ref, ws_ref, wa_ref, wo_ref, lg_ref, lb_ref, o_ref, *, alpha):
    merged = jax.nn.sigmoid(g0_ref[...]) * jnp.dot(ya_ref[...], wr_ref[...], preferred_element_type=F32)
    merged = merged + jax.nn.sigmoid(g1_ref[...]) * jnp.dot(ys_ref[...], ws_ref[...],
                                                           preferred_element_type=F32)
    merged = merged + jax.nn.sigmoid(g2_ref[...]) * jnp.dot(yc_ref[...], wa_ref[...],
                                                           preferred_element_type=F32)
    z = alpha * x_ref[...] + jnp.dot(merged.astype(BF16), wo_ref[...], preferred_element_type=F32)
    o_ref[...] = _layer_norm(z, lg_ref[...], lb_ref[...])


def _merge(x, ya, ys, yc, proj2, lw, tm, gate_col0, alpha):
    n, d = x.shape
    row = lambda w: pl.BlockSpec((tm, w), lambda i: (i, 0))
    full = lambda a: pl.BlockSpec(a.shape, lambda i: (0,) * a.ndim)
    gspec = lambda c: pl.BlockSpec((tm, d), lambda i, c=c: (i, c))
    ws = [lw["w_br_rnn"], lw["w_br_ssm"], lw["w_br_attn"], lw["w_out"], lw["ln1_g"], lw["ln1_b"]]
    return pl.pallas_call(
        functools.partial(_merge_kernel, alpha=alpha),
        grid=(n // tm,),
        in_specs=[row(d), row(ya.shape[1]), row(ys.shape[1]), row(yc.shape[1]),
                  gspec(gate_col0), gspec(gate_col0 + 1), gspec(gate_col0 + 2)] + [full(a) for a in ws],
        out_specs=row(d),
        out_shape=jax.ShapeDtypeStruct((n, d), F32),
        compiler_params=_cparams(1),
        name="merge_ln",
    )(x, ya, ys, yc, proj2, proj2, proj2, *ws)


def _ffn_kernel(x_ref, wg_ref, wu_ref, wo_ref, lg_ref, lb_ref, o_ref, xb_sc, acc_sc, *, alpha):
    j = pl.program_id(1)

    @pl.when(j == 0)
    def _():
        xb_sc[...] = x_ref[...].astype(BF16)
        acc_sc[...] = jnp.zeros_like(acc_sc)

    xb = xb_sc[...]
    hg = jnp.dot(xb, wg_ref[...], preferred_element_type=F32)
    hu = jnp.dot(xb, wu_ref[...], preferred_element_type=F32)
    h = (hg * jax.nn.sigmoid(hg)) * hu
    acc_sc[...] += jnp.dot(h.astype(BF16), wo_ref[...], preferred_element_type=F32)

    @pl.when(j == pl.num_programs(1) - 1)
    def _():
        o_ref[...] = _layer_norm(alpha * x_ref[...] + acc_sc[...], lg_ref[...], lb_ref[...])


def _ffn(x, lw, tm, tf, alpha):
    n, d = x.shape
    d_ff = lw["w_ffn_out"].shape[0]
    nj = d_ff // tf
    return pl.pallas_call(
        functools.partial(_ffn_kernel, alpha=alpha),
        grid=(n // tm, nj),
        in_specs=[pl.BlockSpec((tm, d), lambda i, j: (i, 0)),
                  pl.BlockSpec((d, tf), lambda i, j: (0, j)),
                  pl.BlockSpec((d, tf), lambda i, j: (0, nj + j)),
                  pl.BlockSpec((tf, d), lambda i, j: (j, 0)),
                  pl.BlockSpec((1, d), lambda i, j: (0, 0)),
                  pl.BlockSpec((1, d), lambda i, j: (0, 0))],
        out_specs=pl.BlockSpec((tm, d), lambda i, j: (i, 0)),
        out_shape=jax.ShapeDtypeStruct((n, d), F32),
        scratch_shapes=[pltpu.VMEM((tm, d), BF16), pltpu.VMEM((tm, d), F32)],
        compiler_params=_cparams(2),
        name="ffn_ln",
    )(x, lw["w_ffn_in"], lw["w_ffn_in"], lw["w_ffn_out"], lw["ln2_g"], lw["ln2_b"])


def _block_diag(w, per_block):
    n, r, c = w.shape
    eye = jnp.eye(per_block, dtype=w.dtype)
    out = jnp.einsum("kgrc,gh->kgrhc", w.reshape(n // per_block, per_block, r, c), eye)
    return out.reshape(n // per_block, per_block * r, per_block * c)


def _prep_layer(l, p):
    heads_per_tile = MXU_DIM // (p["rg_w_a"].shape[-1])
    row = lambda a: a[l].reshape(1, -1)
    groups = p["s5_b_re"].shape[1]
    lw = {
        "w_in": p["w_in"][l].astype(BF16),
        "conv_w": p["conv_w"][l],
        "conv_b": row(p["conv_b"]),
        "rg_wbd": jnp.concatenate([_block_diag(p["rg_w_a"][l], heads_per_tile),
                                   _block_diag(p["rg_w_x"][l], heads_per_tile)], axis=2).astype(BF16),
        "rg_b_a": row(p["rg_b_a"]), "rg_b_x": row(p["rg_b_x"]), "rg_lambda": row(p["rg_lambda"]),
        "s5_lam_re": row(p["s5_lambda_re"]), "s5_lam_im": row(p["s5_lambda_im"]),
        "s5_log_dt": jnp.repeat(p["s5_log_step"][l], SSM_STATE).reshape(1, -1),
        "s5_wbr": _block_diag(jnp.swapaxes(p["s5_b_re"][l], 1, 2), groups // 2),
        "s5_wbi": _block_diag(jnp.swapaxes(p["s5_b_im"][l], 1, 2), groups // 2),
        "s5_cre": _block_diag(jnp.swapaxes(p["s5_c_re"][l], 1, 2), LANES // SSM_GROUP).astype(BF16),
        "s5_cim": _block_diag(jnp.swapaxes(p["s5_c_im"][l], 1, 2), LANES // SSM_GROUP).astype(BF16),
        "s5_d": row(p["s5_d"]),
        "s5_w_glu": p["s5_w_glu"][l].astype(BF16),
        "s5_b_glu": row(p["s5_b_glu"]),
        "w_br_rnn": p["w_br_rnn"][l].astype(BF16),
        "w_br_ssm": p["w_br_ssm"][l].astype(BF16),
        "w_br_attn": p["w_br_attn"][l].astype(BF16),
        "w_out": p["w_out"][l].astype(BF16),
        "ln1_g": row(p["ln1_g"]), "ln1_b": row(p["ln1_b"]),
        "w_ffn_in": p["w_ffn_in"][l].astype(BF16),
        "w_ffn_out": p["w_ffn_out"][l].astype(BF16),
        "ln2_g": row(p["ln2_g"]), "ln2_b": row(p["ln2_b"]),
    }
    return lw


def _rope_tables(pos0, t):
    half = HEAD_DIM // 2
    inv = jnp.power(ROPE_THETA, -jnp.arange(half, dtype=F32) * (2.0 / HEAD_DIM))
    ang = (pos0 + jnp.arange(t)).astype(F32)[:, None] * inv
    cos, sin = jnp.cos(ang), jnp.sin(ang)
    reps = LANES // HEAD_DIM
    cos_t = jnp.tile(jnp.concatenate([cos, cos], axis=1), (1, reps))
    sin_t = jnp.tile(jnp.concatenate([-sin, sin], axis=1), (1, reps))
    return cos_t, sin_t


def _largest_tile(n, cap):
    t = min(n, cap)
    while n % t:
        t //= 2
    return t


def _trunk_layer(x3, lw, alpha, cbuf8, h0, s5r0, s5i0, *, prompt, pos0=0, cache=None):
    bsz, t, d = x3.shape
    n = bsz * t
    d_rnn = h0.shape[-1]
    d_ssm = lw["s5_d"].shape[-1]
    d_attn = ATTN_HEADS * HEAD_DIM
    x2 = x3.reshape(n, d)
    proj2 = _proj(x2, lw["w_in"], _largest_tile(n, 256), 1536)
    proj3 = proj2.reshape(bsz, t, -1)

    if prompt:
        bb, tc = 1, MOBA_BLOCK
    else:
        bb, tc = _largest_tile(bsz, 32), t
    assert tc % SUBLANES == 0 and tc & (tc - 1) == 0 and t % tc == 0
    ya, conv_new, h_new = _rglru(proj3, cbuf8, h0, lw, bb, tc)
    ys, s5r, s5i = _s5(proj3, s5r0, s5i0, lw, bb, tc, d_rnn // d_ssm)

    col0 = (d_rnn + d_ssm) // d_attn
    tm_rope = MOBA_BLOCK
    assert n % tm_rope == 0
    if prompt:
        assert t % MOBA_BLOCK == 0 and t // MOBA_BLOCK <= GATE_SLOTS
        cos_t, sin_t = _rope_tables(0, t)
        qt, krt, vrt, kb, vt, selt = _rope(proj2, cos_t, sin_t, tm_rope, True, t // MOBA_BLOCK, col0)
        yc = _attn_prompt(qt, kb, vt, selt, bsz, t)
        new_k = krt.reshape(bsz, ATTN_HEADS, HEAD_DIM, t).transpose(0, 3, 1, 2)
        new_v = vrt.reshape(bsz, ATTN_HEADS, HEAD_DIM, t).transpose(0, 3, 1, 2)
    else:
        assert tm_rope % t == 0
        cos_t, sin_t = _rope_tables(pos0, t)
        cos_t = jnp.tile(cos_t, (tm_rope // t, 1))
        sin_t = jnp.tile(sin_t, (tm_rope // t, 1))
        qr, kr, vr = _rope(proj2, cos_t, sin_t, tm_rope, False, 1, col0)
        cache_kt, cache_vt, page_table, layer = cache
        yc = _attn_sample(qr.reshape(bsz, t, d_attn), kr.reshape(bsz, t, d_attn), vr.reshape(bsz, t, d_attn),
                          cache_kt, cache_vt, page_table, layer).reshape(n, d_attn)
        new_k = kr.reshape(bsz, t, ATTN_HEADS, HEAD_DIM)
        new_v = vr.reshape(bsz, t, ATTN_HEADS, HEAD_DIM)

    gate_col0 = (d_rnn + d_ssm + 3 * d_attn) // d
    x1 = _merge(x2, ya.reshape(n, d_rnn), ys.reshape(n, d_ssm), yc, proj2, lw, _largest_tile(n, 256),
                gate_col0, alpha)
    x_out = _ffn(x1, lw, _largest_tile(n, 1024), MXU_DIM, alpha)
    return (x_out.reshape(bsz, t, d), new_k, new_v, conv_new, h_new.reshape(bsz, d_rnn), s5r, s5i)


def kernel(x_prompt, x_sample, cache_k, cache_v, state_conv, state_rglru, state_s5_re, state_s5_im, page_table,
           w_in, conv_w, conv_b, rg_w_a, rg_b_a, rg_w_x, rg_b_x, rg_lambda,
           s5_lambda_re, s5_lambda_im, s5_b_re, s5_b_im, s5_c_re, s5_c_im, s5_d, s5_log_step, s5_w_glu, s5_b_glu,
           w_br_rnn, w_br_ssm, w_br_attn, w_out, ln1_g, ln1_b, w_ffn_in, w_ffn_out, ln2_g, ln2_b):
    params = dict(w_in=w_in, conv_w=conv_w, conv_b=conv_b, rg_w_a=rg_w_a, rg_b_a=rg_b_a, rg_w_x=rg_w_x,
                  rg_b_x=rg_b_x, rg_lambda=rg_lambda, s5_lambda_re=s5_lambda_re, s5_lambda_im=s5_lambda_im,
                  s5_b_re=s5_b_re, s5_b_im=s5_b_im, s5_c_re=s5_c_re, s5_c_im=s5_c_im, s5_d=s5_d,
                  s5_log_step=s5_log_step, s5_w_glu=s5_w_glu, s5_b_glu=s5_b_glu, w_br_rnn=w_br_rnn,
                  w_br_ssm=w_br_ssm, w_br_attn=w_br_attn, w_out=w_out, ln1_g=ln1_g, ln1_b=ln1_b,
                  w_ffn_in=w_ffn_in, w_ffn_out=w_ffn_out, ln2_g=ln2_g, ln2_b=ln2_b)
    depth = w_in.shape[0]
    alpha = (2.0 * depth) ** 0.25
    bp = x_prompt.shape[0]
    bs = x_sample.shape[0]
    d_rnn = state_rglru.shape[-1]
    groups, n_p = state_s5_re.shape[-2:]
    n_state = groups * n_p
    n_pages = page_table.shape[1]
    page = cache_k.shape[2]
    past_len = n_pages * page
    d_attn = ATTN_HEADS * HEAD_DIM
    assert past_len % MOBA_BLOCK == 0 and x_sample.shape[1] <= MOBA_BLOCK
    cache_kt = cache_k.transpose(0, 1, 3, 4, 2)
    cache_vt = cache_v.transpose(0, 1, 3, 4, 2)

    zeros_p = lambda *s: jnp.zeros((bp,) + s, F32)
    yp, ys = x_prompt, x_sample
    outs_p, outs_s = [], []
    for l in range(depth):
        lw = _prep_layer(l, params)
        res = _trunk_layer(yp, lw, alpha, zeros_p(SUBLANES, d_rnn), zeros_p(1, d_rnn),
                           zeros_p(1, n_state), zeros_p(1, n_state), prompt=True)
        yp = res[0]
        outs_p.append(res[1:])
        cbuf8 = jnp.pad(state_conv[l], ((0, 0), (SUBLANES - (CONV_WIDTH - 1), 0), (0, 0)))
        res = _trunk_layer(ys, lw, alpha, cbuf8, state_rglru[l].reshape(bs, 1, d_rnn),
                           state_s5_re[l].reshape(bs, 1, n_state), state_s5_im[l].reshape(bs, 1, n_state),
                           prompt=False, pos0=past_len, cache=(cache_kt, cache_vt, page_table, l))
        ys = res[0]
        outs_s.append(res[1:])

    def stack(outs, i, shape=None):
        arrs = [o[i] if shape is None else o[i].reshape(shape) for o in outs]
        return jnp.stack(arrs)

    return (yp, ys,
            stack(outs_p, 0), stack(outs_p, 1), stack(outs_p, 2), stack(outs_p, 3),
            stack(outs_p, 4, (bp, groups, n_p)), stack(outs_p, 5, (bp, groups, n_p)),
            stack(outs_s, 0), stack(outs_s, 1), stack(outs_s, 2), stack(outs_s, 3),
            stack(outs_s, 4, (bs, groups, n_p)), stack(outs_s, 5, (bs, groups, n_p)))
```

```python
import functools

import jax
import jax.numpy as jnp
import numpy as np
from jax import lax
from jax.experimental import pallas as pl
from jax.experimental.pallas import tpu as pltpu

F32 = jnp.float32
BF16 = jnp.bfloat16

ATTN_HEADS = 8
HEAD_DIM = 64
RNN_HEADS = 16
CONV_WIDTH = 4
LRU_C = 8.0
SSM_GROUP = 16
SSM_STATE = 64
MOBA_BLOCK = 256
MOBA_TOPK = 3
ROPE_THETA = 10000.0
LN_EPS = 1e-5
NEG_INF = -1e30
LOG2_E = 1.4426950408889634

LANES = 128
SUBLANES = 8
MXU_DIM = 256
VMEM_LIMIT_BYTES = 56 * 1024 * 1024

GATE_SLOTS = 32
SCAN_SHIFTS = (1, 2, 4)
_NT = (((1,), (1,)), ((), ()))


def _cparams(n_axes):
    return pltpu.CompilerParams(dimension_semantics=("arbitrary",) * n_axes,
                                vmem_limit_bytes=VMEM_LIMIT_BYTES)


def _div_pow2(x, d):
    assert d & (d - 1) == 0
    return lax.shift_right_logical(x, d.bit_length() - 1)


def _layer_norm(z, g, b):
    mu = jnp.mean(z, axis=-1, keepdims=True)
    zc = z - mu
    var = jnp.mean(zc * zc, axis=-1, keepdims=True)
    return zc * lax.rsqrt(var + LN_EPS) * g + b


def _proj_kernel(x_ref, w_ref, o_ref, *, tn):
    xb = x_ref[...].astype(BF16)
    for j in range(o_ref.shape[1] // tn):
        cs = slice(j * tn, (j + 1) * tn)
        o_ref[:, cs] = jnp.dot(xb, w_ref[:, cs], preferred_element_type=F32)


def _proj(x, w, tm, tn):
    n, d = x.shape
    nout = w.shape[1]
    return pl.pallas_call(
        functools.partial(_proj_kernel, tn=tn),
        grid=(n // tm,),
        in_specs=[pl.BlockSpec((tm, d), lambda i: (i, 0)),
                  pl.BlockSpec((d, nout), lambda i: (0, 0))],
        out_specs=pl.BlockSpec((tm, nout), lambda i: (i, 0)),
        out_shape=jax.ShapeDtypeStruct((n, nout), F32),
        compiler_params=_cparams(1),
        name="proj",
    )(x, w)


def _rglru_kernel(x_ref, cbuf_ref, h0_ref, cw_ref, cb_ref, wbd_ref, ba_ref, bx_ref, lam_ref,
                  ya_ref, cnew_ref, hnew_ref, xbuf, hcar, *, bb, tc):
    ti = pl.program_id(1)
    nt = pl.num_programs(1)
    c_dim = x_ref.shape[-1]
    m = bb * tc

    @pl.when(ti == 0)
    def _():
        xbuf[:, 0:SUBLANES, :] = cbuf_ref[...]
        hcar[...] = h0_ref[...]

    x = x_ref[...]
    xbuf[:, SUBLANES:SUBLANES + tc, :] = x
    cw = cw_ref[...]
    xc = cb_ref[...] + cw[0:1] * xbuf[:, 5:5 + tc, :]
    xc = xc + cw[1:2] * xbuf[:, 6:6 + tc, :]
    xc = xc + cw[2:3] * xbuf[:, 7:7 + tc, :]
    xc = xc + cw[3:4] * x
    xc2 = xc.reshape(m, c_dim)
    xcb = xc2.astype(BF16)

    nl = -lam_ref[...]
    softplus = jnp.maximum(nl, 0.0) + jnp.log1p(jnp.exp(-jnp.abs(nl)))
    c_row = -LRU_C * softplus
    assert tc == SUBLANES or bb == 1
    row = lax.broadcasted_iota(jnp.int32, (m // SUBLANES, SUBLANES, MXU_DIM), 1)
    hc = hcar[...]
    for g in range(c_dim // MXU_DIM):
        sl = slice(g * MXU_DIM, (g + 1) * MXU_DIM)
        ga = jnp.dot(xcb[:, sl], wbd_ref[g], preferred_element_type=F32)
        r = jax.nn.sigmoid(ga[:, :MXU_DIM] + ba_ref[:, sl])
        i = jax.nn.sigmoid(ga[:, MXU_DIM:] + bx_ref[:, sl])
        log_a = c_row[:, sl] * r
        a = jnp.exp(log_a)
        mult = jnp.sqrt(jnp.maximum(-jnp.tanh(log_a) * (a * a + 1.0), 0.0))
        b = mult * (i * xc2[:, sl])
        a = a.reshape(m // SUBLANES, SUBLANES, MXU_DIM)
        b = b.reshape(m // SUBLANES, SUBLANES, MXU_DIM)
        s = 1
        while s < SUBLANES:
            msk = row >= s
            a_sh = jnp.where(msk, pltpu.roll(a, s, 1), 1.0)
            b_sh = jnp.where(msk, pltpu.roll(b, s, 1), 0.0)
            b = a * b_sh + b
            a = a * a_sh
            s *= 2
        a = a.reshape(m, MXU_DIM)
        b = b.reshape(m, MXU_DIM)
        if tc == SUBLANES:
            hcg = jnp.broadcast_to(hc[:, :, sl], (bb, tc, MXU_DIM)).reshape(m, MXU_DIM)
            h = b + a * hcg
        else:
            carry = hc[0, :, sl]
            groups = []
            for j in range(tc // SUBLANES):
                rs = slice(j * SUBLANES, (j + 1) * SUBLANES)
                hj = b[rs] + a[rs] * carry
                groups.append(hj)
                carry = hj[SUBLANES - 1:SUBLANES]
            h = jnp.concatenate(groups, axis=0)
        h3 = h.reshape(bb, tc, MXU_DIM)
        ya_ref[:, :, sl] = h3.astype(ya_ref.dtype)
        hcar[:, :, sl] = h3[:, tc - 1:tc, :]

    xbuf[:, 0:SUBLANES, :] = xbuf[:, tc:tc + SUBLANES, :]

    @pl.when(ti == nt - 1)
    def _():
        cnew_ref[...] = xbuf[:, 5:8, :]
        hnew_ref[...] = hcar[...]


def _rglru(proj3, cbuf8, h0, lw, bb, tc):
    bsz, t, _ = proj3.shape
    c_dim = h0.shape[-1]
    full = lambda *shape: pl.BlockSpec(shape, lambda bi, ti: (0,) * len(shape))
    kern = functools.partial(_rglru_kernel, bb=bb, tc=tc)
    return pl.pallas_call(
        kern,
        grid=(bsz // bb, t // tc),
        in_specs=[pl.BlockSpec((bb, tc, c_dim), lambda bi, ti: (bi, ti, 0)),
                  pl.BlockSpec((bb, SUBLANES, c_dim), lambda bi, ti: (bi, 0, 0)),
                  pl.BlockSpec((bb, 1, c_dim), lambda bi, ti: (bi, 0, 0)),
                  full(CONV_WIDTH, c_dim), full(1, c_dim),
                  full(c_dim // MXU_DIM, MXU_DIM, 2 * MXU_DIM),
                  full(1, c_dim), full(1, c_dim), full(1, c_dim)],
        out_specs=[pl.BlockSpec((bb, tc, c_dim), lambda bi, ti: (bi, ti, 0)),
                   pl.BlockSpec((bb, CONV_WIDTH - 1, c_dim), lambda bi, ti: (bi, 0, 0)),
                   pl.BlockSpec((bb, 1, c_dim), lambda bi, ti: (bi, 0, 0))],
        out_shape=[jax.ShapeDtypeStruct((bsz, t, c_dim), BF16),
                   jax.ShapeDtypeStruct((bsz, CONV_WIDTH - 1, c_dim), F32),
                   jax.ShapeDtypeStruct((bsz, 1, c_dim), F32)],
        scratch_shapes=[pltpu.VMEM((bb, tc + SUBLANES, c_dim), F32),
                        pltpu.VMEM((bb, 1, c_dim), F32)],
        compiler_params=_cparams(2),
        name="rglru",
    )(proj3, cbuf8, h0, lw["conv_w"], lw["conv_b"], lw["rg_wbd"], lw["rg_b_a"], lw["rg_b_x"],
      lw["rg_lambda"])


def _cmul(ar, ai, br, bi):
    return ar * br - ai * bi, ar * bi + ai * br


def _s5_kernel(u_ref, h0r_ref, h0i_ref, lamr_ref, lami_ref, ldt_ref, wbr_ref, wbi_ref,
               cre_ref, cim_ref, d_ref, wglu_ref, bglu_ref,
               ys_ref, sr_ref, si_ref,
               wb_sc, apr_sc, api_sc, lvr_sc, lvi_sc, hr_sc, hi_sc, cr_sc, ci_sc, *, bb, tc, lane_chunk):
    bi = pl.program_id(0)
    ti = pl.program_id(1)
    nt = pl.num_programs(1)
    m = bb * tc
    n_state = hr_sc.shape[-1]
    d_ssm = u_ref.shape[-1]
    half = n_state // 2

    @pl.when((bi == 0) & (ti == 0))
    def _():
        dt = jnp.exp(ldt_ref[...])
        lr = lamr_ref[...]
        li = lami_ref[...]
        mag = jnp.exp(lr * dt)
        abr = mag * jnp.cos(li * dt)
        abi = mag * jnp.sin(li * dt)
        nr = abr - 1.0
        den = lr * lr + li * li
        zr = (nr * lr + abi * li) / den
        zi = (abi * lr - nr * li) / den
        for kb in range(2):
            ks = slice(kb * half, (kb + 1) * half)
            br = wbr_ref[kb]
            bim = wbi_ref[kb]
            wb_sc[kb, :, 0:half] = (zr[:, ks] * br - zi[:, ks] * bim).astype(BF16)
            wb_sc[kb, :, half:n_state] = (zr[:, ks] * bim + zi[:, ks] * br).astype(BF16)
        row8 = lax.broadcasted_iota(jnp.int32, (SUBLANES, n_state), 0)
        pr = jnp.broadcast_to(abr, (SUBLANES, n_state))
        pi = jnp.broadcast_to(abi, (SUBLANES, n_state))
        for s in SCAN_SHIFTS:
            msk = row8 >= s
            qr = jnp.where(msk, pltpu.roll(pr, s, 0), 1.0)
            qi = jnp.where(msk, pltpu.roll(pi, s, 0), 0.0)
            pr, pi = _cmul(pr, pi, qr, qi)
        apr_sc[...] = pr
        api_sc[...] = pi
        for idx, s in enumerate(SCAN_SHIFTS):
            inside = row8 >= s
            lvr_sc[idx] = jnp.where(inside, jnp.broadcast_to(pr[s - 1:s], (SUBLANES, n_state)), 0.0)
            lvi_sc[idx] = jnp.where(inside, jnp.broadcast_to(pi[s - 1:s], (SUBLANES, n_state)), 0.0)

    @pl.when(ti == 0)
    def _():
        cr_sc[...] = h0r_ref[...]
        ci_sc[...] = h0i_ref[...]

    u = u_ref[...].reshape(m, d_ssm)
    ub = u.astype(BF16)
    k_half = d_ssm // 2
    for kb in range(2):
        bu = jnp.dot(ub[:, kb * k_half:(kb + 1) * k_half], wb_sc[kb], preferred_element_type=F32)
        hr_sc[:, :, kb * half:(kb + 1) * half] = bu[:, :half].reshape(bb, tc, half)
        hi_sc[:, :, kb * half:(kb + 1) * half] = bu[:, half:].reshape(bb, tc, half)

    rows8 = bb * SUBLANES
    for lc in range(n_state // lane_chunk):
        ls = slice(lc * lane_chunk, (lc + 1) * lane_chunk)

        def per_seq(tab):
            return jnp.broadcast_to(tab[None], (bb, SUBLANES, lane_chunk)).reshape(rows8, lane_chunk)

        steps = tuple((s, per_seq(lvr_sc[idx, :, ls]), per_seq(lvi_sc[idx, :, ls]))
                      for idx, s in enumerate(SCAN_SHIFTS))
        prt = per_seq(apr_sc[:, ls])
        pit = per_seq(api_sc[:, ls])

        def body(j, carry, ls=ls, steps=steps, prt=prt, pit=pit):
            cr, ci = carry
            r0 = pl.multiple_of(j * SUBLANES, SUBLANES)
            xr = hr_sc[:, pl.ds(r0, SUBLANES), ls].reshape(rows8, lane_chunk)
            xi = hi_sc[:, pl.ds(r0, SUBLANES), ls].reshape(rows8, lane_chunk)
            for s, ar, ai in steps:
                dr, di = _cmul(ar, ai, pltpu.roll(xr, s, 0), pltpu.roll(xi, s, 0))
                xr = xr + dr
                xi = xi + di
            crb = jnp.broadcast_to(cr, (bb, SUBLANES, lane_chunk)).reshape(rows8, lane_chunk)
            cib = jnp.broadcast_to(ci, (bb, SUBLANES, lane_chunk)).reshape(rows8, lane_chunk)
            dr, di = _cmul(prt, pit, crb, cib)
            xr3 = (xr + dr).reshape(bb, SUBLANES, lane_chunk)
            xi3 = (xi + di).reshape(bb, SUBLANES, lane_chunk)
            hr_sc[:, pl.ds(r0, SUBLANES), ls] = xr3
            hi_sc[:, pl.ds(r0, SUBLANES), ls] = xi3
            return xr3[:, SUBLANES - 1:SUBLANES, :], xi3[:, SUBLANES - 1:SUBLANES, :]

        cr, ci = lax.fori_loop(0, tc // SUBLANES, body, (cr_sc[:, :, ls], ci_sc[:, :, ls]))
        cr_sc[:, :, ls] = cr
        ci_sc[:, :, ls] = ci

    n_pack = cre_ref.shape[0]
    k_pack = n_state // n_pack
    parts = []
    for p4 in range(n_pack):
        ks = slice(p4 * k_pack, (p4 + 1) * k_pack)
        hrb = hr_sc[:, :, ks].reshape(m, k_pack).astype(BF16)
        hib = hi_sc[:, :, ks].reshape(m, k_pack).astype(BF16)
        parts.append(jnp.dot(hrb, cre_ref[p4], preferred_element_type=F32)
                     - jnp.dot(hib, cim_ref[p4], preferred_element_type=F32))
    y = jnp.concatenate(parts, axis=1) + d_ref[...] * u
    g = y * (0.5 * (1.0 + jnp.tanh(np.sqrt(2.0 / np.pi).astype(np.float32) * (y + 0.044715 * (y * y * y)))))
    z = jnp.dot(g.astype(BF16), wglu_ref[...], preferred_element_type=F32) + bglu_ref[...]
    out = g * jax.nn.sigmoid(z)
    ys_ref[...] = out.reshape(bb, tc, d_ssm).astype(ys_ref.dtype)

    @pl.when(ti == nt - 1)
    def _():
        sr_ref[...] = cr_sc[...]
        si_ref[...] = ci_sc[...]


def _s5(proj3, h0r, h0i, lw, bb, tc, col_block):
    bsz, t, _ = proj3.shape
    n_state = h0r.shape[-1]
    d_ssm = lw["s5_d"].shape[-1]
    n_pack = lw["s5_cre"].shape[0]
    full = lambda *shape: pl.BlockSpec(shape, lambda bi, ti: (0,) * len(shape))
    lane_chunk = max(LANES, 4 * LANES // bb)
    kern = functools.partial(_s5_kernel, bb=bb, tc=tc, lane_chunk=lane_chunk)
    state_spec = pl.BlockSpec((bb, 1, n_state), lambda bi, ti: (bi, 0, 0))
    return pl.pallas_call(
        kern,
        grid=(bsz // bb, t // tc),
        in_specs=[pl.BlockSpec((bb, tc, d_ssm), lambda bi, ti: (bi, ti, col_block)),
                  state_spec, state_spec,
                  full(1, n_state), full(1, n_state), full(1, n_state),
                  full(2, d_ssm // 2, n_state // 2), full(2, d_ssm // 2, n_state // 2),
                  full(n_pack, n_state // n_pack, d_ssm // n_pack),
                  full(n_pack, n_state // n_pack, d_ssm // n_pack),
                  full(1, d_ssm), full(d_ssm, d_ssm), full(1, d_ssm)],
        out_specs=[pl.BlockSpec((bb, tc, d_ssm), lambda bi, ti: (bi, ti, 0)), state_spec, state_spec],
        out_shape=[jax.ShapeDtypeStruct((bsz, t, d_ssm), BF16),
                   jax.ShapeDtypeStruct((bsz, 1, n_state), F32),
                   jax.ShapeDtypeStruct((bsz, 1, n_state), F32)],
        scratch_shapes=[pltpu.VMEM((2, d_ssm // 2, n_state), BF16),
                        pltpu.VMEM((SUBLANES, n_state), F32), pltpu.VMEM((SUBLANES, n_state), F32),
                        pltpu.VMEM((len(SCAN_SHIFTS), SUBLANES, n_state), F32),
                        pltpu.VMEM((len(SCAN_SHIFTS), SUBLANES, n_state), F32),
                        pltpu.VMEM((bb, tc, n_state), F32), pltpu.VMEM((bb, tc, n_state), F32),
                        pltpu.VMEM((bb, 1, n_state), F32), pltpu.VMEM((bb, 1, n_state), F32)],
        compiler_params=_cparams(2),
        name="s5",
    )(proj3, h0r, h0i, lw["s5_lam_re"], lw["s5_lam_im"], lw["s5_log_dt"], lw["s5_wbr"], lw["s5_wbi"],
      lw["s5_cre"], lw["s5_cim"], lw["s5_d"], lw["s5_w_glu"], lw["s5_b_glu"])


def _rope_kernel(q_ref, k_ref, v_ref, cos_ref, sin_ref, *rest, gate, blocks_per_seq):
    if gate:
        qs_ref, kr_ref, vr_ref, kb_ref, vb_ref, sel_ref, km_sc = rest
    else:
        qs_ref, kr_ref, vr_ref = rest
    tm, d_attn = q_ref.shape
    reps = d_attn // LANES
    cos = jnp.concatenate([cos_ref[...]] * reps, axis=1)
    sin = jnp.concatenate([sin_ref[...]] * reps, axis=1)
    lane = lax.broadcasted_iota(jnp.int32, (tm, d_attn), 1)
    first_half = (lane & (HEAD_DIM - 1)) < HEAD_DIM // 2

    def rot(x):
        partner = jnp.where(first_half, pltpu.roll(x, d_attn - HEAD_DIM // 2, 1),
                            pltpu.roll(x, HEAD_DIM // 2, 1))
        return x * cos + partner * sin

    qr = rot(q_ref[...])
    kr = rot(k_ref[...])
    v = v_ref[...]
    if not gate:
        qs_ref[...] = qr
        kr_ref[...] = kr
        vr_ref[...] = v
        return
    krt = kr.T
    vt = v.T
    kr_ref[...] = krt
    vr_ref[...] = vt
    qs_ref[...] = (qr * (HEAD_DIM ** -0.5 * LOG2_E)).T.astype(BF16)
    kb_ref[...] = kr.astype(BF16)
    vb_ref[...] = vt.astype(BF16)

    tb = pl.program_id(0) % blocks_per_seq

    @pl.when(tb == 0)
    def _():
        km_sc[...] = jnp.zeros_like(km_sc)

    gates = lax.dot_general(km_sc[...], qr, _NT, precision=lax.Precision.HIGHEST,
                            preferred_element_type=F32)
    slot = lax.broadcasted_iota(jnp.int32, (GATE_SLOTS, tm), 0)
    slotf = slot.astype(F32)
    for h in range(ATTN_HEADS):
        hs = slice(h * GATE_SLOTS, (h + 1) * GATE_SLOTS)
        gh = jnp.where(slot < tb, gates[hs, :], -jnp.inf)
        sel = jnp.zeros((GATE_SLOTS, tm), F32)
        for _ in range(MOBA_TOPK):
            mx = jnp.max(gh, axis=0, keepdims=True)
            cand = (gh == mx) & (mx > -jnp.inf)
            first = jnp.min(jnp.where(cand, slotf, float(GATE_SLOTS)), axis=0, keepdims=True)
            pick = slotf == first
            sel = jnp.where(pick, 1.0, sel)
            gh = jnp.where(pick, -jnp.inf, gh)
        sel_ref[hs, :] = sel

    km = jnp.sum(kr, axis=0, keepdims=True) * (1.0 / MOBA_BLOCK)
    lane_row = lax.broadcasted_iota(jnp.int32, (1, d_attn), 1)
    for h in range(ATTN_HEADS):
        km_sc[pl.ds(h * GATE_SLOTS + tb, 1), :] = jnp.where(_div_pow2(lane_row, HEAD_DIM) == h, km, 0.0)


def _rope(proj2, cos_t, sin_t, tm, gate, blocks_per_seq, col0):
    n = proj2.shape[0]
    d_attn = ATTN_HEADS * HEAD_DIM
    n_tab = cos_t.shape[0] // tm
    qkv_spec = lambda c: pl.BlockSpec((tm, d_attn), lambda i, c=c: (i, c))
    tab_spec = pl.BlockSpec((tm, LANES), lambda i: (i % n_tab, 0))
    row_spec = pl.BlockSpec((tm, d_attn), lambda i: (i, 0))
    kern = functools.partial(_rope_kernel, gate=gate, blocks_per_seq=blocks_per_seq)
    if gate:
        n_col = ATTN_HEADS * GATE_SLOTS
        bsz = n // (tm * blocks_per_seq)
        t = tm * blocks_per_seq
        nb = blocks_per_seq
        col_spec = lambda rows: pl.BlockSpec((None, rows, tm), lambda i: (i // nb, 0, i % nb))
        out_specs = [col_spec(d_attn), col_spec(d_attn), col_spec(d_attn), row_spec, col_spec(d_attn),
                     col_spec(n_col)]
        out_shape = [jax.ShapeDtypeStruct((bsz, d_attn, t), BF16), jax.ShapeDtypeStruct((bsz, d_attn, t), F32),
                     jax.ShapeDtypeStruct((bsz, d_attn, t), F32), jax.ShapeDtypeStruct((n, d_attn), BF16),
                     jax.ShapeDtypeStruct((bsz, d_attn, t), BF16), jax.ShapeDtypeStruct((bsz, n_col, t), F32)]
        scratch = [pltpu.VMEM((n_col, d_attn), F32)]
    else:
        out_specs = [row_spec] * 3
        out_shape = [jax.ShapeDtypeStruct((n, d_attn), F32)] * 3
        scratch = []
    return pl.pallas_call(
        kern,
        grid=(n // tm,),
        in_specs=[qkv_spec(col0), qkv_spec(col0 + 1), qkv_spec(col0 + 2), tab_spec, tab_spec],
        out_specs=out_specs,
        out_shape=out_shape,
        scratch_shapes=scratch,
        compiler_params=_cparams(1),
        name="rope_gate" if gate else "rope",
    )(proj2, proj2, proj2, cos_t, sin_t)


def _attn_kernel(qt_ref, k_ref, vt_ref, sel_ref, o_ref, acc_sc):
    qi = pl.program_id(2)
    width, blk = qt_ref.shape
    nh = width // HEAD_DIM
    qt = qt_ref[...]
    row_head = _div_pow2(lax.broadcasted_iota(jnp.int32, qt.shape, 0), HEAD_DIM)
    zero = jnp.zeros_like(qt)
    q_heads = jnp.concatenate([jnp.where(row_head == e, qt, zero) for e in range(nh)], axis=1)
    keyi = lax.broadcasted_iota(jnp.int32, (blk, blk), 0)
    qcol = lax.broadcasted_iota(jnp.int32, (blk, blk), 1)
    causal = keyi <= qcol

    r0 = pl.multiple_of(qi * blk, blk)
    ko = k_ref[pl.ds(r0, blk), :]
    vo = vt_ref[:, pl.ds(r0, blk)]
    s_own = jnp.dot(ko, q_heads, preferred_element_type=F32)
    ones_own = jnp.ones((SUBLANES, blk), BF16)
    stats = []
    for e in range(nh):
        s = jnp.where(causal, s_own[:, e * blk:(e + 1) * blk], NEG_INF)
        mx = jnp.max(s, axis=0, keepdims=True)
        p = jnp.exp2(s - mx).astype(BF16)
        stats += [mx, jnp.dot(ones_own, p, preferred_element_type=F32)[0:1]]
        acc_sc[e] = jnp.dot(vo[e * HEAD_DIM:(e + 1) * HEAD_DIM, :], p, preferred_element_type=F32)

    ones_pair = jnp.ones((SUBLANES, 2 * blk), BF16)

    def body(j, carry):
        rn = pl.multiple_of(j * (2 * blk), 2 * blk)
        kn = k_ref[pl.ds(rn, 2 * blk), :]
        vn = vt_ref[:, pl.ds(rn, 2 * blk)]
        s_all = jnp.dot(kn, q_heads, preferred_element_type=F32)
        out = []
        for e in range(nh):
            mx, l = carry[2 * e], carry[2 * e + 1]
            mn = mx
            halves = []
            for c in range(2):
                keep = sel_ref[pl.ds(e * GATE_SLOTS + 2 * j + c, 1), :] > 0.5
                sc = s_all[c * blk:(c + 1) * blk, e * blk:(e + 1) * blk]
                mn = jnp.maximum(mn, jnp.where(keep, jnp.max(sc, axis=0, keepdims=True), NEG_INF))
                halves.append((sc, keep))
            alpha = jnp.exp2(mx - mn)
            p = jnp.concatenate([jnp.exp2(sc - jnp.where(keep, mn, -NEG_INF)).astype(BF16)
                                 for sc, keep in halves], axis=0)
            l = alpha * l + jnp.dot(ones_pair, p, preferred_element_type=F32)[0:1]
            acc_sc[e] = alpha * acc_sc[e] + jnp.dot(vn[e * HEAD_DIM:(e + 1) * HEAD_DIM, :], p,
                                                    preferred_element_type=F32)
            out += [mn, l]
        return tuple(out)

    res = lax.fori_loop(0, (qi + 1) // 2, body, tuple(stats))
    ot = jnp.concatenate([acc_sc[e] / res[2 * e + 1] for e in range(nh)], axis=0)
    o_ref[...] = ot.T.astype(o_ref.dtype)


def _attn_prompt(qt, kb, vt, selt, bsz, t):
    d_attn = ATTN_HEADS * HEAD_DIM
    nq = t // MOBA_BLOCK
    kb3 = kb.reshape(bsz, t, d_attn)
    width = MXU_DIM
    nh = width // HEAD_DIM
    return pl.pallas_call(
        _attn_kernel,
        grid=(bsz, d_attn // width, nq),
        in_specs=[pl.BlockSpec((None, width, MOBA_BLOCK), lambda b, hg, qi: (b, hg, qi)),
                  pl.BlockSpec((None, t, width), lambda b, hg, qi: (b, 0, hg)),
                  pl.BlockSpec((None, width, t), lambda b, hg, qi: (b, hg, 0)),
                  pl.BlockSpec((None, nh * GATE_SLOTS, MOBA_BLOCK), lambda b, hg, qi: (b, hg, qi))],
        out_specs=pl.BlockSpec((MOBA_BLOCK, width), lambda b, hg, qi: (b * nq + qi, hg)),
        out_shape=jax.ShapeDtypeStruct((bsz * t, d_attn), BF16),
        scratch_shapes=[pltpu.VMEM((nh, HEAD_DIM, MOBA_BLOCK), F32)],
        compiler_params=_cparams(3),
        name="attn_prompt",
    )(qt, kb3, vt, selt)


def _sattn_kernel(pt_ref, q_ref, qt_ref, kn_ref, vn_ref, *rest, nblk, blocks_per_step):
    n_pages = 2 * blocks_per_step
    k_refs = rest[:n_pages]
    v_refs = rest[n_pages:2 * n_pages]
    o_ref, g_sc, m_sc, l_sc, o_sc = rest[2 * n_pages:]
    step = pl.program_id(1)
    tq, d_attn = q_ref.shape
    page = k_refs[0].shape[-1]
    rows = ATTN_HEADS * tq
    scale = HEAD_DIM ** -0.5

    def heads_to_rows(x):
        return jnp.concatenate([x[:, h * HEAD_DIM:(h + 1) * HEAD_DIM] for h in range(ATTN_HEADS)], axis=0)

    qr = q_ref[...]
    qs = (heads_to_rows(qr) * scale).astype(BF16)
    lane = lax.broadcasted_iota(jnp.int32, (rows, LANES), 1)
    row_head = _div_pow2(lax.broadcasted_iota(jnp.int32, (rows, d_attn), 0), tq)
    lane_head = _div_pow2(lax.broadcasted_iota(jnp.int32, (rows, d_attn), 1), HEAD_DIM)
    qbd = jnp.where(row_head == lane_head, jnp.concatenate([qr] * ATTN_HEADS, axis=0), 0.0)
    qbs = (qbd * scale).astype(BF16)

    @pl.when(step == 0)
    def _():
        m_sc[...] = jnp.zeros_like(m_sc)
        l_sc[...] = jnp.zeros_like(l_sc)

    m_acc, l_acc = m_sc[...], l_sc[...]
    for c in range(blocks_per_step):
        n = step * blocks_per_step + c
        k0 = k_refs[2 * c][...].reshape(d_attn, page)
        k1 = k_refs[2 * c + 1][...].reshape(d_attn, page)
        v0 = v_refs[2 * c][...].reshape(d_attn, page).astype(BF16)
        v1 = v_refs[2 * c + 1][...].reshape(d_attn, page).astype(BF16)
        kmean = jnp.sum(k0 + k1, axis=1, keepdims=True) * (1.0 / MOBA_BLOCK)
        g_sc[n] = jnp.sum((kmean * qt_ref[...]).reshape(ATTN_HEADS, HEAD_DIM, tq), axis=1)
        s = jnp.concatenate([jnp.dot(qbs, k0.astype(BF16), preferred_element_type=F32),
                             jnp.dot(qbs, k1.astype(BF16), preferred_element_type=F32)], axis=1)
        mx = jnp.max(s, axis=1, keepdims=True)
        p = jnp.exp(s - mx)
        l = jnp.sum(p, axis=1, keepdims=True)
        pb = p.astype(BF16)
        o_full = (lax.dot_general(pb[:, :page], v0, _NT, preferred_element_type=F32)
                  + lax.dot_general(pb[:, page:], v1, _NT, preferred_element_type=F32))
        o_sc[n] = jnp.concatenate([o_full[h * tq:(h + 1) * tq, h * HEAD_DIM:(h + 1) * HEAD_DIM]
                                   for h in range(ATTN_HEADS)], axis=0)
        hit = lane == n
        m_acc = jnp.where(hit, mx, m_acc)
        l_acc = jnp.where(hit, l, l_acc)
    m_sc[...] = m_acc
    l_sc[...] = l_acc

    @pl.when(step == pl.num_programs(1) - 1)
    def _():
        remaining = [g_sc[nn] for nn in range(nblk)]
        chosen = [jnp.zeros((ATTN_HEADS, tq), F32) for _ in range(nblk)]
        for _ in range(MOBA_TOPK):
            gmx = remaining[0]
            for nn in range(1, nblk):
                gmx = jnp.maximum(gmx, remaining[nn])
            open_ = gmx > -jnp.inf
            for nn in range(nblk):
                pick = (remaining[nn] == gmx) & open_
                open_ = open_ & jnp.logical_not(pick)
                chosen[nn] = jnp.where(pick, 1.0, chosen[nn])
                remaining[nn] = jnp.where(pick, -jnp.inf, remaining[nn])
        own_q = lax.broadcasted_iota(jnp.int32, (rows, tq), 1) == (
            lax.broadcasted_iota(jnp.int32, (rows, tq), 0) & (tq - 1))
        picked = jnp.zeros((rows, LANES), F32)
        for nn in range(nblk):
            per_row = jnp.broadcast_to(chosen[nn][:, None, :], (ATTN_HEADS, tq, tq)).reshape(rows, tq)
            flag = jnp.sum(jnp.where(own_q, per_row, 0.0), axis=1, keepdims=True)
            picked = jnp.where(lane == nn, flag, picked)
        selw = picked > 0.5
        m_all = m_sc[...]
        l_all = l_sc[...]
        knew = heads_to_rows(kn_ref[...]).astype(BF16)
        vnew = heads_to_rows(vn_ref[...]).astype(BF16)
        s_own = lax.dot_general(qs, knew, _NT, preferred_element_type=F32)
        ri = lax.broadcasted_iota(jnp.int32, (rows, rows), 0)
        ci = lax.broadcasted_iota(jnp.int32, (rows, rows), 1)
        ok = (_div_pow2(ri, tq) == _div_pow2(ci, tq)) & ((ci & (tq - 1)) <= (ri & (tq - 1)))
        s_own = jnp.where(ok, s_own, NEG_INF)
        m_sel = jnp.max(jnp.where(selw, m_all, -jnp.inf), axis=1, keepdims=True)
        mf = jnp.maximum(jnp.max(s_own, axis=1, keepdims=True), m_sel)
        w = jnp.where(selw, jnp.exp(m_all - mf), 0.0)
        p_own = jnp.exp(s_own - mf)
        lf = jnp.sum(w * l_all, axis=1, keepdims=True) + jnp.sum(p_own, axis=1, keepdims=True)
        of = jnp.dot(p_own.astype(BF16), vnew, preferred_element_type=F32)
        for nn in range(nblk):
            of = of + w[:, nn:nn + 1] * o_sc[nn]
        out = of / lf
        o_ref[...] = jnp.concatenate([out[h * tq:(h + 1) * tq, :] for h in range(ATTN_HEADS)],
                                     axis=1).astype(o_ref.dtype)


def _attn_sample(qr3, kr3, vr3, cache_kt, cache_vt, page_table, layer):
    bsz, tq, d_attn = qr3.shape
    page = cache_kt.shape[4]
    pages_per_blk = MOBA_BLOCK // page
    assert pages_per_blk == 2 and cache_kt.shape[2:4] == (ATTN_HEADS, HEAD_DIM) and page == LANES
    nblk = page_table.shape[1] // pages_per_blk
    rows = ATTN_HEADS * tq
    blocks_per_step = max(c for c in (4, 2, 1) if nblk % c == 0)
    pages_per_step = pages_per_blk * blocks_per_step
    tok_spec = pl.BlockSpec((None, tq, d_attn), lambda b, n, pt: (b, 0, 0))

    def page_spec(j):
        return pl.BlockSpec((None, None, ATTN_HEADS, HEAD_DIM, page),
                            lambda b, n, pt, j=j: (layer, pt[b, pages_per_step * n + j], 0, 0, 0))

    page_specs = [page_spec(j) for j in range(pages_per_step)]
    grid_spec = pltpu.PrefetchScalarGridSpec(
        num_scalar_prefetch=1,
        grid=(bsz, nblk // blocks_per_step),
        in_specs=[tok_spec, pl.BlockSpec((None, d_attn, tq), lambda b, n, pt: (b, 0, 0)), tok_spec, tok_spec]
        + page_specs + page_specs,
        out_specs=pl.BlockSpec((None, tq, d_attn), lambda b, n, pt: (b, 0, 0)),
        scratch_shapes=[pltpu.VMEM((nblk, ATTN_HEADS, tq), F32), pltpu.VMEM((rows, LANES), F32),
                        pltpu.VMEM((rows, LANES), F32), pltpu.VMEM((nblk, rows, HEAD_DIM), F32)],
    )
    return pl.pallas_call(
        functools.partial(_sattn_kernel, nblk=nblk, blocks_per_step=blocks_per_step),
        grid_spec=grid_spec,
        out_shape=jax.ShapeDtypeStruct((bsz, tq, d_attn), BF16),
        compiler_params=_cparams(2),
        name="attn_sample",
    )(page_table, qr3, jnp.swapaxes(qr3, 1, 2), kr3, vr3,
      *([cache_kt] * pages_per_step), *([cache_vt] * pages_per_step))


def _merge_kernel(x_ref, ya_ref, ys_ref, yc_ref, g0_ref, g1_ref, g2_ref,
                  wr_ref, ws_ref, wa_ref, wo_ref, lg_ref, lb_ref, o_ref, *, alpha):
    merged = jax.nn.sigmoid(g0_ref[...]) * jnp.dot(ya_ref[...], wr_ref[...], preferred_element_type=F32)
    merged = merged + jax.nn.sigmoid(g1_ref[...]) * jnp.dot(ys_ref[...], ws_ref[...],
                                                           preferred_element_type=F32)
    merged = merged + jax.nn.sigmoid(g2_ref[...]) * jnp.dot(yc_ref[...], wa_ref[...],
                                                           preferred_element_type=F32)
    z = alpha * x_ref[...] + jnp.dot(merged.astype(BF16), wo_ref[...], preferred_element_type=F32)
    o_ref[...] = _layer_norm(z, lg_ref[...], lb_ref[...])


def _merge(x, ya, ys, yc, proj2, lw, tm, gate_col0, alpha):
    n, d = x.shape
    row = lambda w: pl.BlockSpec((tm, w), lambda i: (i, 0))
    full = lambda a: pl.BlockSpec(a.shape, lambda i: (0,) * a.ndim)
    gspec = lambda c: pl.BlockSpec((tm, d), lambda i, c=c: (i, c))
    ws = [lw["w_br_rnn"], lw["w_br_ssm"], lw["w_br_attn"], lw["w_out"], lw["ln1_g"], lw["ln1_b"]]
    return pl.pallas_call(
        functools.partial(_merge_kernel, alpha=alpha),
        grid=(n // tm,),
        in_specs=[row(d), row(ya.shape[1]), row(ys.shape[1]), row(yc.shape[1]),
                  gspec(gate_col0), gspec(gate_col0 + 1), gspec(gate_col0 + 2)] + [full(a) for a in ws],
        out_specs=row(d),
        out_shape=jax.ShapeDtypeStruct((n, d), F32),
        compiler_params=_cparams(1),
        name="merge_ln",
    )(x, ya, ys, yc, proj2, proj2, proj2, *ws)


def _ffn_kernel(x_ref, wg_ref, wu_ref, wo_ref, lg_ref, lb_ref, o_ref, xb_sc, acc_sc, *, alpha):
    j = pl.program_id(1)

    @pl.when(j == 0)
    def _():
        xb_sc[...] = x_ref[...].astype(BF16)
        acc_sc[...] = jnp.zeros_like(acc_sc)

    xb = xb_sc[...]
    hg = jnp.dot(xb, wg_ref[...], preferred_element_type=F32)
    hu = jnp.dot(xb, wu_ref[...], preferred_element_type=F32)
    h = (hg * jax.nn.sigmoid(hg)) * hu
    acc_sc[...] += jnp.dot(h.astype(BF16), wo_ref[...], preferred_element_type=F32)

    @pl.when(j == pl.num_programs(1) - 1)
    def _():
        o_ref[...] = _layer_norm(alpha * x_ref[...] + acc_sc[...], lg_ref[...], lb_ref[...])


def _ffn(x, lw, tm, tf, alpha):
    n, d = x.shape
    d_ff = lw["w_ffn_out"].shape[0]
    nj = d_ff // tf
    return pl.pallas_call(
        functools.partial(_ffn_kernel, alpha=alpha),
        grid=(n // tm, nj),
        in_specs=[pl.BlockSpec((tm, d), lambda i, j: (i, 0)),
                  pl.BlockSpec((d, tf), lambda i, j: (0, j)),
                  pl.BlockSpec((d, tf), lambda i, j: (0, nj + j)),
                  pl.BlockSpec((tf, d), lambda i, j: (j, 0)),
                  pl.BlockSpec((1, d), lambda i, j: (0, 0)),
                  pl.BlockSpec((1, d), lambda i, j: (0, 0))],
        out_specs=pl.BlockSpec((tm, d), lambda i, j: (i, 0)),
        out_shape=jax.ShapeDtypeStruct((n, d), F32),
        scratch_shapes=[pltpu.VMEM((tm, d), BF16), pltpu.VMEM((tm, d), F32)],
        compiler_params=_cparams(2),
        name="ffn_ln",
    )(x, lw["w_ffn_in"], lw["w_ffn_in"], lw["w_ffn_out"], lw["ln2_g"], lw["ln2_b"])


def _block_diag(w, per_block):
    n, r, c = w.shape
    eye = jnp.eye(per_block, dtype=w.dtype)
    out = jnp.einsum("kgrc,gh->kgrhc", w.reshape(n // per_block, per_block, r, c), eye)
    return out.reshape(n // per_block, per_block * r, per_block * c)


def _prep_layer(l, p):
    heads_per_tile = MXU_DIM // (p["rg_w_a"].shape[-1])
    row = lambda a: a[l].reshape(1, -1)
    groups = p["s5_b_re"].shape[1]
    lw = {
        "w_in": p["w_in"][l].astype(BF16),
        "conv_w": p["conv_w"][l],
        "conv_b": row(p["conv_b"]),
        "rg_wbd": jnp.concatenate([_block_diag(p["rg_w_a"][l], heads_per_tile),
                                   _block_diag(p["rg_w_x"][l], heads_per_tile)], axis=2).astype(BF16),
        "rg_b_a": row(p["rg_b_a"]), "rg_b_x": row(p["rg_b_x"]), "rg_lambda": row(p["rg_lambda"]),
        "s5_lam_re": row(p["s5_lambda_re"]), "s5_lam_im": row(p["s5_lambda_im"]),
        "s5_log_dt": jnp.repeat(p["s5_log_step"][l], SSM_STATE).reshape(1, -1),
        "s5_wbr": _block_diag(jnp.swapaxes(p["s5_b_re"][l], 1, 2), groups // 2),
        "s5_wbi": _block_diag(jnp.swapaxes(p["s5_b_im"][l], 1, 2), groups // 2),
        "s5_cre": _block_diag(jnp.swapaxes(p["s5_c_re"][l], 1, 2), LANES // SSM_GROUP).astype(BF16),
        "s5_cim": _block_diag(jnp.swapaxes(p["s5_c_im"][l], 1, 2), LANES // SSM_GROUP).astype(BF16),
        "s5_d": row(p["s5_d"]),
        "s5_w_glu": p["s5_w_glu"][l].astype(BF16),
        "s5_b_glu": row(p["s5_b_glu"]),
        "w_br_rnn": p["w_br_rnn"][l].astype(BF16),
        "w_br_ssm": p["w_br_ssm"][l].astype(BF16),
        "w_br_attn": p["w_br_attn"][l].astype(BF16),
        "w_out": p["w_out"][l].astype(BF16),
        "ln1_g": row(p["ln1_g"]), "ln1_b": row(p["ln1_b"]),
        "w_ffn_in": p["w_ffn_in"][l].astype(BF16),
        "w_ffn_out": p["w_ffn_out"][l].astype(BF16),
        "ln2_g": row(p["ln2_g"]), "ln2_b": row(p["ln2_b"]),
    }
    return lw


def _rope_tables(pos0, t):
    half = HEAD_DIM // 2
    inv = jnp.power(ROPE_THETA, -jnp.arange(half, dtype=F32) * (2.0 / HEAD_DIM))
    ang = (pos0 + jnp.arange(t)).astype(F32)[:, None] * inv
    cos, sin = jnp.cos(ang), jnp.sin(ang)
    reps = LANES // HEAD_DIM
    cos_t = jnp.tile(jnp.concatenate([cos, cos], axis=1), (1, reps))
    sin_t = jnp.tile(jnp.concatenate([-sin, sin], axis=1), (1, reps))
    return cos_t, sin_t


def _largest_tile(n, cap):
    t = min(n, cap)
    while n % t:
        t //= 2
    return t


def _trunk_layer(x3, lw, alpha, cbuf8, h0, s5r0, s5i0, *, prompt, pos0=0, cache=None):
    bsz, t, d = x3.shape
    n = bsz * t
    d_rnn = h0.shape[-1]
    d_ssm = lw["s5_d"].shape[-1]
    d_attn = ATTN_HEADS * HEAD_DIM
    x2 = x3.reshape(n, d)
    proj2 = _proj(x2, lw["w_in"], _largest_tile(n, 256), 1536)
    proj3 = proj2.reshape(bsz, t, -1)

    if prompt:
        bb, tc = 1, MOBA_BLOCK
    else:
        bb, tc = _largest_tile(bsz, 32), t
    assert tc % SUBLANES == 0 and tc & (tc - 1) == 0 and t % tc == 0
    ya, conv_new, h_new = _rglru(proj3, cbuf8, h0, lw, bb, tc)
    ys, s5r, s5i = _s5(proj3, s5r0, s5i0, lw, bb, tc, d_rnn // d_ssm)

    col0 = (d_rnn + d_ssm) // d_attn
    tm_rope = MOBA_BLOCK
    assert n % tm_rope == 0
    if prompt:
        assert t % MOBA_BLOCK == 0 and t // MOBA_BLOCK <= GATE_SLOTS
        cos_t, sin_t = _rope_tables(0, t)
        qt, krt, vrt, kb, vt, selt = _rope(proj2, cos_t, sin_t, tm_rope, True, t // MOBA_BLOCK, col0)
        yc = _attn_prompt(qt, kb, vt, selt, bsz, t)
        new_k = krt.reshape(bsz, ATTN_HEADS, HEAD_DIM, t).transpose(0, 3, 1, 2)
        new_v = vrt.reshape(bsz, ATTN_HEADS, HEAD_DIM, t).transpose(0, 3, 1, 2)
    else:
        assert tm_rope % t == 0
        cos_t, sin_t = _rope_tables(pos0, t)
        cos_t = jnp.tile(cos_t, (tm_rope // t, 1))
        sin_t = jnp.tile(sin_t, (tm_rope // t, 1))
        qr, kr, vr = _rope(proj2, cos_t, sin_t, tm_rope, False, 1, col0)
        cache_kt, cache_vt, page_table, layer = cache
        yc = _attn_sample(qr.reshape(bsz, t, d_attn), kr.reshape(bsz, t, d_attn), vr.reshape(bsz, t, d_attn),
                          cache_kt, cache_vt, page_table, layer).reshape(n, d_attn)
        new_k = kr.reshape(bsz, t, ATTN_HEADS, HEAD_DIM)
        new_v = vr.reshape(bsz, t, ATTN_HEADS, HEAD_DIM)

    gate_col0 = (d_rnn + d_ssm + 3 * d_attn) // d
    x1 = _merge(x2, ya.reshape(n, d_rnn), ys.reshape(n, d_ssm), yc, proj2, lw, _largest_tile(n, 256),
                gate_col0, alpha)
    x_out = _ffn(x1, lw, _largest_tile(n, 1024), MXU_DIM, alpha)
    return (x_out.reshape(bsz, t, d), new_k, new_v, conv_new, h_new.reshape(bsz, d_rnn), s5r, s5i)


def kernel(x_prompt, x_sample, cache_k, cache_v, state_conv, state_rglru, state_s5_re, state_s5_im, page_table,
           w_in, conv_w, conv_b, rg_w_a, rg_b_a, rg_w_x, rg_b_x, rg_lambda,
           s5_lambda_re, s5_lambda_im, s5_b_re, s5_b_im, s5_c_re, s5_c_im, s5_d, s5_log_step, s5_w_glu, s5_b_glu,
           w_br_rnn, w_br_ssm, w_br_attn, w_out, ln1_g, ln1_b, w_ffn_in, w_ffn_out, ln2_g, ln2_b):
    params = dict(w_in=w_in, conv_w=conv_w, conv_b=conv_b, rg_w_a=rg_w_a, rg_b_a=rg_b_a, rg_w_x=rg_w_x,
                  rg_b_x=rg_b_x, rg_lambda=rg_lambda, s5_lambda_re=s5_lambda_re, s5_lambda_im=s5_lambda_im,
                  s5_b_re=s5_b_re, s5_b_im=s5_b_im, s5_c_re=s5_c_re, s5_c_im=s5_c_im, s5_d=s5_d,
                  s5_log_step=s5_log_step, s5_w_glu=s5_w_glu, s5_b_glu=s5_b_glu, w_br_rnn=w_br_rnn,
                  w_br_ssm=w_br_ssm, w_br_attn=w_br_attn, w_out=w_out, ln1_g=ln1_g, ln1_b=ln1_b,
                  w_ffn_in=w_ffn_in, w_ffn_out=w_ffn_out, ln2_g=ln2_g, ln2_b=ln2_b)
    depth = w_in.shape[0]
    alpha = (2.0 * depth) ** 0.25
    bp = x_prompt.shape[0]
    bs = x_sample.shape[0]
    d_rnn = state_rglru.shape[-1]
    groups, n_p = state_s5_re.shape[-2:]
    n_state = groups * n_p
    n_pages = page_table.shape[1]
    page = cache_k.shape[2]
    past_len = n_pages * page
    d_attn = ATTN_HEADS * HEAD_DIM
    assert past_len % MOBA_BLOCK == 0 and x_sample.shape[1] <= MOBA_BLOCK
    cache_kt = cache_k.transpose(0, 1, 3, 4, 2)
    cache_vt = cache_v.transpose(0, 1, 3, 4, 2)

    zeros_p = lambda *s: jnp.zeros((bp,) + s, F32)
    yp, ys = x_prompt, x_sample
    outs_p, outs_s = [], []
    for l in range(depth):
        lw = _prep_layer(l, params)
        res = _trunk_layer(yp, lw, alpha, zeros_p(SUBLANES, d_rnn), zeros_p(1, d_rnn),
                           zeros_p(1, n_state), zeros_p(1, n_state), prompt=True)
        yp = res[0]
        outs_p.append(res[1:])
        cbuf8 = jnp.pad(state_conv[l], ((0, 0), (SUBLANES - (CONV_WIDTH - 1), 0), (0, 0)))
        res = _trunk_layer(ys, lw, alpha, cbuf8, state_rglru[l].reshape(bs, 1, d_rnn),
                           state_s5_re[l].reshape(bs, 1, n_state), state_s5_im[l].reshape(bs, 1, n_state),
                           prompt=False, pos0=past_len, cache=(cache_kt, cache_vt, page_table, l))
        ys = res[0]
        outs_s.append(res[1:])

    def stack(outs, i, shape=None):
        arrs = [o[i] if shape is None else o[i].reshape(shape) for o in outs]
        return jnp.stack(arrs)

    return (yp, ys,
            stack(outs_p, 0), stack(outs_p, 1), stack(outs_p, 2), stack(outs_p, 3),
            stack(outs_p, 4, (bp, groups, n_p)), stack(outs_p, 5, (bp, groups, n_p)),
            stack(outs_s, 0), stack(outs_s, 1), stack(outs_s, 2), stack(outs_s, 3),
            stack(outs_s, 4, (bs, groups, n_p)), stack(outs_s, 5, (bs, groups, n_p)))
```

```python
import functools

import jax
import jax.numpy as jnp
import numpy as np
from jax import lax
from jax.experimental import pallas as pl
from jax.experimental.pallas import tpu as pltpu

F32 = jnp.float32
BF16 = jnp.bfloat16

ATTN_HEADS = 8
HEAD_DIM = 64
RNN_HEADS = 16
CONV_WIDTH = 4
LRU_C = 8.0
SSM_GROUP = 16
SSM_STATE = 64
MOBA_BLOCK = 256
MOBA_TOPK = 3
ROPE_THETA = 10000.0
LN_EPS = 1e-5
NEG_INF = -1e30
LOG2_E = 1.4426950408889634

LANES = 128
SUBLANES = 8
MXU_DIM = 256
VMEM_LIMIT_BYTES = 56 * 1024 * 1024

GATE_SLOTS = 32
SCAN_SHIFTS = (1, 2, 4)
_NT = (((1,), (1,)), ((), ()))


def _cparams(n_axes):
    return pltpu.CompilerParams(dimension_semantics=("arbitrary",) * n_axes,
                                vmem_limit_bytes=VMEM_LIMIT_BYTES)


def _resident(shape):
    return pl.BlockSpec(shape, lambda *_: (0,) * len(shape), pipeline_mode=pl.Buffered(1))


def _div_pow2(x, d):
    assert d & (d - 1) == 0
    return lax.shift_right_logical(x, d.bit_length() - 1)


def _layer_norm(z, g, b):
    mu = jnp.mean(z, axis=-1, keepdims=True)
    zc = z - mu
    var = jnp.mean(zc * zc, axis=-1, keepdims=True)
    return zc * lax.rsqrt(var + LN_EPS) * g + b


def _proj_kernel(x_ref, w_ref, o_ref, *, tn):
    xb = x_ref[...].astype(BF16)
    for j in range(o_ref.shape[1] // tn):
        cs = slice(j * tn, (j + 1) * tn)
        o_ref[:, cs] = jnp.dot(xb, w_ref[:, cs], preferred_element_type=F32)


def _proj(x, w, tm, tn):
    n, d = x.shape
    nout = w.shape[1]
    return pl.pallas_call(
        functools.partial(_proj_kernel, tn=tn),
        grid=(n // tm,),
        in_specs=[pl.BlockSpec((tm, d), lambda i: (i, 0)),
                  _resident((d, nout))],
        out_specs=pl.BlockSpec((tm, nout), lambda i: (i, 0)),
        out_shape=jax.ShapeDtypeStruct((n, nout), F32),
        compiler_params=_cparams(1),
        name="proj",
    )(x, w)


def _rglru_kernel(x_ref, cbuf_ref, h0_ref, cw_ref, cb_ref, wbd_ref, ba_ref, bx_ref, lam_ref,
                  ya_ref, cnew_ref, hnew_ref, xbuf, hcar, *, bb, tc):
    ti = pl.program_id(1)
    nt = pl.num_programs(1)
    c_dim = x_ref.shape[-1]
    m = bb * tc

    @pl.when(ti == 0)
    def _():
        xbuf[:, 0:SUBLANES, :] = cbuf_ref[...]
        hcar[...] = h0_ref[...]

    x = x_ref[...]
    xbuf[:, SUBLANES:SUBLANES + tc, :] = x
    cw = cw_ref[...]
    xc = cb_ref[...] + cw[0:1] * xbuf[:, 5:5 + tc, :]
    xc = xc + cw[1:2] * xbuf[:, 6:6 + tc, :]
    xc = xc + cw[2:3] * xbuf[:, 7:7 + tc, :]
    xc = xc + cw[3:4] * x
    xc2 = xc.reshape(m, c_dim)
    xcb = xc2.astype(BF16)

    nl = -lam_ref[...]
    softplus = jnp.maximum(nl, 0.0) + jnp.log1p(jnp.exp(-jnp.abs(nl)))
    c_row = -LRU_C * softplus
    assert tc == SUBLANES or bb == 1
    row = lax.broadcasted_iota(jnp.int32, (m // SUBLANES, SUBLANES, MXU_DIM), 1)
    hc = hcar[...]
    for g in range(c_dim // MXU_DIM):
        sl = slice(g * MXU_DIM, (g + 1) * MXU_DIM)
        ga = jnp.dot(xcb[:, sl], wbd_ref[g], preferred_element_type=F32)
        r = jax.nn.sigmoid(ga[:, :MXU_DIM] + ba_ref[:, sl])
        i = jax.nn.sigmoid(ga[:, MXU_DIM:] + bx_ref[:, sl])
        log_a = c_row[:, sl] * r
        a = jnp.exp(log_a)
        mult = jnp.sqrt(jnp.maximum(-jnp.tanh(log_a) * (a * a + 1.0), 0.0))
        b = mult * (i * xc2[:, sl])
        a = a.reshape(m // SUBLANES, SUBLANES, MXU_DIM)
        b = b.reshape(m // SUBLANES, SUBLANES, MXU_DIM)
        s = 1
        while s < SUBLANES:
            msk = row >= s
            a_sh = jnp.where(msk, pltpu.roll(a, s, 1), 1.0)
            b_sh = jnp.where(msk, pltpu.roll(b, s, 1), 0.0)
            b = a * b_sh + b
            a = a * a_sh
            s *= 2
        a = a.reshape(m, MXU_DIM)
        b = b.reshape(m, MXU_DIM)
        if tc == SUBLANES:
            hcg = jnp.broadcast_to(hc[:, :, sl], (bb, tc, MXU_DIM)).reshape(m, MXU_DIM)
            h = b + a * hcg
        else:
            carry = hc[0, :, sl]
            groups = []
            for j in range(tc // SUBLANES):
                rs = slice(j * SUBLANES, (j + 1) * SUBLANES)
                hj = b[rs] + a[rs] * carry
                groups.append(hj)
                carry = hj[SUBLANES - 1:SUBLANES]
            h = jnp.concatenate(groups, axis=0)
        h3 = h.reshape(bb, tc, MXU_DIM)
        ya_ref[:, :, sl] = h3.astype(ya_ref.dtype)
        hcar[:, :, sl] = h3[:, tc - 1:tc, :]

    xbuf[:, 0:SUBLANES, :] = xbuf[:, tc:tc + SUBLANES, :]

    @pl.when(ti == nt - 1)
    def _():
        cnew_ref[...] = xbuf[:, 5:8, :]
        hnew_ref[...] = hcar[...]


def _rglru(proj3, cbuf8, h0, lw, bb, tc):
    bsz, t, _ = proj3.shape
    c_dim = h0.shape[-1]
    full = lambda *shape: pl.BlockSpec(shape, lambda bi, ti: (0,) * len(shape))
    kern = functools.partial(_rglru_kernel, bb=bb, tc=tc)
    return pl.pallas_call(
        kern,
        grid=(bsz // bb, t // tc),
        in_specs=[pl.BlockSpec((bb, tc, c_dim), lambda bi, ti: (bi, ti, 0)),
                  pl.BlockSpec((bb, SUBLANES, c_dim), lambda bi, ti: (bi, 0, 0)),
                  pl.BlockSpec((bb, 1, c_dim), lambda bi, ti: (bi, 0, 0)),
                  full(CONV_WIDTH, c_dim), full(1, c_dim),
                  full(c_dim // MXU_DIM, MXU_DIM, 2 * MXU_DIM),
                  full(1, c_dim), full(1, c_dim), full(1, c_dim)],
        out_specs=[pl.BlockSpec((bb, tc, c_dim), lambda bi, ti: (bi, ti, 0)),
                   pl.BlockSpec((bb, CONV_WIDTH - 1, c_dim), lambda bi, ti: (bi, 0, 0)),
                   pl.BlockSpec((bb, 1, c_dim), lambda bi, ti: (bi, 0, 0))],
        out_shape=[jax.ShapeDtypeStruct((bsz, t, c_dim), BF16),
                   jax.ShapeDtypeStruct((bsz, CONV_WIDTH - 1, c_dim), F32),
                   jax.ShapeDtypeStruct((bsz, 1, c_dim), F32)],
        scratch_shapes=[pltpu.VMEM((bb, tc + SUBLANES, c_dim), F32),
                        pltpu.VMEM((bb, 1, c_dim), F32)],
        compiler_params=_cparams(2),
        name="rglru",
    )(proj3, cbuf8, h0, lw["conv_w"], lw["conv_b"], lw["rg_wbd"], lw["rg_b_a"], lw["rg_b_x"],
      lw["rg_lambda"])


def _cmul(ar, ai, br, bi):
    return ar * br - ai * bi, ar * bi + ai * br


def _s5_segment_scan(hr_sc, hi_sc, pwr_sc, pwi_sc, cr_sc, ci_sc, seg_len, lane_chunk):
    n_state = hr_sc.shape[-1]
    for lc in range(n_state // lane_chunk):
        ls = slice(lc * lane_chunk, (lc + 1) * lane_chunk)
        a1r = jnp.broadcast_to(pwr_sc[0:1, ls], (SUBLANES, lane_chunk))
        a1i = jnp.broadcast_to(pwi_sc[0:1, ls], (SUBLANES, lane_chunk))

        def local_step(j, h, ls=ls, a1r=a1r, a1i=a1i):
            r0 = pl.multiple_of(j * SUBLANES, SUBLANES)
            dr, di = _cmul(a1r, a1i, h[0], h[1])
            xr = dr + hr_sc[0, pl.ds(r0, SUBLANES), ls]
            xi = di + hi_sc[0, pl.ds(r0, SUBLANES), ls]
            hr_sc[0, pl.ds(r0, SUBLANES), ls] = xr
            hi_sc[0, pl.ds(r0, SUBLANES), ls] = xi
            return xr, xi

        zero = jnp.zeros((SUBLANES, lane_chunk), F32)
        er, ei = lax.fori_loop(0, seg_len, local_step, (zero, zero))

        alr = pwr_sc[seg_len - 1:seg_len, ls]
        ali = pwi_sc[seg_len - 1:seg_len, ls]
        cr, ci = cr_sc[0, :, ls], ci_sc[0, :, ls]
        entry_r, entry_i = [], []
        for s in range(SUBLANES):
            entry_r.append(cr)
            entry_i.append(ci)
            dr, di = _cmul(alr, ali, cr, ci)
            cr, ci = er[s:s + 1] + dr, ei[s:s + 1] + di
        cr_sc[0, :, ls] = cr
        ci_sc[0, :, ls] = ci
        hin_r = jnp.concatenate(entry_r, axis=0)
        hin_i = jnp.concatenate(entry_i, axis=0)

        def fix_step(j, d, ls=ls, a1r=a1r, a1i=a1i):
            r0 = pl.multiple_of(j * SUBLANES, SUBLANES)
            hr_sc[0, pl.ds(r0, SUBLANES), ls] += d[0]
            hi_sc[0, pl.ds(r0, SUBLANES), ls] += d[1]
            return _cmul(a1r, a1i, d[0], d[1])

        lax.fori_loop(0, seg_len, fix_step, _cmul(a1r, a1i, hin_r, hin_i))


def _s5_kernel(u_ref, h0r_ref, h0i_ref, lamr_ref, lami_ref, ldt_ref, wbr_ref, wbi_ref,
               cre_ref, cim_ref, d_ref, wglu_ref, bglu_ref,
               ys_ref, sr_ref, si_ref,
               wb_sc, apr_sc, api_sc, lvr_sc, lvi_sc, pwr_sc, pwi_sc, col_sc, hr_sc, hi_sc, cr_sc, ci_sc,
               *, bb, tc, lane_chunk):
    bi = pl.program_id(0)
    ti = pl.program_id(1)
    nt = pl.num_programs(1)
    m = bb * tc
    n_state = hr_sc.shape[-1]
    d_ssm = u_ref.shape[-1]
    half = n_state // 2
    segmented = tc > SUBLANES
    seg_len = tc // SUBLANES
    assert not segmented or bb == 1

    @pl.when((bi == 0) & (ti == 0))
    def _():
        dt = jnp.exp(ldt_ref[...])
        lr = lamr_ref[...]
        li = lami_ref[...]
        mag = jnp.exp(lr * dt)
        abr = mag * jnp.cos(li * dt)
        abi = mag * jnp.sin(li * dt)
        nr = abr - 1.0
        den = lr * lr + li * li
        zr = (nr * lr + abi * li) / den
        zi = (abi * lr - nr * li) / den
        for kb in range(2):
            ks = slice(kb * half, (kb + 1) * half)
            br = wbr_ref[kb]
            bim = wbi_ref[kb]
            wb_sc[kb, :, 0:half] = (zr[:, ks] * br - zi[:, ks] * bim).astype(BF16)
            wb_sc[kb, :, half:n_state] = (zr[:, ks] * bim + zi[:, ks] * br).astype(BF16)
        row8 = lax.broadcasted_iota(jnp.int32, (SUBLANES, n_state), 0)
        pr = jnp.broadcast_to(abr, (SUBLANES, n_state))
        pi = jnp.broadcast_to(abi, (SUBLANES, n_state))
        for s in SCAN_SHIFTS:
            msk = row8 >= s
            qr = jnp.where(msk, pltpu.roll(pr, s, 0), 1.0)
            qi = jnp.where(msk, pltpu.roll(pi, s, 0), 0.0)
            pr, pi = _cmul(pr, pi, qr, qi)
        apr_sc[...] = pr
        api_sc[...] = pi
        for idx, s in enumerate(SCAN_SHIFTS):
            inside = row8 >= s
            lvr_sc[idx] = jnp.where(inside, jnp.broadcast_to(pr[s - 1:s], (SUBLANES, n_state)), 0.0)
            lvi_sc[idx] = jnp.where(inside, jnp.broadcast_to(pi[s - 1:s], (SUBLANES, n_state)), 0.0)
        if segmented:
            pwr_sc[0:SUBLANES] = pr
            pwi_sc[0:SUBLANES] = pi
            have = SUBLANES
            while have < seg_len:
                tr, ti_ = _cmul(pwr_sc[have - 1:have], pwi_sc[have - 1:have], pwr_sc[0:have], pwi_sc[0:have])
                pwr_sc[have:2 * have] = tr
                pwi_sc[have:2 * have] = ti_
                have *= 2

    @pl.when(ti == 0)
    def _():
        cr_sc[...] = h0r_ref[...]
        ci_sc[...] = h0i_ref[...]

    n_col = d_ssm // LANES
    if segmented:
        for c in range(n_col):
            col_sc[c] = u_ref[0, :, c * LANES:(c + 1) * LANES]
        u = jnp.concatenate(
            [jnp.concatenate([col_sc[c, pl.ds(j, SUBLANES, stride=seg_len), :] for j in range(seg_len)], axis=0)
             for c in range(n_col)], axis=1)
    else:
        u = u_ref[...].reshape(m, d_ssm)
    ub = u.astype(BF16)
    k_half = d_ssm // 2
    for kb in range(2):
        bu = jnp.dot(ub[:, kb * k_half:(kb + 1) * k_half], wb_sc[kb], preferred_element_type=F32)
        hr_sc[:, :, kb * half:(kb + 1) * half] = bu[:, :half].reshape(bb, tc, half)
        hi_sc[:, :, kb * half:(kb + 1) * half] = bu[:, half:].reshape(bb, tc, half)

    if segmented:
        _s5_segment_scan(hr_sc, hi_sc, pwr_sc, pwi_sc, cr_sc, ci_sc, seg_len, lane_chunk)
    rows8 = bb * SUBLANES
    for lc in range(0 if segmented else n_state // lane_chunk):
        ls = slice(lc * lane_chunk, (lc + 1) * lane_chunk)

        def per_seq(tab):
            return jnp.broadcast_to(tab[None], (bb, SUBLANES, lane_chunk)).reshape(rows8, lane_chunk)

        steps = tuple((s, per_seq(lvr_sc[idx, :, ls]), per_seq(lvi_sc[idx, :, ls]))
                      for idx, s in enumerate(SCAN_SHIFTS))
        prt = per_seq(apr_sc[:, ls])
        pit = per_seq(api_sc[:, ls])

        def body(j, carry, ls=ls, steps=steps, prt=prt, pit=pit):
            cr, ci = carry
            r0 = pl.multiple_of(j * SUBLANES, SUBLANES)
            xr = hr_sc[:, pl.ds(r0, SUBLANES), ls].reshape(rows8, lane_chunk)
            xi = hi_sc[:, pl.ds(r0, SUBLANES), ls].reshape(rows8, lane_chunk)
            for s, ar, ai in steps:
                dr, di = _cmul(ar, ai, pltpu.roll(xr, s, 0), pltpu.roll(xi, s, 0))
                xr = xr + dr
                xi = xi + di
            crb = jnp.broadcast_to(cr, (bb, SUBLANES, lane_chunk)).reshape(rows8, lane_chunk)
            cib = jnp.broadcast_to(ci, (bb, SUBLANES, lane_chunk)).reshape(rows8, lane_chunk)
            dr, di = _cmul(prt, pit, crb, cib)
            xr3 = (xr + dr).reshape(bb, SUBLANES, lane_chunk)
            xi3 = (xi + di).reshape(bb, SUBLANES, lane_chunk)
            hr_sc[:, pl.ds(r0, SUBLANES), ls] = xr3
            hi_sc[:, pl.ds(r0, SUBLANES), ls] = xi3
            return xr3[:, SUBLANES - 1:SUBLANES, :], xi3[:, SUBLANES - 1:SUBLANES, :]

        cr, ci = lax.fori_loop(0, tc // SUBLANES, body, (cr_sc[:, :, ls], ci_sc[:, :, ls]))
        cr_sc[:, :, ls] = cr
        ci_sc[:, :, ls] = ci

    n_pack = cre_ref.shape[0]
    k_pack = n_state // n_pack
    parts = []
    for p4 in range(n_pack):
        ks = slice(p4 * k_pack, (p4 + 1) * k_pack)
        hrb = hr_sc[:, :, ks].reshape(m, k_pack).astype(BF16)
        hib = hi_sc[:, :, ks].reshape(m, k_pack).astype(BF16)
        parts.append(jnp.dot(hrb, cre_ref[p4], preferred_element_type=F32)
                     - jnp.dot(hib, cim_ref[p4], preferred_element_type=F32))
    y = jnp.concatenate(parts, axis=1) + d_ref[...] * u
    g = y * (0.5 * (1.0 + jnp.tanh(np.sqrt(2.0 / np.pi).astype(np.float32) * (y + 0.044715 * (y * y * y)))))
    z = jnp.dot(g.astype(BF16), wglu_ref[...], preferred_element_type=F32) + bglu_ref[...]
    out = g * jax.nn.sigmoid(z)
    if segmented:
        for c in range(n_col):
            col_sc[c] = out[:, c * LANES:(c + 1) * LANES]
        for c in range(n_col):
            for s in range(SUBLANES):
                ys_ref[0, s * seg_len:(s + 1) * seg_len, c * LANES:(c + 1) * LANES] = (
                    col_sc[c, pl.ds(s, seg_len, stride=SUBLANES), :].astype(ys_ref.dtype))
    else:
        ys_ref[...] = out.reshape(bb, tc, d_ssm).astype(ys_ref.dtype)

    @pl.when(ti == nt - 1)
    def _():
        sr_ref[...] = cr_sc[...]
        si_ref[...] = ci_sc[...]


def _s5(proj3, h0r, h0i, lw, bb, tc, col_block):
    bsz, t, _ = proj3.shape
    n_state = h0r.shape[-1]
    d_ssm = lw["s5_d"].shape[-1]
    n_pack = lw["s5_cre"].shape[0]
    full = lambda *shape: pl.BlockSpec(shape, lambda bi, ti: (0,) * len(shape))
    lane_chunk = max(LANES, 4 * LANES // bb)
    n_pow = max(SUBLANES, tc // SUBLANES)
    assert n_pow & (n_pow - 1) == 0
    kern = functools.partial(_s5_kernel, bb=bb, tc=tc, lane_chunk=lane_chunk)
    state_spec = pl.BlockSpec((bb, 1, n_state), lambda bi, ti: (bi, 0, 0))
    return pl.pallas_call(
        kern,
        grid=(bsz // bb, t // tc),
        in_specs=[pl.BlockSpec((bb, tc, d_ssm), lambda bi, ti: (bi, ti, col_block)),
                  state_spec, state_spec,
                  full(1, n_state), full(1, n_state), full(1, n_state),
                  full(2, d_ssm // 2, n_state // 2), full(2, d_ssm // 2, n_state // 2),
                  full(n_pack, n_state // n_pack, d_ssm // n_pack),
                  full(n_pack, n_state // n_pack, d_ssm // n_pack),
                  full(1, d_ssm), full(d_ssm, d_ssm), full(1, d_ssm)],
        out_specs=[pl.BlockSpec((bb, tc, d_ssm), lambda bi, ti: (bi, ti, 0)), state_spec, state_spec],
        out_shape=[jax.ShapeDtypeStruct((bsz, t, d_ssm), BF16),
                   jax.ShapeDtypeStruct((bsz, 1, n_state), F32),
                   jax.ShapeDtypeStruct((bsz, 1, n_state), F32)],
        scratch_shapes=[pltpu.VMEM((2, d_ssm // 2, n_state), BF16),
                        pltpu.VMEM((SUBLANES, n_state), F32), pltpu.VMEM((SUBLANES, n_state), F32),
                        pltpu.VMEM((len(SCAN_SHIFTS), SUBLANES, n_state), F32),
                        pltpu.VMEM((len(SCAN_SHIFTS), SUBLANES, n_state), F32),
                        pltpu.VMEM((n_pow, n_state), F32), pltpu.VMEM((n_pow, n_state), F32),
                        pltpu.VMEM((d_ssm // LANES, tc, LANES), F32),
                        pltpu.VMEM((bb, tc, n_state), F32), pltpu.VMEM((bb, tc, n_state), F32),
                        pltpu.VMEM((bb, 1, n_state), F32), pltpu.VMEM((bb, 1, n_state), F32)],
        compiler_params=_cparams(2),
        name="s5",
    )(proj3, h0r, h0i, lw["s5_lam_re"], lw["s5_lam_im"], lw["s5_log_dt"], lw["s5_wbr"], lw["s5_wbi"],
      lw["s5_cre"], lw["s5_cim"], lw["s5_d"], lw["s5_w_glu"], lw["s5_b_glu"])


def _rope_kernel(q_ref, k_ref, v_ref, cos_ref, sin_ref, *rest, gate, blocks_per_seq):
    if gate:
        qs_ref, kr_ref, vr_ref, kb_ref, vb_ref, sel_ref, km_sc = rest
    else:
        qs_ref, kr_ref, vr_ref = rest
    tm, d_attn = q_ref.shape
    reps = d_attn // LANES
    cos = jnp.concatenate([cos_ref[...]] * reps, axis=1)
    sin = jnp.concatenate([sin_ref[...]] * reps, axis=1)
    lane = lax.broadcasted_iota(jnp.int32, (tm, d_attn), 1)
    first_half = (lane & (HEAD_DIM - 1)) < HEAD_DIM // 2

    def rot(x):
        partner = jnp.where(first_half, pltpu.roll(x, d_attn - HEAD_DIM // 2, 1),
                            pltpu.roll(x, HEAD_DIM // 2, 1))
        return x * cos + partner * sin

    qr = rot(q_ref[...])
    kr = rot(k_ref[...])
    v = v_ref[...]
    if not gate:
        qs_ref[...] = qr
        kr_ref[...] = kr
        vr_ref[...] = v
        return
    krt = kr.T
    vt = v.T
    kr_ref[...] = krt
    vr_ref[...] = vt
    qs_ref[...] = (qr * (HEAD_DIM ** -0.5 * LOG2_E)).T.astype(BF16)
    kb_ref[...] = kr.astype(BF16)
    vb_ref[...] = vt.astype(BF16)

    tb = pl.program_id(0) % blocks_per_seq

    @pl.when(tb == 0)
    def _():
        km_sc[...] = jnp.zeros_like(km_sc)

    gates = lax.dot_general(km_sc[...], qr, _NT, precision=lax.Precision.HIGHEST,
                            preferred_element_type=F32)
    slot = lax.broadcasted_iota(jnp.int32, (GATE_SLOTS, tm), 0)
    slotf = slot.astype(F32)
    for h in range(ATTN_HEADS):
        hs = slice(h * GATE_SLOTS, (h + 1) * GATE_SLOTS)
        gh = jnp.where(slot < tb, gates[hs, :], -jnp.inf)
        sel = jnp.zeros((GATE_SLOTS, tm), F32)
        for _ in range(MOBA_TOPK):
            mx = jnp.max(gh, axis=0, keepdims=True)
            cand = (gh == mx) & (mx > -jnp.inf)
            first = jnp.min(jnp.where(cand, slotf, float(GATE_SLOTS)), axis=0, keepdims=True)
            pick = slotf == first
            sel = jnp.where(pick, 1.0, sel)
            gh = jnp.where(pick, -jnp.inf, gh)
        sel_ref[hs, :] = sel

    km = jnp.sum(kr, axis=0, keepdims=True) * (1.0 / MOBA_BLOCK)
    lane_row = lax.broadcasted_iota(jnp.int32, (1, d_attn), 1)
    for h in range(ATTN_HEADS):
        km_sc[pl.ds(h * GATE_SLOTS + tb, 1), :] = jnp.where(_div_pow2(lane_row, HEAD_DIM) == h, km, 0.0)


def _rope(proj2, cos_t, sin_t, tm, gate, blocks_per_seq, col0):
    n = proj2.shape[0]
    d_attn = ATTN_HEADS * HEAD_DIM
    n_tab = cos_t.shape[0] // tm
    qkv_spec = lambda c: pl.BlockSpec((tm, d_attn), lambda i, c=c: (i, c))
    tab_spec = pl.BlockSpec((tm, LANES), lambda i: (i % n_tab, 0))
    row_spec = pl.BlockSpec((tm, d_attn), lambda i: (i, 0))
    kern = functools.partial(_rope_kernel, gate=gate, blocks_per_seq=blocks_per_seq)
    if gate:
        n_col = ATTN_HEADS * GATE_SLOTS
        bsz = n // (tm * blocks_per_seq)
        t = tm * blocks_per_seq
        nb = blocks_per_seq
        col_spec = lambda rows: pl.BlockSpec((None, rows, tm), lambda i: (i // nb, 0, i % nb))
        out_specs = [col_spec(d_attn), col_spec(d_attn), col_spec(d_attn), row_spec, col_spec(d_attn),
                     col_spec(n_col)]
        out_shape = [jax.ShapeDtypeStruct((bsz, d_attn, t), BF16), jax.ShapeDtypeStruct((bsz, d_attn, t), F32),
                     jax.ShapeDtypeStruct((bsz, d_attn, t), F32), jax.ShapeDtypeStruct((n, d_attn), BF16),
                     jax.ShapeDtypeStruct((bsz, d_attn, t), BF16), jax.ShapeDtypeStruct((bsz, n_col, t), F32)]
        scratch = [pltpu.VMEM((n_col, d_attn), F32)]
    else:
        out_specs = [row_spec] * 3
        out_shape = [jax.ShapeDtypeStruct((n, d_attn), F32)] * 3
        scratch = []
    return pl.pallas_call(
        kern,
        grid=(n // tm,),
        in_specs=[qkv_spec(col0), qkv_spec(col0 + 1), qkv_spec(col0 + 2), tab_spec, tab_spec],
        out_specs=out_specs,
        out_shape=out_shape,
        scratch_shapes=scratch,
        compiler_params=_cparams(1),
        name="rope_gate" if gate else "rope",
    )(proj2, proj2, proj2, cos_t, sin_t)


def _attn_kernel(qt_ref, k_ref, vt_ref, sel_ref, o_ref, acc_sc):
    qi = pl.program_id(2)
    width, blk = qt_ref.shape
    nh = width // HEAD_DIM
    qt = qt_ref[...]
    row_head = _div_pow2(lax.broadcasted_iota(jnp.int32, qt.shape, 0), HEAD_DIM)
    zero = jnp.zeros_like(qt)
    q_heads = jnp.concatenate([jnp.where(row_head == e, qt, zero) for e in range(nh)], axis=1)
    keyi = lax.broadcasted_iota(jnp.int32, (blk, blk), 0)
    qcol = lax.broadcasted_iota(jnp.int32, (blk, blk), 1)
    causal = keyi <= qcol

    r0 = pl.multiple_of(qi * blk, blk)
    ko = k_ref[pl.ds(r0, blk), :]
    vo = vt_ref[:, pl.ds(r0, blk)]
    s_own = jnp.dot(ko, q_heads, preferred_element_type=F32)
    ones_own = jnp.ones((SUBLANES, blk), BF16)
    stats = []
    for e in range(nh):
        s = jnp.where(causal, s_own[:, e * blk:(e + 1) * blk], NEG_INF)
        mx = jnp.max(s, axis=0, keepdims=True)
        p = jnp.exp2(s - mx).astype(BF16)
        stats += [mx, jnp.dot(ones_own, p, preferred_element_type=F32)[0:1]]
        acc_sc[e] = jnp.dot(vo[e * HEAD_DIM:(e + 1) * HEAD_DIM, :], p, preferred_element_type=F32)

    ones_pair = jnp.ones((SUBLANES, 2 * blk), BF16)

    def body(j, carry):
        rn = pl.multiple_of(j * (2 * blk), 2 * blk)
        kn = k_ref[pl.ds(rn, 2 * blk), :]
        vn = vt_ref[:, pl.ds(rn, 2 * blk)]
        s_all = jnp.dot(kn, q_heads, preferred_element_type=F32)
        out = []
        for e in range(nh):
            mx, l = carry[2 * e], carry[2 * e + 1]
            mn = mx
            halves = []
            for c in range(2):
                keep = sel_ref[pl.ds(e * GATE_SLOTS + 2 * j + c, 1), :] > 0.5
                sc = s_all[c * blk:(c + 1) * blk, e * blk:(e + 1) * blk]
                mn = jnp.maximum(mn, jnp.where(keep, jnp.max(sc, axis=0, keepdims=True), NEG_INF))
                halves.append((sc, keep))
            alpha = jnp.exp2(mx - mn)
            p = jnp.concatenate([jnp.exp2(sc - jnp.where(keep, mn, -NEG_INF)).astype(BF16)
                                 for sc, keep in halves], axis=0)
            l = alpha * l + jnp.dot(ones_pair, p, preferred_element_type=F32)[0:1]
            acc_sc[e] = alpha * acc_sc[e] + jnp.dot(vn[e * HEAD_DIM:(e + 1) * HEAD_DIM, :], p,
                                                    preferred_element_type=F32)
            out += [mn, l]
        return tuple(out)

    res = lax.fori_loop(0, (qi + 1) // 2, body, tuple(stats))
    ot = jnp.concatenate([acc_sc[e] / res[2 * e + 1] for e in range(nh)], axis=0)
    o_ref[...] = ot.T.astype(o_ref.dtype)


def _attn_prompt(qt, kb, vt, selt, bsz, t):
    d_attn = ATTN_HEADS * HEAD_DIM
    nq = t // MOBA_BLOCK
    kb3 = kb.reshape(bsz, t, d_attn)
    width = MXU_DIM
    nh = width // HEAD_DIM
    return pl.pallas_call(
        _attn_kernel,
        grid=(bsz, d_attn // width, nq),
        in_specs=[pl.BlockSpec((None, width, MOBA_BLOCK), lambda b, hg, qi: (b, hg, qi)),
                  pl.BlockSpec((None, t, width), lambda b, hg, qi: (b, 0, hg)),
                  pl.BlockSpec((None, width, t), lambda b, hg, qi: (b, hg, 0)),
                  pl.BlockSpec((None, nh * GATE_SLOTS, MOBA_BLOCK), lambda b, hg, qi: (b, hg, qi))],
        out_specs=pl.BlockSpec((MOBA_BLOCK, width), lambda b, hg, qi: (b * nq + qi, hg)),
        out_shape=jax.ShapeDtypeStruct((bsz * t, d_attn), BF16),
        scratch_shapes=[pltpu.VMEM((nh, HEAD_DIM, MOBA_BLOCK), F32)],
        compiler_params=_cparams(3),
        name="attn_prompt",
    )(qt, kb3, vt, selt)


def _sattn_kernel(pt_ref, q_ref, qt_ref, kn_ref, vn_ref, *rest, nblk, blocks_per_step):
    n_pages = 2 * blocks_per_step
    k_refs = rest[:n_pages]
    v_refs = rest[n_pages:2 * n_pages]
    o_ref, g_sc, m_sc, l_sc, o_sc = rest[2 * n_pages:]
    step = pl.program_id(1)
    tq, d_attn = q_ref.shape
    page = k_refs[0].shape[-1]
    rows = ATTN_HEADS * tq
    scale = HEAD_DIM ** -0.5

    def heads_to_rows(x):
        return jnp.concatenate([x[:, h * HEAD_DIM:(h + 1) * HEAD_DIM] for h in range(ATTN_HEADS)], axis=0)

    qr = q_ref[...]
    qs = (heads_to_rows(qr) * scale).astype(BF16)
    lane = lax.broadcasted_iota(jnp.int32, (rows, LANES), 1)
    row_head = _div_pow2(lax.broadcasted_iota(jnp.int32, (rows, d_attn), 0), tq)
    lane_head = _div_pow2(lax.broadcasted_iota(jnp.int32, (rows, d_attn), 1), HEAD_DIM)
    qbd = jnp.where(row_head == lane_head, jnp.concatenate([qr] * ATTN_HEADS, axis=0), 0.0)
    qbs = (qbd * scale).astype(BF16)

    @pl.when(step == 0)
    def _():
        m_sc[...] = jnp.zeros_like(m_sc)
        l_sc[...] = jnp.zeros_like(l_sc)

    m_acc, l_acc = m_sc[...], l_sc[...]
    for c in range(blocks_per_step):
        n = step * blocks_per_step + c
        k0 = k_refs[2 * c][...].reshape(d_attn, page)
        k1 = k_refs[2 * c + 1][...].reshape(d_attn, page)
        v0 = v_refs[2 * c][...].reshape(d_attn, page).astype(BF16)
        v1 = v_refs[2 * c + 1][...].reshape(d_attn, page).astype(BF16)
        kmean = jnp.sum(k0 + k1, axis=1, keepdims=True) * (1.0 / MOBA_BLOCK)
        g_sc[n] = jnp.sum((kmean * qt_ref[...]).reshape(ATTN_HEADS, HEAD_DIM, tq), axis=1)
        s = jnp.concatenate([jnp.dot(qbs, k0.astype(BF16), preferred_element_type=F32),
                             jnp.dot(qbs, k1.astype(BF16), preferred_element_type=F32)], axis=1)
        mx = jnp.max(s, axis=1, keepdims=True)
        p = jnp.exp(s - mx)
        l = jnp.sum(p, axis=1, keepdims=True)
        pb = p.astype(BF16)
        o_full = (lax.dot_general(pb[:, :page], v0, _NT, preferred_element_type=F32)
                  + lax.dot_general(pb[:, page:], v1, _NT, preferred_element_type=F32))
        o_sc[n] = jnp.concatenate([o_full[h * tq:(h + 1) * tq, h * HEAD_DIM:(h + 1) * HEAD_DIM]
                                   for h in range(ATTN_HEADS)], axis=0)
        hit = lane == n
        m_acc = jnp.where(hit, mx, m_acc)
        l_acc = jnp.where(hit, l, l_acc)
    m_sc[...] = m_acc
    l_sc[...] = l_acc

    @pl.when(step == pl.num_programs(1) - 1)
    def _():
        remaining = [g_sc[nn] for nn in range(nblk)]
        chosen = [jnp.zeros((ATTN_HEADS, tq), F32) for _ in range(nblk)]
        for _ in range(MOBA_TOPK):
            gmx = remaining[0]
            for nn in range(1, nblk):
                gmx = jnp.maximum(gmx, remaining[nn])
            open_ = gmx > -jnp.inf
            for nn in range(nblk):
                pick = (remaining[nn] == gmx) & open_
                open_ = open_ & jnp.logical_not(pick)
                chosen[nn] = jnp.where(pick, 1.0, chosen[nn])
                remaining[nn] = jnp.where(pick, -jnp.inf, remaining[nn])
        own_q = lax.broadcasted_iota(jnp.int32, (rows, tq), 1) == (
            lax.broadcasted_iota(jnp.int32, (rows, tq), 0) & (tq - 1))
        picked = jnp.zeros((rows, LANES), F32)
        for nn in range(nblk):
            per_row = jnp.broadcast_to(chosen[nn][:, None, :], (ATTN_HEADS, tq, tq)).reshape(rows, tq)
            flag = jnp.sum(jnp.where(own_q, per_row, 0.0), axis=1, keepdims=True)
            picked = jnp.where(lane == nn, flag, picked)
        selw = picked > 0.5
        m_all = m_sc[...]
        l_all = l_sc[...]
        knew = heads_to_rows(kn_ref[...]).astype(BF16)
        vnew = heads_to_rows(vn_ref[...]).astype(BF16)
        s_own = lax.dot_general(qs, knew, _NT, preferred_element_type=F32)
        ri = lax.broadcasted_iota(jnp.int32, (rows, rows), 0)
        ci = lax.broadcasted_iota(jnp.int32, (rows, rows), 1)
        ok = (_div_pow2(ri, tq) == _div_pow2(ci, tq)) & ((ci & (tq - 1)) <= (ri & (tq - 1)))
        s_own = jnp.where(ok, s_own, NEG_INF)
        m_sel = jnp.max(jnp.where(selw, m_all, -jnp.inf), axis=1, keepdims=True)
        mf = jnp.maximum(jnp.max(s_own, axis=1, keepdims=True), m_sel)
        w = jnp.where(selw, jnp.exp(m_all - mf), 0.0)
        p_own = jnp.exp(s_own - mf)
        lf = jnp.sum(w * l_all, axis=1, keepdims=True) + jnp.sum(p_own, axis=1, keepdims=True)
        of = jnp.dot(p_own.astype(BF16), vnew, preferred_element_type=F32)
        for nn in range(nblk):
            of = of + w[:, nn:nn + 1] * o_sc[nn]
        out = of / lf
        o_ref[...] = jnp.concatenate([out[h * tq:(h + 1) * tq, :] for h in range(ATTN_HEADS)],
                                     axis=1).astype(o_ref.dtype)


def _attn_sample(qr3, kr3, vr3, cache_kt, cache_vt, page_table, layer):
    bsz, tq, d_attn = qr3.shape
    page = cache_kt.shape[4]
    pages_per_blk = MOBA_BLOCK // page
    assert pages_per_blk == 2 and cache_kt.shape[2:4] == (ATTN_HEADS, HEAD_DIM) and page == LANES
    nblk = page_table.shape[1] // pages_per_blk
    rows = ATTN_HEADS * tq
    blocks_per_step = max(c for c in (4, 2, 1) if nblk % c == 0)
    pages_per_step = pages_per_blk * blocks_per_step
    tok_spec = pl.BlockSpec((None, tq, d_attn), lambda b, n, pt: (b, 0, 0))

    def page_spec(j):
        return pl.BlockSpec((None, None, ATTN_HEADS, HEAD_DIM, page),
                            lambda b, n, pt, j=j: (layer, pt[b, pages_per_step * n + j], 0, 0, 0))

    page_specs = [page_spec(j) for j in range(pages_per_step)]
    grid_spec = pltpu.PrefetchScalarGridSpec(
        num_scalar_prefetch=1,
        grid=(bsz, nblk // blocks_per_step),
        in_specs=[tok_spec, pl.BlockSpec((None, d_attn, tq), lambda b, n, pt: (b, 0, 0)), tok_spec, tok_spec]
        + page_specs + page_specs,
        out_specs=pl.BlockSpec((None, tq, d_attn), lambda b, n, pt: (b, 0, 0)),
        scratch_shapes=[pltpu.VMEM((nblk, ATTN_HEADS, tq), F32), pltpu.VMEM((rows, LANES), F32),
                        pltpu.VMEM((rows, LANES), F32), pltpu.VMEM((nblk, rows, HEAD_DIM), F32)],
    )
    return pl.pallas_call(
        functools.partial(_sattn_kernel, nblk=nblk, blocks_per_step=blocks_per_step),
        grid_spec=grid_spec,
        out_shape=jax.ShapeDtypeStruct((bsz, tq, d_attn), BF16),
        compiler_params=_cparams(2),
        name="attn_sample",
    )(page_table, qr3, jnp.swapaxes(qr3, 1, 2), kr3, vr3,
      *([cache_kt] * pages_per_step), *([cache_vt] * pages_per_step))


def _merge_kernel(x_ref, ya_ref, ys_ref, yc_ref, g0_ref, g1_ref, g2_ref,
                  wr_ref, ws_ref, wa_ref, wo_ref, lg_ref, lb_ref, o_ref, *, alpha):
    merged = jax.nn.sigmoid(g0_ref[...]) * jnp.dot(ya_ref[...], wr_ref[...], preferred_element_type=F32)
    merged = merged + jax.nn.sigmoid(g1_ref[...]) * jnp.dot(ys_ref[...], ws_ref[...],
                                                           preferred_element_type=F32)
    merged = merged + jax.nn.sigmoid(g2_ref[...]) * jnp.dot(yc_ref[...], wa_ref[...],
                                                           preferred_element_type=F32)
    z = alpha * x_ref[...] + jnp.dot(merged.astype(BF16), wo_ref[...], preferred_element_type=F32)
    o_ref[...] = _layer_norm(z, lg_ref[...], lb_ref[...])


def _merge(x, ya, ys, yc, proj2, lw, tm, gate_col0, alpha):
    n, d = x.shape
    row = lambda w: pl.BlockSpec((tm, w), lambda i: (i, 0))
    gspec = lambda c: pl.BlockSpec((tm, d), lambda i, c=c: (i, c))
    ws = [lw["w_br_rnn"], lw["w_br_ssm"], lw["w_br_attn"], lw["w_out"], lw["ln1_g"], lw["ln1_b"]]
    return pl.pallas_call(
        functools.partial(_merge_kernel, alpha=alpha),
        grid=(n // tm,),
        in_specs=[row(d), row(ya.shape[1]), row(ys.shape[1]), row(yc.shape[1]),
                  gspec(gate_col0), gspec(gate_col0 + 1), gspec(gate_col0 + 2)] + [_resident(a.shape) for a in ws],
        out_specs=row(d),
        out_shape=jax.ShapeDtypeStruct((n, d), F32),
        compiler_params=_cparams(1),
        name="merge_ln",
    )(x, ya, ys, yc, proj2, proj2, proj2, *ws)


def _ffn_kernel(x_ref, wi_ref, wo_ref, lg_ref, lb_ref, o_ref, h_sc, *, alpha, tf):
    d_ff = wo_ref.shape[0]
    x = x_ref[...]
    xb = x.astype(BF16)
    for j in range(d_ff // tf):
        hg = jnp.dot(xb, wi_ref[:, j * tf:(j + 1) * tf], preferred_element_type=F32)
        hu = jnp.dot(xb, wi_ref[:, d_ff + j * tf:d_ff + (j + 1) * tf], preferred_element_type=F32)
        h_sc[:, j * tf:(j + 1) * tf] = ((hg * jax.nn.sigmoid(hg)) * hu).astype(BF16)
    y = jnp.dot(h_sc[...], wo_ref[...], preferred_element_type=F32)
    o_ref[...] = _layer_norm(alpha * x + y, lg_ref[...], lb_ref[...])


def _ffn(x, lw, tm, alpha):
    n, d = x.shape
    d_ff = lw["w_ffn_out"].shape[0]
    tf = max(c for c in range(LANES, 6 * MXU_DIM + 1, LANES) if d_ff % c == 0)
    return pl.pallas_call(
        functools.partial(_ffn_kernel, alpha=alpha, tf=tf),
        grid=(n // tm,),
        in_specs=[pl.BlockSpec((tm, d), lambda i: (i, 0)),
                  _resident((d, 2 * d_ff)), _resident((d_ff, d)), _resident((1, d)), _resident((1, d))],
        out_specs=pl.BlockSpec((tm, d), lambda i: (i, 0)),
        out_shape=jax.ShapeDtypeStruct((n, d), F32),
        scratch_shapes=[pltpu.VMEM((tm, d_ff), BF16)],
        compiler_params=_cparams(1),
        name="ffn_ln",
    )(x, lw["w_ffn_in"], lw["w_ffn_out"], lw["ln2_g"], lw["ln2_b"])


def _block_diag(w, per_block):
    n, r, c = w.shape
    eye = jnp.eye(per_block, dtype=w.dtype)
    out = jnp.einsum("kgrc,gh->kgrhc", w.reshape(n // per_block, per_block, r, c), eye)
    return out.reshape(n // per_block, per_block * r, per_block * c)


def _prep_layer(l, p):
    heads_per_tile = MXU_DIM // (p["rg_w_a"].shape[-1])
    row = lambda a: a[l].reshape(1, -1)
    groups = p["s5_b_re"].shape[1]
    lw = {
        "w_in": p["w_in"][l].astype(BF16),
        "conv_w": p["conv_w"][l],
        "conv_b": row(p["conv_b"]),
        "rg_wbd": jnp.concatenate([_block_diag(p["rg_w_a"][l], heads_per_tile),
                                   _block_diag(p["rg_w_x"][l], heads_per_tile)], axis=2).astype(BF16),
        "rg_b_a": row(p["rg_b_a"]), "rg_b_x": row(p["rg_b_x"]), "rg_lambda": row(p["rg_lambda"]),
        "s5_lam_re": row(p["s5_lambda_re"]), "s5_lam_im": row(p["s5_lambda_im"]),
        "s5_log_dt": jnp.repeat(p["s5_log_step"][l], SSM_STATE).reshape(1, -1),
        "s5_wbr": _block_diag(jnp.swapaxes(p["s5_b_re"][l], 1, 2), groups // 2),
        "s5_wbi": _block_diag(jnp.swapaxes(p["s5_b_im"][l], 1, 2), groups // 2),
        "s5_cre": _block_diag(jnp.swapaxes(p["s5_c_re"][l], 1, 2), LANES // SSM_GROUP).astype(BF16),
        "s5_cim": _block_diag(jnp.swapaxes(p["s5_c_im"][l], 1, 2), LANES // SSM_GROUP).astype(BF16),
        "s5_d": row(p["s5_d"]),
        "s5_w_glu": p["s5_w_glu"][l].astype(BF16),
        "s5_b_glu": row(p["s5_b_glu"]),
        "w_br_rnn": p["w_br_rnn"][l].astype(BF16),
        "w_br_ssm": p["w_br_ssm"][l].astype(BF16),
        "w_br_attn": p["w_br_attn"][l].astype(BF16),
        "w_out": p["w_out"][l].astype(BF16),
        "ln1_g": row(p["ln1_g"]), "ln1_b": row(p["ln1_b"]),
        "w_ffn_in": p["w_ffn_in"][l].astype(BF16),
        "w_ffn_out": p["w_ffn_out"][l].astype(BF16),
        "ln2_g": row(p["ln2_g"]), "ln2_b": row(p["ln2_b"]),
    }
    return lw


def _rope_tables(pos0, t):
    half = HEAD_DIM // 2
    inv = jnp.power(ROPE_THETA, -jnp.arange(half, dtype=F32) * (2.0 / HEAD_DIM))
    ang = (pos0 + jnp.arange(t)).astype(F32)[:, None] * inv
    cos, sin = jnp.cos(ang), jnp.sin(ang)
    reps = LANES // HEAD_DIM
    cos_t = jnp.tile(jnp.concatenate([cos, cos], axis=1), (1, reps))
    sin_t = jnp.tile(jnp.concatenate([-sin, sin], axis=1), (1, reps))
    return cos_t, sin_t


def _largest_tile(n, cap):
    t = min(n, cap)
    while n % t:
        t //= 2
    return t


def _trunk_layer(x3, lw, alpha, cbuf8, h0, s5r0, s5i0, *, prompt, pos0=0, cache=None):
    bsz, t, d = x3.shape
    n = bsz * t
    d_rnn = h0.shape[-1]
    d_ssm = lw["s5_d"].shape[-1]
    d_attn = ATTN_HEADS * HEAD_DIM
    x2 = x3.reshape(n, d)
    proj2 = _proj(x2, lw["w_in"], _largest_tile(n, 256), 1536)
    proj3 = proj2.reshape(bsz, t, -1)

    if prompt:
        bb, tc = 1, MOBA_BLOCK
    else:
        bb, tc = _largest_tile(bsz, 32), t
    assert tc % SUBLANES == 0 and tc & (tc - 1) == 0 and t % tc == 0
    ya, conv_new, h_new = _rglru(proj3, cbuf8, h0, lw, bb, tc)
    ys, s5r, s5i = _s5(proj3, s5r0, s5i0, lw, bb, tc, d_rnn // d_ssm)

    col0 = (d_rnn + d_ssm) // d_attn
    tm_rope = MOBA_BLOCK
    assert n % tm_rope == 0
    if prompt:
        assert t % MOBA_BLOCK == 0 and t // MOBA_BLOCK <= GATE_SLOTS
        cos_t, sin_t = _rope_tables(0, t)
        qt, krt, vrt, kb, vt, selt = _rope(proj2, cos_t, sin_t, tm_rope, True, t // MOBA_BLOCK, col0)
        yc = _attn_prompt(qt, kb, vt, selt, bsz, t)
        new_k = krt.reshape(bsz, ATTN_HEADS, HEAD_DIM, t).transpose(0, 3, 1, 2)
        new_v = vrt.reshape(bsz, ATTN_HEADS, HEAD_DIM, t).transpose(0, 3, 1, 2)
    else:
        assert tm_rope % t == 0
        cos_t, sin_t = _rope_tables(pos0, t)
        cos_t = jnp.tile(cos_t, (tm_rope // t, 1))
        sin_t = jnp.tile(sin_t, (tm_rope // t, 1))
        qr, kr, vr = _rope(proj2, cos_t, sin_t, tm_rope, False, 1, col0)
        cache_kt, cache_vt, page_table, layer = cache
        yc = _attn_sample(qr.reshape(bsz, t, d_attn), kr.reshape(bsz, t, d_attn), vr.reshape(bsz, t, d_attn),
                          cache_kt, cache_vt, page_table, layer).reshape(n, d_attn)
        new_k = kr.reshape(bsz, t, ATTN_HEADS, HEAD_DIM)
        new_v = vr.reshape(bsz, t, ATTN_HEADS, HEAD_DIM)

    gate_col0 = (d_rnn + d_ssm + 3 * d_attn) // d
    x1 = _merge(x2, ya.reshape(n, d_rnn), ys.reshape(n, d_ssm), yc, proj2, lw, _largest_tile(n, 512),
                gate_col0, alpha)
    x_out = _ffn(x1, lw, _largest_tile(n, 256), alpha)
    return (x_out.reshape(bsz, t, d), new_k, new_v, conv_new, h_new.reshape(bsz, d_rnn), s5r, s5i)


def kernel(x_prompt, x_sample, cache_k, cache_v, state_conv, state_rglru, state_s5_re, state_s5_im, page_table,
           w_in, conv_w, conv_b, rg_w_a, rg_b_a, rg_w_x, rg_b_x, rg_lambda,
           s5_lambda_re, s5_lambda_im, s5_b_re, s5_b_im, s5_c_re, s5_c_im, s5_d, s5_log_step, s5_w_glu, s5_b_glu,
           w_br_rnn, w_br_ssm, w_br_attn, w_out, ln1_g, ln1_b, w_ffn_in, w_ffn_out, ln2_g, ln2_b):
    params = dict(w_in=w_in, conv_w=conv_w, conv_b=conv_b, rg_w_a=rg_w_a, rg_b_a=rg_b_a, rg_w_x=rg_w_x,
                  rg_b_x=rg_b_x, rg_lambda=rg_lambda, s5_lambda_re=s5_lambda_re, s5_lambda_im=s5_lambda_im,
                  s5_b_re=s5_b_re, s5_b_im=s5_b_im, s5_c_re=s5_c_re, s5_c_im=s5_c_im, s5_d=s5_d,
                  s5_log_step=s5_log_step, s5_w_glu=s5_w_glu, s5_b_glu=s5_b_glu, w_br_rnn=w_br_rnn,
                  w_br_ssm=w_br_ssm, w_br_attn=w_br_attn, w_out=w_out, ln1_g=ln1_g, ln1_b=ln1_b,
                  w_ffn_in=w_ffn_in, w_ffn_out=w_ffn_out, ln2_g=ln2_g, ln2_b=ln2_b)
    depth = w_in.shape[0]
    alpha = (2.0 * depth) ** 0.25
    bp = x_prompt.shape[0]
    bs = x_sample.shape[0]
    d_rnn = state_rglru.shape[-1]
    groups, n_p = state_s5_re.shape[-2:]
    n_state = groups * n_p
    n_pages = page_table.shape[1]
    page = cache_k.shape[2]
    past_len = n_pages * page
    d_attn = ATTN_HEADS * HEAD_DIM
    assert past_len % MOBA_BLOCK == 0 and x_sample.shape[1] <= MOBA_BLOCK
    cache_kt = cache_k.transpose(0, 1, 3, 4, 2)
    cache_vt = cache_v.transpose(0, 1, 3, 4, 2)

    zeros_p = lambda *s: jnp.zeros((bp,) + s, F32)
    yp, ys = x_prompt, x_sample
    outs_p, outs_s = [], []
    for l in range(depth):
        lw = _prep_layer(l, params)
        res = _trunk_layer(yp, lw, alpha, zeros_p(SUBLANES, d_rnn), zeros_p(1, d_rnn),
                           zeros_p(1, n_state), zeros_p(1, n_state), prompt=True)
        yp = res[0]
        outs_p.append(res[1:])
        cbuf8 = jnp.pad(state_conv[l], ((0, 0), (SUBLANES - (CONV_WIDTH - 1), 0), (0, 0)))
        res = _trunk_layer(ys, lw, alpha, cbuf8, state_rglru[l].reshape(bs, 1, d_rnn),
                           state_s5_re[l].reshape(bs, 1, n_state), state_s5_im[l].reshape(bs, 1, n_state),
                           prompt=False, pos0=past_len, cache=(cache_kt, cache_vt, page_table, l))
        ys = res[0]
        outs_s.append(res[1:])

    def stack(outs, i, shape=None):
        arrs = [o[i] if shape is None else o[i].reshape(shape) for o in outs]
        return jnp.stack(arrs)

    return (yp, ys,
            stack(outs_p, 0), stack(outs_p, 1), stack(outs_p, 2), stack(outs_p, 3),
            stack(outs_p, 4, (bp, groups, n_p)), stack(outs_p, 5, (bp, groups, n_p)),
            stack(outs_s, 0), stack(outs_s, 1), stack(outs_s, 2), stack(outs_s, 3),
            stack(outs_s, 4, (bs, groups, n_p)), stack(outs_s, 5, (bs, groups, n_p)))
```

```python
import functools

import jax
import jax.numpy as jnp
import numpy as np
from jax import lax
from jax.experimental import pallas as pl
from jax.experimental.pallas import tpu as pltpu

F32 = jnp.float32
BF16 = jnp.bfloat16

ATTN_HEADS = 8
HEAD_DIM = 64
RNN_HEADS = 16
CONV_WIDTH = 4
LRU_C = 8.0
SSM_GROUP = 16
SSM_STATE = 64
MOBA_BLOCK = 256
MOBA_TOPK = 3
ROPE_THETA = 10000.0
LN_EPS = 1e-5
NEG_INF = -1e30
LOG2_E = 1.4426950408889634

LANES = 128
SUBLANES = 8
MXU_DIM = 256
VMEM_LIMIT_BYTES = 56 * 1024 * 1024

GATE_SLOTS = 32
SCAN_SHIFTS = (1, 2, 4)
_NT = (((1,), (1,)), ((), ()))


def _cparams(n_axes):
    return pltpu.CompilerParams(dimension_semantics=("arbitrary",) * n_axes,
                                vmem_limit_bytes=VMEM_LIMIT_BYTES)


def _resident(shape, layer=None):
    if layer is None:
        return pl.BlockSpec(shape, lambda *_: (0,) * len(shape), pipeline_mode=pl.Buffered(1))
    return pl.BlockSpec((None,) + tuple(shape), lambda *_: (layer,) + (0,) * len(shape),
                        pipeline_mode=pl.Buffered(1))


def _div_pow2(x, d):
    assert d & (d - 1) == 0
    return lax.shift_right_logical(x, d.bit_length() - 1)


def _layer_norm(z, g, b):
    mu = jnp.mean(z, axis=-1, keepdims=True)
    zc = z - mu
    var = jnp.mean(zc * zc, axis=-1, keepdims=True)
    return zc * lax.rsqrt(var + LN_EPS) * g + b


def _proj_kernel(x_ref, w_ref, o_ref, *, tn):
    xb = x_ref[...].astype(BF16)
    for j in range(o_ref.shape[1] // tn):
        cs = slice(j * tn, (j + 1) * tn)
        o_ref[:, cs] = jnp.dot(xb, w_ref[:, cs], preferred_element_type=F32)


def _proj(x, w, layer, tm, tn):
    n, d = x.shape
    nout = w.shape[-1]
    return pl.pallas_call(
        functools.partial(_proj_kernel, tn=tn),
        grid=(n // tm,),
        in_specs=[pl.BlockSpec((tm, d), lambda i: (i, 0)),
                  _resident((d, nout), layer)],
        out_specs=pl.BlockSpec((tm, nout), lambda i: (i, 0)),
        out_shape=jax.ShapeDtypeStruct((n, nout), F32),
        compiler_params=_cparams(1),
        name="proj",
    )(x, w)


def _rglru_kernel(x_ref, cbuf_ref, h0_ref, cw_ref, cb_ref, wbd_ref, ba_ref, bx_ref, lam_ref,
                  ya_ref, cnew_ref, hnew_ref, xbuf, hcar, *, bb, tc):
    ti = pl.program_id(1)
    nt = pl.num_programs(1)
    c_dim = x_ref.shape[-1]
    m = bb * tc

    @pl.when(ti == 0)
    def _():
        xbuf[:, 0:SUBLANES, :] = cbuf_ref[...]
        hcar[...] = h0_ref[...]

    x = x_ref[...]
    xbuf[:, SUBLANES:SUBLANES + tc, :] = x
    cw = cw_ref[...]
    xc = cb_ref[...] + cw[0:1] * xbuf[:, 5:5 + tc, :]
    xc = xc + cw[1:2] * xbuf[:, 6:6 + tc, :]
    xc = xc + cw[2:3] * xbuf[:, 7:7 + tc, :]
    xc = xc + cw[3:4] * x
    xc2 = xc.reshape(m, c_dim)
    xcb = xc2.astype(BF16)

    nl = -lam_ref[...]
    softplus = jnp.maximum(nl, 0.0) + jnp.log1p(jnp.exp(-jnp.abs(nl)))
    c_row = -LRU_C * softplus
    assert tc == SUBLANES or bb == 1
    row = lax.broadcasted_iota(jnp.int32, (m // SUBLANES, SUBLANES, MXU_DIM), 1)
    hc = hcar[...]
    for g in range(c_dim // MXU_DIM):
        sl = slice(g * MXU_DIM, (g + 1) * MXU_DIM)
        ga = jnp.dot(xcb[:, sl], wbd_ref[g], preferred_element_type=F32)
        r = jax.nn.sigmoid(ga[:, :MXU_DIM] + ba_ref[:, sl])
        i = jax.nn.sigmoid(ga[:, MXU_DIM:] + bx_ref[:, sl])
        log_a = c_row[:, sl] * r
        a = jnp.exp(log_a)
        mult = jnp.sqrt(jnp.maximum(-jnp.tanh(log_a) * (a * a + 1.0), 0.0))
        b = mult * (i * xc2[:, sl])
        a = a.reshape(m // SUBLANES, SUBLANES, MXU_DIM)
        b = b.reshape(m // SUBLANES, SUBLANES, MXU_DIM)
        s = 1
        while s < SUBLANES:
            msk = row >= s
            a_sh = jnp.where(msk, pltpu.roll(a, s, 1), 1.0)
            b_sh = jnp.where(msk, pltpu.roll(b, s, 1), 0.0)
            b = a * b_sh + b
            a = a * a_sh
            s *= 2
        a = a.reshape(m, MXU_DIM)
        b = b.reshape(m, MXU_DIM)
        if tc == SUBLANES:
            hcg = jnp.broadcast_to(hc[:, :, sl], (bb, tc, MXU_DIM)).reshape(m, MXU_DIM)
            h = b + a * hcg
        else:
            carry = hc[0, :, sl]
            groups = []
            for j in range(tc // SUBLANES):
                rs = slice(j * SUBLANES, (j + 1) * SUBLANES)
                hj = b[rs] + a[rs] * carry
                groups.append(hj)
                carry = hj[SUBLANES - 1:SUBLANES]
            h = jnp.concatenate(groups, axis=0)
        h3 = h.reshape(bb, tc, MXU_DIM)
        ya_ref[:, :, sl] = h3.astype(ya_ref.dtype)
        hcar[:, :, sl] = h3[:, tc - 1:tc, :]

    xbuf[:, 0:SUBLANES, :] = xbuf[:, tc:tc + SUBLANES, :]

    @pl.when(ti == nt - 1)
    def _():
        cnew_ref[...] = xbuf[:, 5:8, :]
        hnew_ref[...] = hcar[...]


def _rglru(proj3, cbuf8, h0, lw, bb, tc):
    bsz, t, _ = proj3.shape
    c_dim = h0.shape[-1]
    full = lambda *shape: pl.BlockSpec(shape, lambda bi, ti: (0,) * len(shape))
    kern = functools.partial(_rglru_kernel, bb=bb, tc=tc)
    return pl.pallas_call(
        kern,
        grid=(bsz // bb, t // tc),
        in_specs=[pl.BlockSpec((bb, tc, c_dim), lambda bi, ti: (bi, ti, 0)),
                  pl.BlockSpec((bb, SUBLANES, c_dim), lambda bi, ti: (bi, 0, 0)),
                  pl.BlockSpec((bb, 1, c_dim), lambda bi, ti: (bi, 0, 0)),
                  full(CONV_WIDTH, c_dim), full(1, c_dim),
                  full(c_dim // MXU_DIM, MXU_DIM, 2 * MXU_DIM),
                  full(1, c_dim), full(1, c_dim), full(1, c_dim)],
        out_specs=[pl.BlockSpec((bb, tc, c_dim), lambda bi, ti: (bi, ti, 0)),
                   pl.BlockSpec((bb, CONV_WIDTH - 1, c_dim), lambda bi, ti: (bi, 0, 0)),
                   pl.BlockSpec((bb, 1, c_dim), lambda bi, ti: (bi, 0, 0))],
        out_shape=[jax.ShapeDtypeStruct((bsz, t, c_dim), BF16),
                   jax.ShapeDtypeStruct((bsz, CONV_WIDTH - 1, c_dim), F32),
                   jax.ShapeDtypeStruct((bsz, 1, c_dim), F32)],
        scratch_shapes=[pltpu.VMEM((bb, tc + SUBLANES, c_dim), F32),
                        pltpu.VMEM((bb, 1, c_dim), F32)],
        compiler_params=_cparams(2),
        name="rglru",
    )(proj3, cbuf8, h0, lw["conv_w"], lw["conv_b"], lw["rg_wbd"], lw["rg_b_a"], lw["rg_b_x"],
      lw["rg_lambda"])


def _cmul(ar, ai, br, bi):
    return ar * br - ai * bi, ar * bi + ai * br


def _s5_segment_scan(hr_sc, hi_sc, pwr_sc, pwi_sc, cr_sc, ci_sc, seg_len, lane_chunk):
    n_state = hr_sc.shape[-1]
    for lc in range(n_state // lane_chunk):
        ls = slice(lc * lane_chunk, (lc + 1) * lane_chunk)
        a1r = jnp.broadcast_to(pwr_sc[0:1, ls], (SUBLANES, lane_chunk))
        a1i = jnp.broadcast_to(pwi_sc[0:1, ls], (SUBLANES, lane_chunk))

        def local_step(j, h, ls=ls, a1r=a1r, a1i=a1i):
            r0 = pl.multiple_of(j * SUBLANES, SUBLANES)
            dr, di = _cmul(a1r, a1i, h[0], h[1])
            xr = dr + hr_sc[0, pl.ds(r0, SUBLANES), ls]
            xi = di + hi_sc[0, pl.ds(r0, SUBLANES), ls]
            hr_sc[0, pl.ds(r0, SUBLANES), ls] = xr
            hi_sc[0, pl.ds(r0, SUBLANES), ls] = xi
            return xr, xi

        zero = jnp.zeros((SUBLANES, lane_chunk), F32)
        er, ei = lax.fori_loop(0, seg_len, local_step, (zero, zero))

        alr = pwr_sc[seg_len - 1:seg_len, ls]
        ali = pwi_sc[seg_len - 1:seg_len, ls]
        cr, ci = cr_sc[0, :, ls], ci_sc[0, :, ls]
        entry_r, entry_i = [], []
        for s in range(SUBLANES):
            entry_r.append(cr)
            entry_i.append(ci)
            dr, di = _cmul(alr, ali, cr, ci)
            cr, ci = er[s:s + 1] + dr, ei[s:s + 1] + di
        cr_sc[0, :, ls] = cr
        ci_sc[0, :, ls] = ci
        hin_r = jnp.concatenate(entry_r, axis=0)
        hin_i = jnp.concatenate(entry_i, axis=0)

        def fix_step(j, d, ls=ls, a1r=a1r, a1i=a1i):
            r0 = pl.multiple_of(j * SUBLANES, SUBLANES)
            hr_sc[0, pl.ds(r0, SUBLANES), ls] += d[0]
            hi_sc[0, pl.ds(r0, SUBLANES), ls] += d[1]
            return _cmul(a1r, a1i, d[0], d[1])

        lax.fori_loop(0, seg_len, fix_step, _cmul(a1r, a1i, hin_r, hin_i))


def _s5_kernel(u_ref, h0r_ref, h0i_ref, lamr_ref, lami_ref, ldt_ref, wbr_ref, wbi_ref,
               cre_ref, cim_ref, d_ref, wglu_ref, bglu_ref,
               ys_ref, sr_ref, si_ref,
               wb_sc, apr_sc, api_sc, lvr_sc, lvi_sc, pwr_sc, pwi_sc, col_sc, hr_sc, hi_sc, cr_sc, ci_sc,
               *, bb, tc, lane_chunk):
    bi = pl.program_id(0)
    ti = pl.program_id(1)
    nt = pl.num_programs(1)
    m = bb * tc
    n_state = hr_sc.shape[-1]
    d_ssm = u_ref.shape[-1]
    half = n_state // 2
    segmented = tc > SUBLANES
    seg_len = tc // SUBLANES
    assert not segmented or bb == 1

    @pl.when((bi == 0) & (ti == 0))
    def _():
        dt = jnp.exp(ldt_ref[...])
        lr = lamr_ref[...]
        li = lami_ref[...]
        mag = jnp.exp(lr * dt)
        abr = mag * jnp.cos(li * dt)
        abi = mag * jnp.sin(li * dt)
        nr = abr - 1.0
        den = lr * lr + li * li
        zr = (nr * lr + abi * li) / den
        zi = (abi * lr - nr * li) / den
        for kb in range(2):
            ks = slice(kb * half, (kb + 1) * half)
            br = wbr_ref[kb]
            bim = wbi_ref[kb]
            wb_sc[kb, :, 0:half] = (zr[:, ks] * br - zi[:, ks] * bim).astype(BF16)
            wb_sc[kb, :, half:n_state] = (zr[:, ks] * bim + zi[:, ks] * br).astype(BF16)
        row8 = lax.broadcasted_iota(jnp.int32, (SUBLANES, n_state), 0)
        pr = jnp.broadcast_to(abr, (SUBLANES, n_state))
        pi = jnp.broadcast_to(abi, (SUBLANES, n_state))
        for s in SCAN_SHIFTS:
            msk = row8 >= s
            qr = jnp.where(msk, pltpu.roll(pr, s, 0), 1.0)
            qi = jnp.where(msk, pltpu.roll(pi, s, 0), 0.0)
            pr, pi = _cmul(pr, pi, qr, qi)
        apr_sc[...] = pr
        api_sc[...] = pi
        for idx, s in enumerate(SCAN_SHIFTS):
            inside = row8 >= s
            lvr_sc[idx] = jnp.where(inside, jnp.broadcast_to(pr[s - 1:s], (SUBLANES, n_state)), 0.0)
            lvi_sc[idx] = jnp.where(inside, jnp.broadcast_to(pi[s - 1:s], (SUBLANES, n_state)), 0.0)
        if segmented:
            pwr_sc[0:SUBLANES] = pr
            pwi_sc[0:SUBLANES] = pi
            have = SUBLANES
            while have < seg_len:
                tr, ti_ = _cmul(pwr_sc[have - 1:have], pwi_sc[have - 1:have], pwr_sc[0:have], pwi_sc[0:have])
                pwr_sc[have:2 * have] = tr
                pwi_sc[have:2 * have] = ti_
                have *= 2

    @pl.when(ti == 0)
    def _():
        cr_sc[...] = h0r_ref[...]
        ci_sc[...] = h0i_ref[...]

    n_col = d_ssm // LANES
    if segmented:
        for c in range(n_col):
            col_sc[c] = u_ref[0, :, c * LANES:(c + 1) * LANES]
        u = jnp.concatenate(
            [jnp.concatenate([col_sc[c, pl.ds(j, SUBLANES, stride=seg_len), :] for j in range(seg_len)], axis=0)
             for c in range(n_col)], axis=1)
    else:
        u = u_ref[...].reshape(m, d_ssm)
    ub = u.astype(BF16)
    k_half = d_ssm // 2
    for kb in range(2):
        bu = jnp.dot(ub[:, kb * k_half:(kb + 1) * k_half], wb_sc[kb], preferred_element_type=F32)
        hr_sc[:, :, kb * half:(kb + 1) * half] = bu[:, :half].reshape(bb, tc, half)
        hi_sc[:, :, kb * half:(kb + 1) * half] = bu[:, half:].reshape(bb, tc, half)

    if segmented:
        _s5_segment_scan(hr_sc, hi_sc, pwr_sc, pwi_sc, cr_sc, ci_sc, seg_len, lane_chunk)
    rows8 = bb * SUBLANES
    for lc in range(0 if segmented else n_state // lane_chunk):
        ls = slice(lc * lane_chunk, (lc + 1) * lane_chunk)

        def per_seq(tab):
            return jnp.broadcast_to(tab[None], (bb, SUBLANES, lane_chunk)).reshape(rows8, lane_chunk)

        steps = tuple((s, per_seq(lvr_sc[idx, :, ls]), per_seq(lvi_sc[idx, :, ls]))
                      for idx, s in enumerate(SCAN_SHIFTS))
        prt = per_seq(apr_sc[:, ls])
        pit = per_seq(api_sc[:, ls])

        def body(j, carry, ls=ls, steps=steps, prt=prt, pit=pit):
            cr, ci = carry
            r0 = pl.multiple_of(j * SUBLANES, SUBLANES)
            xr = hr_sc[:, pl.ds(r0, SUBLANES), ls].reshape(rows8, lane_chunk)
            xi = hi_sc[:, pl.ds(r0, SUBLANES), ls].reshape(rows8, lane_chunk)
            for s, ar, ai in steps:
                dr, di = _cmul(ar, ai, pltpu.roll(xr, s, 0), pltpu.roll(xi, s, 0))
                xr = xr + dr
                xi = xi + di
            crb = jnp.broadcast_to(cr, (bb, SUBLANES, lane_chunk)).reshape(rows8, lane_chunk)
            cib = jnp.broadcast_to(ci, (bb, SUBLANES, lane_chunk)).reshape(rows8, lane_chunk)
            dr, di = _cmul(prt, pit, crb, cib)
            xr3 = (xr + dr).reshape(bb, SUBLANES, lane_chunk)
            xi3 = (xi + di).reshape(bb, SUBLANES, lane_chunk)
            hr_sc[:, pl.ds(r0, SUBLANES), ls] = xr3
            hi_sc[:, pl.ds(r0, SUBLANES), ls] = xi3
            return xr3[:, SUBLANES - 1:SUBLANES, :], xi3[:, SUBLANES - 1:SUBLANES, :]

        cr, ci = lax.fori_loop(0, tc // SUBLANES, body, (cr_sc[:, :, ls], ci_sc[:, :, ls]))
        cr_sc[:, :, ls] = cr
        ci_sc[:, :, ls] = ci

    n_pack = cre_ref.shape[0]
    k_pack = n_state // n_pack
    parts = []
    for p4 in range(n_pack):
        ks = slice(p4 * k_pack, (p4 + 1) * k_pack)
        hrb = hr_sc[:, :, ks].reshape(m, k_pack).astype(BF16)
        hib = hi_sc[:, :, ks].reshape(m, k_pack).astype(BF16)
        parts.append(jnp.dot(hrb, cre_ref[p4], preferred_element_type=F32)
                     - jnp.dot(hib, cim_ref[p4], preferred_element_type=F32))
    y = jnp.concatenate(parts, axis=1) + d_ref[...] * u
    g = y * (0.5 * (1.0 + jnp.tanh(np.sqrt(2.0 / np.pi).astype(np.float32) * (y + 0.044715 * (y * y * y)))))
    z = jnp.dot(g.astype(BF16), wglu_ref[...], preferred_element_type=F32) + bglu_ref[...]
    out = g * jax.nn.sigmoid(z)
    if segmented:
        for c in range(n_col):
            col_sc[c] = out[:, c * LANES:(c + 1) * LANES]
        for c in range(n_col):
            for s in range(SUBLANES):
                ys_ref[0, s * seg_len:(s + 1) * seg_len, c * LANES:(c + 1) * LANES] = (
                    col_sc[c, pl.ds(s, seg_len, stride=SUBLANES), :].astype(ys_ref.dtype))
    else:
        ys_ref[...] = out.reshape(bb, tc, d_ssm).astype(ys_ref.dtype)

    @pl.when(ti == nt - 1)
    def _():
        sr_ref[...] = cr_sc[...]
        si_ref[...] = ci_sc[...]


def _s5(proj3, h0r, h0i, lw, bb, tc, col_block):
    bsz, t, _ = proj3.shape
    n_state = h0r.shape[-1]
    d_ssm = lw["s5_d"].shape[-1]
    n_pack = lw["s5_cre"].shape[0]
    full = lambda *shape: pl.BlockSpec(shape, lambda bi, ti: (0,) * len(shape))
    lane_chunk = max(LANES, 4 * LANES // bb)
    n_pow = max(SUBLANES, tc // SUBLANES)
    assert n_pow & (n_pow - 1) == 0
    kern = functools.partial(_s5_kernel, bb=bb, tc=tc, lane_chunk=lane_chunk)
    state_spec = pl.BlockSpec((bb, 1, n_state), lambda bi, ti: (bi, 0, 0))
    return pl.pallas_call(
        kern,
        grid=(bsz // bb, t // tc),
        in_specs=[pl.BlockSpec((bb, tc, d_ssm), lambda bi, ti: (bi, ti, col_block)),
                  state_spec, state_spec,
                  full(1, n_state), full(1, n_state), full(1, n_state),
                  full(2, d_ssm // 2, n_state // 2), full(2, d_ssm // 2, n_state // 2),
                  full(n_pack, n_state // n_pack, d_ssm // n_pack),
                  full(n_pack, n_state // n_pack, d_ssm // n_pack),
                  full(1, d_ssm), full(d_ssm, d_ssm), full(1, d_ssm)],
        out_specs=[pl.BlockSpec((bb, tc, d_ssm), lambda bi, ti: (bi, ti, 0)), state_spec, state_spec],
        out_shape=[jax.ShapeDtypeStruct((bsz, t, d_ssm), BF16),
                   jax.ShapeDtypeStruct((bsz, 1, n_state), F32),
                   jax.ShapeDtypeStruct((bsz, 1, n_state), F32)],
        scratch_shapes=[pltpu.VMEM((2, d_ssm // 2, n_state), BF16),
                        pltpu.VMEM((SUBLANES, n_state), F32), pltpu.VMEM((SUBLANES, n_state), F32),
                        pltpu.VMEM((len(SCAN_SHIFTS), SUBLANES, n_state), F32),
                        pltpu.VMEM((len(SCAN_SHIFTS), SUBLANES, n_state), F32),
                        pltpu.VMEM((n_pow, n_state), F32), pltpu.VMEM((n_pow, n_state), F32),
                        pltpu.VMEM((d_ssm // LANES, tc, LANES), F32),
                        pltpu.VMEM((bb, tc, n_state), F32), pltpu.VMEM((bb, tc, n_state), F32),
                        pltpu.VMEM((bb, 1, n_state), F32), pltpu.VMEM((bb, 1, n_state), F32)],
        compiler_params=_cparams(2),
        name="s5",
    )(proj3, h0r, h0i, lw["s5_lam_re"], lw["s5_lam_im"], lw["s5_log_dt"], lw["s5_wbr"], lw["s5_wbi"],
      lw["s5_cre"], lw["s5_cim"], lw["s5_d"], lw["s5_w_glu"], lw["s5_b_glu"])


def _rope_kernel(q_ref, k_ref, v_ref, cos_ref, sin_ref, *rest, gate, blocks_per_seq):
    if gate:
        qs_ref, kr_ref, vr_ref, kb_ref, vb_ref, sel_ref, km_sc = rest
    else:
        qs_ref, kr_ref, vr_ref = rest
    tm, d_attn = q_ref.shape
    reps = d_attn // LANES
    cos = jnp.concatenate([cos_ref[...]] * reps, axis=1)
    sin = jnp.concatenate([sin_ref[...]] * reps, axis=1)
    lane = lax.broadcasted_iota(jnp.int32, (tm, d_attn), 1)
    first_half = (lane & (HEAD_DIM - 1)) < HEAD_DIM // 2

    def rot(x):
        partner = jnp.where(first_half, pltpu.roll(x, d_attn - HEAD_DIM // 2, 1),
                            pltpu.roll(x, HEAD_DIM // 2, 1))
        return x * cos + partner * sin

    qr = rot(q_ref[...])
    kr = rot(k_ref[...])
    v = v_ref[...]
    if not gate:
        qs_ref[...] = qr
        kr_ref[...] = kr
        vr_ref[...] = v
        return
    krt = kr.T
    vt = v.T
    kr_ref[...] = krt
    vr_ref[...] = vt
    qs_ref[...] = (qr * (HEAD_DIM ** -0.5 * LOG2_E)).T.astype(BF16)
    kb_ref[...] = kr.astype(BF16)
    vb_ref[...] = vt.astype(BF16)

    tb = pl.program_id(0) % blocks_per_seq

    @pl.when(tb == 0)
    def _():
        km_sc[...] = jnp.zeros_like(km_sc)

    gates = lax.dot_general(km_sc[...], qr, _NT, precision=lax.Precision.HIGHEST,
                            preferred_element_type=F32)
    slot = lax.broadcasted_iota(jnp.int32, (GATE_SLOTS, tm), 0)
    slotf = slot.astype(F32)
    for h in range(ATTN_HEADS):
        hs = slice(h * GATE_SLOTS, (h + 1) * GATE_SLOTS)
        gh = jnp.where(slot < tb, gates[hs, :], -jnp.inf)
        sel = jnp.zeros((GATE_SLOTS, tm), F32)
        for _ in range(MOBA_TOPK):
            mx = jnp.max(gh, axis=0, keepdims=True)
            cand = (gh == mx) & (mx > -jnp.inf)
            first = jnp.min(jnp.where(cand, slotf, float(GATE_SLOTS)), axis=0, keepdims=True)
            pick = slotf == first
            sel = jnp.where(pick, 1.0, sel)
            gh = jnp.where(pick, -jnp.inf, gh)
        sel_ref[hs, :] = sel

    km = jnp.sum(kr, axis=0, keepdims=True) * (1.0 / MOBA_BLOCK)
    lane_row = lax.broadcasted_iota(jnp.int32, (1, d_attn), 1)
    for h in range(ATTN_HEADS):
        km_sc[pl.ds(h * GATE_SLOTS + tb, 1), :] = jnp.where(_div_pow2(lane_row, HEAD_DIM) == h, km, 0.0)


def _rope(proj2, cos_t, sin_t, tm, gate, blocks_per_seq, col0):
    n = proj2.shape[0]
    d_attn = ATTN_HEADS * HEAD_DIM
    n_tab = cos_t.shape[0] // tm
    qkv_spec = lambda c: pl.BlockSpec((tm, d_attn), lambda i, c=c: (i, c))
    tab_spec = pl.BlockSpec((tm, LANES), lambda i: (i % n_tab, 0))
    row_spec = pl.BlockSpec((tm, d_attn), lambda i: (i, 0))
    kern = functools.partial(_rope_kernel, gate=gate, blocks_per_seq=blocks_per_seq)
    if gate:
        n_col = ATTN_HEADS * GATE_SLOTS
        bsz = n // (tm * blocks_per_seq)
        t = tm * blocks_per_seq
        nb = blocks_per_seq
        col_spec = lambda rows: pl.BlockSpec((None, rows, tm), lambda i: (i // nb, 0, i % nb))
        out_specs = [col_spec(d_attn), col_spec(d_attn), col_spec(d_attn), row_spec, col_spec(d_attn),
                     col_spec(n_col)]
        out_shape = [jax.ShapeDtypeStruct((bsz, d_attn, t), BF16), jax.ShapeDtypeStruct((bsz, d_attn, t), F32),
                     jax.ShapeDtypeStruct((bsz, d_attn, t), F32), jax.ShapeDtypeStruct((n, d_attn), BF16),
                     jax.ShapeDtypeStruct((bsz, d_attn, t), BF16), jax.ShapeDtypeStruct((bsz, n_col, t), F32)]
        scratch = [pltpu.VMEM((n_col, d_attn), F32)]
    else:
        out_specs = [row_spec] * 3
        out_shape = [jax.ShapeDtypeStruct((n, d_attn), F32)] * 3
        scratch = []
    return pl.pallas_call(
        kern,
        grid=(n // tm,),
        in_specs=[qkv_spec(col0), qkv_spec(col0 + 1), qkv_spec(col0 + 2), tab_spec, tab_spec],
        out_specs=out_specs,
        out_shape=out_shape,
        scratch_shapes=scratch,
        compiler_params=_cparams(1),
        name="rope_gate" if gate else "rope",
    )(proj2, proj2, proj2, cos_t, sin_t)


def _attn_kernel(qt_ref, k_ref, vt_ref, sel_ref, o_ref, acc_sc):
    qi = pl.program_id(2)
    width, blk = qt_ref.shape
    nh = width // HEAD_DIM
    qt = qt_ref[...]
    row_head = _div_pow2(lax.broadcasted_iota(jnp.int32, qt.shape, 0), HEAD_DIM)
    zero = jnp.zeros_like(qt)
    q_heads = jnp.concatenate([jnp.where(row_head == e, qt, zero) for e in range(nh)], axis=1)
    keyi = lax.broadcasted_iota(jnp.int32, (blk, blk), 0)
    qcol = lax.broadcasted_iota(jnp.int32, (blk, blk), 1)
    causal = keyi <= qcol

    r0 = pl.multiple_of(qi * blk, blk)
    ko = k_ref[pl.ds(r0, blk), :]
    vo = vt_ref[:, pl.ds(r0, blk)]
    s_own = jnp.dot(ko, q_heads, preferred_element_type=F32)
    ones_own = jnp.ones((SUBLANES, blk), BF16)
    stats = []
    for e in range(nh):
        s = jnp.where(causal, s_own[:, e * blk:(e + 1) * blk], NEG_INF)
        mx = jnp.max(s, axis=0, keepdims=True)
        p = jnp.exp2(s - mx).astype(BF16)
        stats += [mx, jnp.dot(ones_own, p, preferred_element_type=F32)[0:1]]
        acc_sc[e] = jnp.dot(vo[e * HEAD_DIM:(e + 1) * HEAD_DIM, :], p, preferred_element_type=F32)

    ones_pair = jnp.ones((SUBLANES, 2 * blk), BF16)

    def body(j, carry):
        rn = pl.multiple_of(j * (2 * blk), 2 * blk)
        kn = k_ref[pl.ds(rn, 2 * blk), :]
        vn = vt_ref[:, pl.ds(rn, 2 * blk)]
        s_all = jnp.dot(kn, q_heads, preferred_element_type=F32)
        out = []
        for e in range(nh):
            mx, l = carry[2 * e], carry[2 * e + 1]
            mn = mx
            halves = []
            for c in range(2):
                keep = sel_ref[pl.ds(e * GATE_SLOTS + 2 * j + c, 1), :] > 0.5
                sc = s_all[c * blk:(c + 1) * blk, e * blk:(e + 1) * blk]
                mn = jnp.maximum(mn, jnp.where(keep, jnp.max(sc, axis=0, keepdims=True), NEG_INF))
                halves.append((sc, keep))
            alpha = jnp.exp2(mx - mn)
            p = jnp.concatenate([jnp.exp2(sc - jnp.where(keep, mn, -NEG_INF)).astype(BF16)
                                 for sc, keep in halves], axis=0)
            l = alpha * l + jnp.dot(ones_pair, p, preferred_element_type=F32)[0:1]
            acc_sc[e] = alpha * acc_sc[e] + jnp.dot(vn[e * HEAD_DIM:(e + 1) * HEAD_DIM, :], p,
                                                    preferred_element_type=F32)
            out += [mn, l]
        return tuple(out)

    res = lax.fori_loop(1, (qi + 1) // 2, body, body(0, tuple(stats)))
    ot = jnp.concatenate([acc_sc[e] / res[2 * e + 1] for e in range(nh)], axis=0)
    o_ref[...] = ot.T.astype(o_ref.dtype)


def _attn_prompt(qt, kb, vt, selt, bsz, t):
    d_attn = ATTN_HEADS * HEAD_DIM
    nq = t // MOBA_BLOCK
    kb3 = kb.reshape(bsz, t, d_attn)
    width = MXU_DIM
    nh = width // HEAD_DIM
    return pl.pallas_call(
        _attn_kernel,
        grid=(bsz, d_attn // width, nq),
        in_specs=[pl.BlockSpec((None, width, MOBA_BLOCK), lambda b, hg, qi: (b, hg, qi)),
                  pl.BlockSpec((None, t, width), lambda b, hg, qi: (b, 0, hg)),
                  pl.BlockSpec((None, width, t), lambda b, hg, qi: (b, hg, 0)),
                  pl.BlockSpec((None, nh * GATE_SLOTS, MOBA_BLOCK), lambda b, hg, qi: (b, hg, qi))],
        out_specs=pl.BlockSpec((MOBA_BLOCK, width), lambda b, hg, qi: (b * nq + qi, hg)),
        out_shape=jax.ShapeDtypeStruct((bsz * t, d_attn), BF16),
        scratch_shapes=[pltpu.VMEM((nh, HEAD_DIM, MOBA_BLOCK), F32)],
        compiler_params=_cparams(3),
        name="attn_prompt",
    )(qt, kb3, vt, selt)


def _sattn_kernel(pt_ref, q_ref, kn_ref, vn_ref, *rest, nblk, blocks_per_step):
    n_pages = 2 * blocks_per_step
    k_refs = rest[:n_pages]
    v_refs = rest[n_pages:2 * n_pages]
    o_ref, g_sc, m_sc, l_sc, o_sc = rest[2 * n_pages:]
    step = pl.program_id(1)
    tq, d_attn = q_ref.shape
    page = k_refs[0].shape[-1]
    rows = ATTN_HEADS * tq
    scale = HEAD_DIM ** -0.5

    def heads_to_rows(x):
        return jnp.concatenate([x[:, h * HEAD_DIM:(h + 1) * HEAD_DIM] for h in range(ATTN_HEADS)], axis=0)

    qr = q_ref[...]
    qt = qr.T
    qs = (heads_to_rows(qr) * scale).astype(BF16)
    lane = lax.broadcasted_iota(jnp.int32, (rows, LANES), 1)
    row_head = _div_pow2(lax.broadcasted_iota(jnp.int32, (rows, d_attn), 0), tq)
    lane_head = _div_pow2(lax.broadcasted_iota(jnp.int32, (rows, d_attn), 1), HEAD_DIM)
    qbd = jnp.where(row_head == lane_head, jnp.concatenate([qr] * ATTN_HEADS, axis=0), 0.0)
    qbs = (qbd * scale).astype(BF16)

    @pl.when(step == 0)
    def _():
        m_sc[...] = jnp.zeros_like(m_sc)
        l_sc[...] = jnp.zeros_like(l_sc)

    m_acc, l_acc = m_sc[...], l_sc[...]
    for c in range(blocks_per_step):
        n = step * blocks_per_step + c
        k0 = k_refs[2 * c][...].reshape(d_attn, page)
        k1 = k_refs[2 * c + 1][...].reshape(d_attn, page)
        v0 = v_refs[2 * c][...].reshape(d_attn, page).astype(BF16)
        v1 = v_refs[2 * c + 1][...].reshape(d_attn, page).astype(BF16)
        kmean = jnp.sum(k0 + k1, axis=1, keepdims=True) * (1.0 / MOBA_BLOCK)
        g_sc[n] = jnp.sum((kmean * qt).reshape(ATTN_HEADS, HEAD_DIM, tq), axis=1)
        s = jnp.concatenate([jnp.dot(qbs, k0.astype(BF16), preferred_element_type=F32),
                             jnp.dot(qbs, k1.astype(BF16), preferred_element_type=F32)], axis=1)
        mx = jnp.max(s, axis=1, keepdims=True)
        p = jnp.exp(s - mx)
        l = jnp.sum(p, axis=1, keepdims=True)
        pb = p.astype(BF16)
        o_full = (lax.dot_general(pb[:, :page], v0, _NT, preferred_element_type=F32)
                  + lax.dot_general(pb[:, page:], v1, _NT, preferred_element_type=F32))
        o_sc[n] = jnp.concatenate([o_full[h * tq:(h + 1) * tq, h * HEAD_DIM:(h + 1) * HEAD_DIM]
                                   for h in range(ATTN_HEADS)], axis=0)
        hit = lane == n
        m_acc = jnp.where(hit, mx, m_acc)
        l_acc = jnp.where(hit, l, l_acc)
    m_sc[...] = m_acc
    l_sc[...] = l_acc

    @pl.when(step == pl.num_programs(1) - 1)
    def _():
        remaining = [g_sc[nn] for nn in range(nblk)]
        chosen = [jnp.zeros((ATTN_HEADS, tq), F32) for _ in range(nblk)]
        for _ in range(MOBA_TOPK):
            gmx = remaining[0]
            for nn in range(1, nblk):
                gmx = jnp.maximum(gmx, remaining[nn])
            open_ = gmx > -jnp.inf
            for nn in range(nblk):
                pick = (remaining[nn] == gmx) & open_
                open_ = open_ & jnp.logical_not(pick)
                chosen[nn] = jnp.where(pick, 1.0, chosen[nn])
                remaining[nn] = jnp.where(pick, -jnp.inf, remaining[nn])
        own_q = lax.broadcasted_iota(jnp.int32, (rows, tq), 1) == (
            lax.broadcasted_iota(jnp.int32, (rows, tq), 0) & (tq - 1))
        picked = jnp.zeros((rows, LANES), F32)
        for nn in range(nblk):
            per_row = jnp.broadcast_to(chosen[nn][:, None, :], (ATTN_HEADS, tq, tq)).reshape(rows, tq)
            flag = jnp.sum(jnp.where(own_q, per_row, 0.0), axis=1, keepdims=True)
            picked = jnp.where(lane == nn, flag, picked)
        selw = picked > 0.5
        m_all = m_sc[...]
        l_all = l_sc[...]
        knew = heads_to_rows(kn_ref[...]).astype(BF16)
        vnew = heads_to_rows(vn_ref[...]).astype(BF16)
        s_own = lax.dot_general(qs, knew, _NT, preferred_element_type=F32)
        ri = lax.broadcasted_iota(jnp.int32, (rows, rows), 0)
        ci = lax.broadcasted_iota(jnp.int32, (rows, rows), 1)
        ok = (_div_pow2(ri, tq) == _div_pow2(ci, tq)) & ((ci & (tq - 1)) <= (ri & (tq - 1)))
        s_own = jnp.where(ok, s_own, NEG_INF)
        m_sel = jnp.max(jnp.where(selw, m_all, -jnp.inf), axis=1, keepdims=True)
        mf = jnp.maximum(jnp.max(s_own, axis=1, keepdims=True), m_sel)
        w = jnp.where(selw, jnp.exp(m_all - mf), 0.0)
        p_own = jnp.exp(s_own - mf)
        lf = jnp.sum(w * l_all, axis=1, keepdims=True) + jnp.sum(p_own, axis=1, keepdims=True)
        of = jnp.dot(p_own.astype(BF16), vnew, preferred_element_type=F32)
        for nn in range(nblk):
            of = of + w[:, nn:nn + 1] * o_sc[nn]
        out = of / lf
        o_ref[...] = jnp.concatenate([out[h * tq:(h + 1) * tq, :] for h in range(ATTN_HEADS)],
                                     axis=1).astype(o_ref.dtype)


def _attn_sample(qr3, kr3, vr3, cache_kt, cache_vt, page_table, layer):
    bsz, tq, d_attn = qr3.shape
    page = cache_kt.shape[4]
    pages_per_blk = MOBA_BLOCK // page
    assert pages_per_blk == 2 and cache_kt.shape[2:4] == (ATTN_HEADS, HEAD_DIM) and page == LANES
    nblk = page_table.shape[1] // pages_per_blk
    rows = ATTN_HEADS * tq
    blocks_per_step = max(c for c in (8, 4, 2, 1) if nblk % c == 0)
    pages_per_step = pages_per_blk * blocks_per_step
    tok_spec = pl.BlockSpec((None, tq, d_attn), lambda b, n, pt: (b, 0, 0))

    def page_spec(j):
        return pl.BlockSpec((None, None, ATTN_HEADS, HEAD_DIM, page),
                            lambda b, n, pt, j=j: (layer, pt[b, pages_per_step * n + j], 0, 0, 0))

    page_specs = [page_spec(j) for j in range(pages_per_step)]
    grid_spec = pltpu.PrefetchScalarGridSpec(
        num_scalar_prefetch=1,
        grid=(bsz, nblk // blocks_per_step),
        in_specs=[tok_spec, tok_spec, tok_spec] + page_specs + page_specs,
        out_specs=pl.BlockSpec((None, tq, d_attn), lambda b, n, pt: (b, 0, 0)),
        scratch_shapes=[pltpu.VMEM((nblk, ATTN_HEADS, tq), F32), pltpu.VMEM((rows, LANES), F32),
                        pltpu.VMEM((rows, LANES), F32), pltpu.VMEM((nblk, rows, HEAD_DIM), F32)],
    )
    return pl.pallas_call(
        functools.partial(_sattn_kernel, nblk=nblk, blocks_per_step=blocks_per_step),
        grid_spec=grid_spec,
        out_shape=jax.ShapeDtypeStruct((bsz, tq, d_attn), BF16),
        compiler_params=_cparams(2),
        name="attn_sample",
    )(page_table, qr3, kr3, vr3,
      *([cache_kt] * pages_per_step), *([cache_vt] * pages_per_step))


def _merge_kernel(x_ref, ya_ref, ys_ref, yc_ref, g0_ref, g1_ref, g2_ref,
                  wr_ref, ws_ref, wa_ref, wo_ref, lg_ref, lb_ref, o_ref, *, alpha):
    merged = jax.nn.sigmoid(g0_ref[...]) * jnp.dot(ya_ref[...], wr_ref[...], preferred_element_type=F32)
    merged = merged + jax.nn.sigmoid(g1_ref[...]) * jnp.dot(ys_ref[...], ws_ref[...],
                                                           preferred_element_type=F32)
    merged = merged + jax.nn.sigmoid(g2_ref[...]) * jnp.dot(yc_ref[...], wa_ref[...],
                                                           preferred_element_type=F32)
    z = alpha * x_ref[...] + jnp.dot(merged.astype(BF16), wo_ref[...], preferred_element_type=F32)
    o_ref[...] = _layer_norm(z, lg_ref[...], lb_ref[...])


def _merge(x, ya, ys, yc, proj2, lw, tm, gate_col0, alpha):
    n, d = x.shape
    row = lambda w: pl.BlockSpec((tm, w), lambda i: (i, 0))
    gspec = lambda c: pl.BlockSpec((tm, d), lambda i, c=c: (i, c))
    layer = lw["layer"]
    mats = [lw["w_br_rnn"], lw["w_br_ssm"], lw["w_br_attn"], lw["w_out"]]
    ws = mats + [lw["ln1_g"], lw["ln1_b"]]
    return pl.pallas_call(
        functools.partial(_merge_kernel, alpha=alpha),
        grid=(n // tm,),
        in_specs=[row(d), row(ya.shape[1]), row(ys.shape[1]), row(yc.shape[1]),
                  gspec(gate_col0), gspec(gate_col0 + 1), gspec(gate_col0 + 2)]
        + [_resident(a.shape[1:], layer) for a in mats] + [_resident((1, d)), _resident((1, d))],
        out_specs=row(d),
        out_shape=jax.ShapeDtypeStruct((n, d), F32),
        compiler_params=_cparams(1),
        name="merge_ln",
    )(x, ya, ys, yc, proj2, proj2, proj2, *ws)


def _ffn_kernel(x_ref, wi_ref, wo_ref, lg_ref, lb_ref, o_ref, h_sc, *, alpha, tf):
    d_ff = wo_ref.shape[0]
    x = x_ref[...]
    xb = x.astype(BF16)
    for j in range(d_ff // tf):
        hg = jnp.dot(xb, wi_ref[:, j * tf:(j + 1) * tf], preferred_element_type=F32)
        hu = jnp.dot(xb, wi_ref[:, d_ff + j * tf:d_ff + (j + 1) * tf], preferred_element_type=F32)
        h_sc[:, j * tf:(j + 1) * tf] = ((hg * jax.nn.sigmoid(hg)) * hu).astype(BF16)
    y = jnp.dot(h_sc[...], wo_ref[...], preferred_element_type=F32)
    o_ref[...] = _layer_norm(alpha * x + y, lg_ref[...], lb_ref[...])


def _ffn(x, lw, tm, alpha):
    n, d = x.shape
    d_ff = lw["w_ffn_out"].shape[1]
    layer = lw["layer"]
    tf = max(c for c in range(LANES, 6 * MXU_DIM + 1, LANES) if d_ff % c == 0)
    return pl.pallas_call(
        functools.partial(_ffn_kernel, alpha=alpha, tf=tf),
        grid=(n // tm,),
        in_specs=[pl.BlockSpec((tm, d), lambda i: (i, 0)),
                  _resident((d, 2 * d_ff), layer), _resident((d_ff, d), layer),
                  _resident((1, d)), _resident((1, d))],
        out_specs=pl.BlockSpec((tm, d), lambda i: (i, 0)),
        out_shape=jax.ShapeDtypeStruct((n, d), F32),
        scratch_shapes=[pltpu.VMEM((tm, d_ff), BF16)],
        compiler_params=_cparams(1),
        name="ffn_ln",
    )(x, lw["w_ffn_in"], lw["w_ffn_out"], lw["ln2_g"], lw["ln2_b"])


def _block_diag(w, per_block):
    n, r, c = w.shape
    eye = jnp.eye(per_block, dtype=w.dtype)
    out = jnp.einsum("kgrc,gh->kgrhc", w.reshape(n // per_block, per_block, r, c), eye)
    return out.reshape(n // per_block, per_block * r, per_block * c)


BIG_WEIGHTS = ("w_in", "w_br_rnn", "w_br_ssm", "w_br_attn", "w_out", "w_ffn_in", "w_ffn_out")


def _prep_layer(l, p, big):
    heads_per_tile = MXU_DIM // (p["rg_w_a"].shape[-1])
    row = lambda a: a[l].reshape(1, -1)
    groups = p["s5_b_re"].shape[1]
    lw = {
        "layer": l,
        "w_in": big["w_in"],
        "conv_w": p["conv_w"][l],
        "conv_b": row(p["conv_b"]),
        "rg_wbd": jnp.concatenate([_block_diag(p["rg_w_a"][l], heads_per_tile),
                                   _block_diag(p["rg_w_x"][l], heads_per_tile)], axis=2).astype(BF16),
        "rg_b_a": row(p["rg_b_a"]), "rg_b_x": row(p["rg_b_x"]), "rg_lambda": row(p["rg_lambda"]),
        "s5_lam_re": row(p["s5_lambda_re"]), "s5_lam_im": row(p["s5_lambda_im"]),
        "s5_log_dt": jnp.repeat(p["s5_log_step"][l], SSM_STATE).reshape(1, -1),
        "s5_wbr": _block_diag(jnp.swapaxes(p["s5_b_re"][l], 1, 2), groups // 2),
        "s5_wbi": _block_diag(jnp.swapaxes(p["s5_b_im"][l], 1, 2), groups // 2),
        "s5_cre": _block_diag(jnp.swapaxes(p["s5_c_re"][l], 1, 2), LANES // SSM_GROUP).astype(BF16),
        "s5_cim": _block_diag(jnp.swapaxes(p["s5_c_im"][l], 1, 2), LANES // SSM_GROUP).astype(BF16),
        "s5_d": row(p["s5_d"]),
        "s5_w_glu": p["s5_w_glu"][l].astype(BF16),
        "s5_b_glu": row(p["s5_b_glu"]),
        "w_br_rnn": big["w_br_rnn"],
        "w_br_ssm": big["w_br_ssm"],
        "w_br_attn": big["w_br_attn"],
        "w_out": big["w_out"],
        "ln1_g": row(p["ln1_g"]), "ln1_b": row(p["ln1_b"]),
        "w_ffn_in": big["w_ffn_in"],
        "w_ffn_out": big["w_ffn_out"],
        "ln2_g": row(p["ln2_g"]), "ln2_b": row(p["ln2_b"]),
    }
    return lw


def _rope_tables(pos0, t):
    half = HEAD_DIM // 2
    inv = jnp.power(ROPE_THETA, -jnp.arange(half, dtype=F32) * (2.0 / HEAD_DIM))
    ang = (pos0 + jnp.arange(t)).astype(F32)[:, None] * inv
    cos, sin = jnp.cos(ang), jnp.sin(ang)
    reps = LANES // HEAD_DIM
    cos_t = jnp.tile(jnp.concatenate([cos, cos], axis=1), (1, reps))
    sin_t = jnp.tile(jnp.concatenate([-sin, sin], axis=1), (1, reps))
    return cos_t, sin_t


def _largest_tile(n, cap):
    t = min(n, cap)
    while n % t:
        t //= 2
    return t


def _trunk_layer(x3, lw, alpha, cbuf8, h0, s5r0, s5i0, *, prompt, pos0=0, cache=None):
    bsz, t, d = x3.shape
    n = bsz * t
    d_rnn = h0.shape[-1]
    d_ssm = lw["s5_d"].shape[-1]
    d_attn = ATTN_HEADS * HEAD_DIM
    x2 = x3.reshape(n, d)
    proj2 = _proj(x2, lw["w_in"], lw["layer"], _largest_tile(n, 256), 1536)
    proj3 = proj2.reshape(bsz, t, -1)

    if prompt:
        bb, tc = 1, MOBA_BLOCK
    else:
        bb, tc = _largest_tile(bsz, 32), t
    assert tc % SUBLANES == 0 and tc & (tc - 1) == 0 and t % tc == 0
    ya, conv_new, h_new = _rglru(proj3, cbuf8, h0, lw, bb, tc)
    ys, s5r, s5i = _s5(proj3, s5r0, s5i0, lw, bb, tc, d_rnn // d_ssm)

    col0 = (d_rnn + d_ssm) // d_attn
    tm_rope = MOBA_BLOCK
    assert n % tm_rope == 0
    if prompt:
        assert t % MOBA_BLOCK == 0 and t // MOBA_BLOCK <= GATE_SLOTS
        cos_t, sin_t = _rope_tables(0, t)
        qt, krt, vrt, kb, vt, selt = _rope(proj2, cos_t, sin_t, tm_rope, True, t // MOBA_BLOCK, col0)
        yc = _attn_prompt(qt, kb, vt, selt, bsz, t)
        new_k = krt.reshape(bsz, ATTN_HEADS, HEAD_DIM, t).transpose(0, 3, 1, 2)
        new_v = vrt.reshape(bsz, ATTN_HEADS, HEAD_DIM, t).transpose(0, 3, 1, 2)
    else:
        assert tm_rope % t == 0
        cos_t, sin_t = _rope_tables(pos0, t)
        cos_t = jnp.tile(cos_t, (tm_rope // t, 1))
        sin_t = jnp.tile(sin_t, (tm_rope // t, 1))
        qr, kr, vr = _rope(proj2, cos_t, sin_t, tm_rope, False, 1, col0)
        cache_kt, cache_vt, page_table, layer = cache
        yc = _attn_sample(qr.reshape(bsz, t, d_attn), kr.reshape(bsz, t, d_attn), vr.reshape(bsz, t, d_attn),
                          cache_kt, cache_vt, page_table, layer).reshape(n, d_attn)
        new_k = kr.reshape(bsz, t, ATTN_HEADS, HEAD_DIM)
        new_v = vr.reshape(bsz, t, ATTN_HEADS, HEAD_DIM)

    gate_col0 = (d_rnn + d_ssm + 3 * d_attn) // d
    x1 = _merge(x2, ya.reshape(n, d_rnn), ys.reshape(n, d_ssm), yc, proj2, lw, _largest_tile(n, 512),
                gate_col0, alpha)
    x_out = _ffn(x1, lw, _largest_tile(n, 256), alpha)
    return (x_out.reshape(bsz, t, d), new_k, new_v, conv_new, h_new.reshape(bsz, d_rnn), s5r, s5i)


def kernel(x_prompt, x_sample, cache_k, cache_v, state_conv, state_rglru, state_s5_re, state_s5_im, page_table,
           w_in, conv_w, conv_b, rg_w_a, rg_b_a, rg_w_x, rg_b_x, rg_lambda,
           s5_lambda_re, s5_lambda_im, s5_b_re, s5_b_im, s5_c_re, s5_c_im, s5_d, s5_log_step, s5_w_glu, s5_b_glu,
           w_br_rnn, w_br_ssm, w_br_attn, w_out, ln1_g, ln1_b, w_ffn_in, w_ffn_out, ln2_g, ln2_b):
    params = dict(w_in=w_in, conv_w=conv_w, conv_b=conv_b, rg_w_a=rg_w_a, rg_b_a=rg_b_a, rg_w_x=rg_w_x,
                  rg_b_x=rg_b_x, rg_lambda=rg_lambda, s5_lambda_re=s5_lambda_re, s5_lambda_im=s5_lambda_im,
                  s5_b_re=s5_b_re, s5_b_im=s5_b_im, s5_c_re=s5_c_re, s5_c_im=s5_c_im, s5_d=s5_d,
                  s5_log_step=s5_log_step, s5_w_glu=s5_w_glu, s5_b_glu=s5_b_glu, w_br_rnn=w_br_rnn,
                  w_br_ssm=w_br_ssm, w_br_attn=w_br_attn, w_out=w_out, ln1_g=ln1_g, ln1_b=ln1_b,
                  w_ffn_in=w_ffn_in, w_ffn_out=w_ffn_out, ln2_g=ln2_g, ln2_b=ln2_b)
    depth = w_in.shape[0]
    alpha = (2.0 * depth) ** 0.25
    bp = x_prompt.shape[0]
    bs = x_sample.shape[0]
    d_rnn = state_rglru.shape[-1]
    groups, n_p = state_s5_re.shape[-2:]
    n_state = groups * n_p
    n_pages = page_table.shape[1]
    page = cache_k.shape[2]
    past_len = n_pages * page
    d_attn = ATTN_HEADS * HEAD_DIM
    assert past_len % MOBA_BLOCK == 0 and x_sample.shape[1] <= MOBA_BLOCK
    cache_kt = cache_k.transpose(0, 1, 3, 4, 2)
    cache_vt = cache_v.transpose(0, 1, 3, 4, 2)

    zeros_p = lambda *s: jnp.zeros((bp,) + s, F32)
    yp, ys = x_prompt, x_sample
    outs_p, outs_s = [], []
    big = {k: params[k].astype(BF16) for k in BIG_WEIGHTS}
    for l in range(depth):
        lw = _prep_layer(l, params, big)
        res = _trunk_layer(yp, lw, alpha, zeros_p(SUBLANES, d_rnn), zeros_p(1, d_rnn),
                           zeros_p(1, n_state), zeros_p(1, n_state), prompt=True)
        yp = res[0]
        outs_p.append(res[1:])
        cbuf8 = jnp.pad(state_conv[l], ((0, 0), (SUBLANES - (CONV_WIDTH - 1), 0), (0, 0)))
        res = _trunk_layer(ys, lw, alpha, cbuf8, state_rglru[l].reshape(bs, 1, d_rnn),
                           state_s5_re[l].reshape(bs, 1, n_state), state_s5_im[l].reshape(bs, 1, n_state),
                           prompt=False, pos0=past_len, cache=(cache_kt, cache_vt, page_table, l))
        ys = res[0]
        outs_s.append(res[1:])

    def stack(outs, i, shape=None):
        arrs = [o[i] if shape is None else o[i].reshape(shape) for o in outs]
        return jnp.stack(arrs)

    return (yp, ys,
            stack(outs_p, 0), stack(outs_p, 1), stack(outs_p, 2), stack(outs_p, 3),
            stack(outs_p, 4, (bp, groups, n_p)), stack(outs_p, 5, (bp, groups, n_p)),
            stack(outs_s, 0), stack(outs_s, 1), stack(outs_s, 2), stack(outs_s, 3),
            stack(outs_s, 4, (bs, groups, n_p)), stack(outs_s, 5, (bs, groups, n_p)))
```

```python
import functools

import jax
import jax.numpy as jnp
import numpy as np
from jax import lax
from jax.experimental import pallas as pl
from jax.experimental.pallas import tpu as pltpu

F32 = jnp.float32
BF16 = jnp.bfloat16

ATTN_HEADS = 8
HEAD_DIM = 64
RNN_HEADS = 16
CONV_WIDTH = 4
LRU_C = 8.0
SSM_GROUP = 16
SSM_STATE = 64
MOBA_BLOCK = 256
MOBA_TOPK = 3
ROPE_THETA = 10000.0
LN_EPS = 1e-5
NEG_INF = -1e30
LOG2_E = 1.4426950408889634

LANES = 128
SUBLANES = 8
MXU_DIM = 256
VMEM_LIMIT_BYTES = 56 * 1024 * 1024

GATE_SLOTS = 32
SCAN_SHIFTS = (1, 2, 4)
_NT = (((1,), (1,)), ((), ()))


def _cparams(n_axes):
    return pltpu.CompilerParams(dimension_semantics=("arbitrary",) * n_axes,
                                vmem_limit_bytes=VMEM_LIMIT_BYTES)


def _resident(shape, layer=None):
    if layer is None:
        return pl.BlockSpec(shape, lambda *_: (0,) * len(shape), pipeline_mode=pl.Buffered(1))
    return pl.BlockSpec((None,) + tuple(shape), lambda *_: (layer,) + (0,) * len(shape),
                        pipeline_mode=pl.Buffered(1))


def _div_pow2(x, d):
    assert d & (d - 1) == 0
    return lax.shift_right_logical(x, d.bit_length() - 1)


def _layer_norm(z, g, b):
    mu = jnp.mean(z, axis=-1, keepdims=True)
    zc = z - mu
    var = jnp.mean(zc * zc, axis=-1, keepdims=True)
    return zc * lax.rsqrt(var + LN_EPS) * g + b


def _rglru_kernel(x_ref, cbuf_ref, h0_ref, cw_ref, cb_ref, wbd_ref, ba_ref, bx_ref, lam_ref,
                  ya_ref, cnew_ref, hnew_ref, xbuf, hcar, *, bb, tc):
    ti = pl.program_id(1)
    nt = pl.num_programs(1)
    c_dim = x_ref.shape[-1]
    m = bb * tc

    @pl.when(ti == 0)
    def _():
        xbuf[:, 0:SUBLANES, :] = cbuf_ref[...]
        hcar[...] = h0_ref[...]

    x = x_ref[...]
    xbuf[:, SUBLANES:SUBLANES + tc, :] = x
    cw = cw_ref[...]
    xc = cb_ref[...] + cw[0:1] * xbuf[:, 5:5 + tc, :]
    xc = xc + cw[1:2] * xbuf[:, 6:6 + tc, :]
    xc = xc + cw[2:3] * xbuf[:, 7:7 + tc, :]
    xc = xc + cw[3:4] * x
    xc2 = xc.reshape(m, c_dim)
    xcb = xc2.astype(BF16)

    nl = -lam_ref[...]
    softplus = jnp.maximum(nl, 0.0) + jnp.log1p(jnp.exp(-jnp.abs(nl)))
    c_row = -LRU_C * softplus
    assert tc == SUBLANES or bb == 1
    row = lax.broadcasted_iota(jnp.int32, (m // SUBLANES, SUBLANES, MXU_DIM), 1)
    hc = hcar[...]
    for g in range(c_dim // MXU_DIM):
        sl = slice(g * MXU_DIM, (g + 1) * MXU_DIM)
        ga = jnp.dot(xcb[:, sl], wbd_ref[g], preferred_element_type=F32)
        r = jax.nn.sigmoid(ga[:, :MXU_DIM] + ba_ref[:, sl])
        i = jax.nn.sigmoid(ga[:, MXU_DIM:] + bx_ref[:, sl])
        log_a = c_row[:, sl] * r
        a = jnp.exp(log_a)
        mult = jnp.sqrt(jnp.maximum(-jnp.tanh(log_a) * (a * a + 1.0), 0.0))
        b = mult * (i * xc2[:, sl])
        a = a.reshape(m // SUBLANES, SUBLANES, MXU_DIM)
        b = b.reshape(m // SUBLANES, SUBLANES, MXU_DIM)
        s = 1
        while s < SUBLANES:
            msk = row >= s
            a_sh = jnp.where(msk, pltpu.roll(a, s, 1), 1.0)
            b_sh = jnp.where(msk, pltpu.roll(b, s, 1), 0.0)
            b = a * b_sh + b
            a = a * a_sh
            s *= 2
        a = a.reshape(m, MXU_DIM)
        b = b.reshape(m, MXU_DIM)
        if tc == SUBLANES:
            hcg = jnp.broadcast_to(hc[:, :, sl], (bb, tc, MXU_DIM)).reshape(m, MXU_DIM)
            h = b + a * hcg
        else:
            carry = hc[0, :, sl]
            groups = []
            for j in range(tc // SUBLANES):
                rs = slice(j * SUBLANES, (j + 1) * SUBLANES)
                hj = b[rs] + a[rs] * carry
                groups.append(hj)
                carry = hj[SUBLANES - 1:SUBLANES]
            h = jnp.concatenate(groups, axis=0)
        h3 = h.reshape(bb, tc, MXU_DIM)
        ya_ref[:, :, sl] = h3.astype(ya_ref.dtype)
        hcar[:, :, sl] = h3[:, tc - 1:tc, :]

    xbuf[:, 0:SUBLANES, :] = xbuf[:, tc:tc + SUBLANES, :]

    @pl.when(ti == nt - 1)
    def _():
        cnew_ref[...] = xbuf[:, 5:8, :]
        hnew_ref[...] = hcar[...]


def _rglru(proj3, cbuf8, h0, lw, bb, tc):
    bsz, t, _ = proj3.shape
    c_dim = h0.shape[-1]
    full = lambda *shape: pl.BlockSpec(shape, lambda bi, ti: (0,) * len(shape))
    kern = functools.partial(_rglru_kernel, bb=bb, tc=tc)
    return pl.pallas_call(
        kern,
        grid=(bsz // bb, t // tc),
        in_specs=[pl.BlockSpec((bb, tc, c_dim), lambda bi, ti: (bi, ti, 0)),
                  pl.BlockSpec((bb, SUBLANES, c_dim), lambda bi, ti: (bi, 0, 0)),
                  pl.BlockSpec((bb, 1, c_dim), lambda bi, ti: (bi, 0, 0)),
                  full(CONV_WIDTH, c_dim), full(1, c_dim),
                  full(c_dim // MXU_DIM, MXU_DIM, 2 * MXU_DIM),
                  full(1, c_dim), full(1, c_dim), full(1, c_dim)],
        out_specs=[pl.BlockSpec((bb, tc, c_dim), lambda bi, ti: (bi, ti, 0)),
                   pl.BlockSpec((bb, CONV_WIDTH - 1, c_dim), lambda bi, ti: (bi, 0, 0)),
                   pl.BlockSpec((bb, 1, c_dim), lambda bi, ti: (bi, 0, 0))],
        out_shape=[jax.ShapeDtypeStruct((bsz, t, c_dim), BF16),
                   jax.ShapeDtypeStruct((bsz, CONV_WIDTH - 1, c_dim), F32),
                   jax.ShapeDtypeStruct((bsz, 1, c_dim), F32)],
        scratch_shapes=[pltpu.VMEM((bb, tc + SUBLANES, c_dim), F32),
                        pltpu.VMEM((bb, 1, c_dim), F32)],
        compiler_params=_cparams(2),
        name="rglru",
    )(proj3, cbuf8, h0, lw["conv_w"], lw["conv_b"], lw["rg_wbd"], lw["rg_b_a"], lw["rg_b_x"],
      lw["rg_lambda"])


def _cmul(ar, ai, br, bi):
    return ar * br - ai * bi, ar * bi + ai * br


def _s5_segment_scan(hr_sc, hi_sc, pwr_sc, pwi_sc, cr_sc, ci_sc, seg_len, lane_chunk):
    n_state = hr_sc.shape[-1]
    for lc in range(n_state // lane_chunk):
        ls = slice(lc * lane_chunk, (lc + 1) * lane_chunk)
        a1r = jnp.broadcast_to(pwr_sc[0:1, ls], (SUBLANES, lane_chunk))
        a1i = jnp.broadcast_to(pwi_sc[0:1, ls], (SUBLANES, lane_chunk))

        def local_step(j, h, ls=ls, a1r=a1r, a1i=a1i):
            r0 = pl.multiple_of(j * SUBLANES, SUBLANES)
            dr, di = _cmul(a1r, a1i, h[0], h[1])
            xr = dr + hr_sc[0, pl.ds(r0, SUBLANES), ls]
            xi = di + hi_sc[0, pl.ds(r0, SUBLANES), ls]
            hr_sc[0, pl.ds(r0, SUBLANES), ls] = xr
            hi_sc[0, pl.ds(r0, SUBLANES), ls] = xi
            return xr, xi

        zero = jnp.zeros((SUBLANES, lane_chunk), F32)
        er, ei = lax.fori_loop(0, seg_len, local_step, (zero, zero))

        alr = pwr_sc[seg_len - 1:seg_len, ls]
        ali = pwi_sc[seg_len - 1:seg_len, ls]
        cr, ci = cr_sc[0, :, ls], ci_sc[0, :, ls]
        entry_r, entry_i = [], []
        for s in range(SUBLANES):
            entry_r.append(cr)
            entry_i.append(ci)
            dr, di = _cmul(alr, ali, cr, ci)
            cr, ci = er[s:s + 1] + dr, ei[s:s + 1] + di
        cr_sc[0, :, ls] = cr
        ci_sc[0, :, ls] = ci
        hin_r = jnp.concatenate(entry_r, axis=0)
        hin_i = jnp.concatenate(entry_i, axis=0)

        def fix_step(j, d, ls=ls, a1r=a1r, a1i=a1i):
            r0 = pl.multiple_of(j * SUBLANES, SUBLANES)
            hr_sc[0, pl.ds(r0, SUBLANES), ls] += d[0]
            hi_sc[0, pl.ds(r0, SUBLANES), ls] += d[1]
            return _cmul(a1r, a1i, d[0], d[1])

        lax.fori_loop(0, seg_len, fix_step, _cmul(a1r, a1i, hin_r, hin_i))


def _s5_kernel(u_ref, h0r_ref, h0i_ref, lamr_ref, lami_ref, ldt_ref, wbr_ref, wbi_ref,
               cre_ref, cim_ref, d_ref, wglu_ref, bglu_ref,
               ys_ref, sr_ref, si_ref,
               wb_sc, apr_sc, api_sc, lvr_sc, lvi_sc, pwr_sc, pwi_sc, col_sc, hr_sc, hi_sc, cr_sc, ci_sc,
               *, bb, tc, lane_chunk):
    bi = pl.program_id(0)
    ti = pl.program_id(1)
    nt = pl.num_programs(1)
    m = bb * tc
    n_state = hr_sc.shape[-1]
    d_ssm = u_ref.shape[-1]
    half = n_state // 2
    segmented = tc > SUBLANES
    seg_len = tc // SUBLANES
    assert not segmented or bb == 1

    @pl.when((bi == 0) & (ti == 0))
    def _():
        dt = jnp.exp(ldt_ref[...])
        lr = lamr_ref[...]
        li = lami_ref[...]
        mag = jnp.exp(lr * dt)
        abr = mag * jnp.cos(li * dt)
        abi = mag * jnp.sin(li * dt)
        nr = abr - 1.0
        den = lr * lr + li * li
        zr = (nr * lr + abi * li) / den
        zi = (abi * lr - nr * li) / den
        for kb in range(2):
            ks = slice(kb * half, (kb + 1) * half)
            br = wbr_ref[kb]
            bim = wbi_ref[kb]
            wb_sc[kb, :, 0:half] = (zr[:, ks] * br - zi[:, ks] * bim).astype(BF16)
            wb_sc[kb, :, half:n_state] = (zr[:, ks] * bim + zi[:, ks] * br).astype(BF16)
        row8 = lax.broadcasted_iota(jnp.int32, (SUBLANES, n_state), 0)
        pr = jnp.broadcast_to(abr, (SUBLANES, n_state))
        pi = jnp.broadcast_to(abi, (SUBLANES, n_state))
        for s in SCAN_SHIFTS:
            msk = row8 >= s
            qr = jnp.where(msk, pltpu.roll(pr, s, 0), 1.0)
            qi = jnp.where(msk, pltpu.roll(pi, s, 0), 0.0)
            pr, pi = _cmul(pr, pi, qr, qi)
        apr_sc[...] = pr
        api_sc[...] = pi
        for idx, s in enumerate(SCAN_SHIFTS):
            inside = row8 >= s
            lvr_sc[idx] = jnp.where(inside, jnp.broadcast_to(pr[s - 1:s], (SUBLANES, n_state)), 0.0)
            lvi_sc[idx] = jnp.where(inside, jnp.broadcast_to(pi[s - 1:s], (SUBLANES, n_state)), 0.0)
        if segmented:
            pwr_sc[0:SUBLANES] = pr
            pwi_sc[0:SUBLANES] = pi
            have = SUBLANES
            while have < seg_len:
                tr, ti_ = _cmul(pwr_sc[have - 1:have], pwi_sc[have - 1:have], pwr_sc[0:have], pwi_sc[0:have])
                pwr_sc[have:2 * have] = tr
                pwi_sc[have:2 * have] = ti_
                have *= 2

    @pl.when(ti == 0)
    def _():
        cr_sc[...] = h0r_ref[...]
        ci_sc[...] = h0i_ref[...]

    n_col = d_ssm // LANES
    if segmented:
        for c in range(n_col):
            col_sc[c] = u_ref[0, :, c * LANES:(c + 1) * LANES]
        u = jnp.concatenate(
            [jnp.concatenate([col_sc[c, pl.ds(j, SUBLANES, stride=seg_len), :] for j in range(seg_len)], axis=0)
             for c in range(n_col)], axis=1)
    else:
        u = u_ref[...].reshape(m, d_ssm)
    ub = u.astype(BF16)
    k_half = d_ssm // 2
    for kb in range(2):
        bu = jnp.dot(ub[:, kb * k_half:(kb + 1) * k_half], wb_sc[kb], preferred_element_type=F32)
        hr_sc[:, :, kb * half:(kb + 1) * half] = bu[:, :half].reshape(bb, tc, half)
        hi_sc[:, :, kb * half:(kb + 1) * half] = bu[:, half:].reshape(bb, tc, half)

    if segmented:
        _s5_segment_scan(hr_sc, hi_sc, pwr_sc, pwi_sc, cr_sc, ci_sc, seg_len, lane_chunk)
    rows8 = bb * SUBLANES
    for lc in range(0 if segmented else n_state // lane_chunk):
        ls = slice(lc * lane_chunk, (lc + 1) * lane_chunk)

        def per_seq(tab):
            return jnp.broadcast_to(tab[None], (bb, SUBLANES, lane_chunk)).reshape(rows8, lane_chunk)

        steps = tuple((s, per_seq(lvr_sc[idx, :, ls]), per_seq(lvi_sc[idx, :, ls]))
                      for idx, s in enumerate(SCAN_SHIFTS))
        prt = per_seq(apr_sc[:, ls])
        pit = per_seq(api_sc[:, ls])

        def body(j, carry, ls=ls, steps=steps, prt=prt, pit=pit):
            cr, ci = carry
            r0 = pl.multiple_of(j * SUBLANES, SUBLANES)
            xr = hr_sc[:, pl.ds(r0, SUBLANES), ls].reshape(rows8, lane_chunk)
            xi = hi_sc[:, pl.ds(r0, SUBLANES), ls].reshape(rows8, lane_chunk)
            for s, ar, ai in steps:
                dr, di = _cmul(ar, ai, pltpu.roll(xr, s, 0), pltpu.roll(xi, s, 0))
                xr = xr + dr
                xi = xi + di
            crb = jnp.broadcast_to(cr, (bb, SUBLANES, lane_chunk)).reshape(rows8, lane_chunk)
            cib = jnp.broadcast_to(ci, (bb, SUBLANES, lane_chunk)).reshape(rows8, lane_chunk)
            dr, di = _cmul(prt, pit, crb, cib)
            xr3 = (xr + dr).reshape(bb, SUBLANES, lane_chunk)
            xi3 = (xi + di).reshape(bb, SUBLANES, lane_chunk)
            hr_sc[:, pl.ds(r0, SUBLANES), ls] = xr3
            hi_sc[:, pl.ds(r0, SUBLANES), ls] = xi3
            return xr3[:, SUBLANES - 1:SUBLANES, :], xi3[:, SUBLANES - 1:SUBLANES, :]

        cr, ci = lax.fori_loop(0, tc // SUBLANES, body, (cr_sc[:, :, ls], ci_sc[:, :, ls]))
        cr_sc[:, :, ls] = cr
        ci_sc[:, :, ls] = ci

    n_pack = cre_ref.shape[0]
    k_pack = n_state // n_pack
    parts = []
    for p4 in range(n_pack):
        ks = slice(p4 * k_pack, (p4 + 1) * k_pack)
        hrb = hr_sc[:, :, ks].reshape(m, k_pack).astype(BF16)
        hib = hi_sc[:, :, ks].reshape(m, k_pack).astype(BF16)
        parts.append(jnp.dot(hrb, cre_ref[p4], preferred_element_type=F32)
                     - jnp.dot(hib, cim_ref[p4], preferred_element_type=F32))
    y = jnp.concatenate(parts, axis=1) + d_ref[...] * u
    g = y * (0.5 * (1.0 + jnp.tanh(np.sqrt(2.0 / np.pi).astype(np.float32) * (y + 0.044715 * (y * y * y)))))
    z = jnp.dot(g.astype(BF16), wglu_ref[...], preferred_element_type=F32) + bglu_ref[...]
    out = g * jax.nn.sigmoid(z)
    if segmented:
        for c in range(n_col):
            col_sc[c] = out[:, c * LANES:(c + 1) * LANES]
        for c in range(n_col):
            for s in range(SUBLANES):
                ys_ref[0, s * seg_len:(s + 1) * seg_len, c * LANES:(c + 1) * LANES] = (
                    col_sc[c, pl.ds(s, seg_len, stride=SUBLANES), :].astype(ys_ref.dtype))
    else:
        ys_ref[...] = out.reshape(bb, tc, d_ssm).astype(ys_ref.dtype)

    @pl.when(ti == nt - 1)
    def _():
        sr_ref[...] = cr_sc[...]
        si_ref[...] = ci_sc[...]


def _s5(proj3, h0r, h0i, lw, bb, tc, col_block):
    bsz, t, _ = proj3.shape
    n_state = h0r.shape[-1]
    d_ssm = lw["s5_d"].shape[-1]
    n_pack = lw["s5_cre"].shape[0]
    full = lambda *shape: pl.BlockSpec(shape, lambda bi, ti: (0,) * len(shape))
    lane_chunk = max(LANES, 4 * LANES // bb)
    n_pow = max(SUBLANES, tc // SUBLANES)
    assert n_pow & (n_pow - 1) == 0
    kern = functools.partial(_s5_kernel, bb=bb, tc=tc, lane_chunk=lane_chunk)
    state_spec = pl.BlockSpec((bb, 1, n_state), lambda bi, ti: (bi, 0, 0))
    return pl.pallas_call(
        kern,
        grid=(bsz // bb, t // tc),
        in_specs=[pl.BlockSpec((bb, tc, d_ssm), lambda bi, ti: (bi, ti, col_block)),
                  state_spec, state_spec,
                  full(1, n_state), full(1, n_state), full(1, n_state),
                  full(2, d_ssm // 2, n_state // 2), full(2, d_ssm // 2, n_state // 2),
                  full(n_pack, n_state // n_pack, d_ssm // n_pack),
                  full(n_pack, n_state // n_pack, d_ssm // n_pack),
                  full(1, d_ssm), full(d_ssm, d_ssm), full(1, d_ssm)],
        out_specs=[pl.BlockSpec((bb, tc, d_ssm), lambda bi, ti: (bi, ti, 0)), state_spec, state_spec],
        out_shape=[jax.ShapeDtypeStruct((bsz, t, d_ssm), BF16),
                   jax.ShapeDtypeStruct((bsz, 1, n_state), F32),
                   jax.ShapeDtypeStruct((bsz, 1, n_state), F32)],
        scratch_shapes=[pltpu.VMEM((2, d_ssm // 2, n_state), BF16),
                        pltpu.VMEM((SUBLANES, n_state), F32), pltpu.VMEM((SUBLANES, n_state), F32),
                        pltpu.VMEM((len(SCAN_SHIFTS), SUBLANES, n_state), F32),
                        pltpu.VMEM((len(SCAN_SHIFTS), SUBLANES, n_state), F32),
                        pltpu.VMEM((n_pow, n_state), F32), pltpu.VMEM((n_pow, n_state), F32),
                        pltpu.VMEM((d_ssm // LANES, tc, LANES), F32),
                        pltpu.VMEM((bb, tc, n_state), F32), pltpu.VMEM((bb, tc, n_state), F32),
                        pltpu.VMEM((bb, 1, n_state), F32), pltpu.VMEM((bb, 1, n_state), F32)],
        compiler_params=_cparams(2),
        name="s5",
    )(proj3, h0r, h0i, lw["s5_lam_re"], lw["s5_lam_im"], lw["s5_log_dt"], lw["s5_wbr"], lw["s5_wbi"],
      lw["s5_cre"], lw["s5_cim"], lw["s5_d"], lw["s5_w_glu"], lw["s5_b_glu"])


def _proj_rope_kernel(x_ref, w_ref, cos_ref, sin_ref, o_ref, *rest, tn, gate, blocks_per_seq):
    xb = x_ref[...].astype(BF16)
    n_plain = o_ref.shape[1]
    d_attn = ATTN_HEADS * HEAD_DIM
    qkv = jnp.dot(xb, w_ref[:, n_plain:n_plain + 3 * d_attn], preferred_element_type=F32)
    _rope_stage(qkv[:, :d_attn], qkv[:, d_attn:2 * d_attn], qkv[:, 2 * d_attn:], cos_ref, sin_ref, rest,
                gate=gate, blocks_per_seq=blocks_per_seq)
    for j in range(n_plain // tn):
        cs = slice(j * tn, (j + 1) * tn)
        o_ref[:, cs] = jnp.dot(xb, w_ref[:, cs], preferred_element_type=F32)


def _rope_stage(q, k, v, cos_ref, sin_ref, rest, *, gate, blocks_per_seq):
    if gate:
        qs_ref, kr_ref, vr_ref, kb_ref, vb_ref, sel_ref, km_sc = rest
    else:
        qs_ref, kr_ref, vr_ref = rest
    tm, d_attn = q.shape
    reps = d_attn // LANES
    cos = jnp.concatenate([cos_ref[...]] * reps, axis=1)
    sin = jnp.concatenate([sin_ref[...]] * reps, axis=1)
    lane = lax.broadcasted_iota(jnp.int32, (tm, d_attn), 1)
    first_half = (lane & (HEAD_DIM - 1)) < HEAD_DIM // 2

    def rot(x):
        partner = jnp.where(first_half, pltpu.roll(x, d_attn - HEAD_DIM // 2, 1),
                            pltpu.roll(x, HEAD_DIM // 2, 1))
        return x * cos + partner * sin

    qr = rot(q)
    kr = rot(k)
    if not gate:
        qs_ref[...] = qr
        kr_ref[...] = kr
        vr_ref[...] = v
        return
    krt = kr.T
    vt = v.T
    kr_ref[...] = krt
    vr_ref[...] = vt
    qs_ref[...] = (qr * (HEAD_DIM ** -0.5 * LOG2_E)).T.astype(BF16)
    kb_ref[...] = kr.astype(BF16)
    vb_ref[...] = vt.astype(BF16)

    tb = pl.program_id(0) % blocks_per_seq

    @pl.when(tb == 0)
    def _():
        km_sc[...] = jnp.zeros_like(km_sc)

    gates = lax.dot_general(km_sc[...], qr, _NT, precision=lax.Precision.HIGHEST,
                            preferred_element_type=F32)
    slot = lax.broadcasted_iota(jnp.int32, (GATE_SLOTS, tm), 0)
    slotf = slot.astype(F32)
    for h in range(ATTN_HEADS):
        hs = slice(h * GATE_SLOTS, (h + 1) * GATE_SLOTS)
        gh = jnp.where(slot < tb, gates[hs, :], -jnp.inf)
        sel = jnp.zeros((GATE_SLOTS, tm), F32)
        for _ in range(MOBA_TOPK):
            mx = jnp.max(gh, axis=0, keepdims=True)
            cand = (gh == mx) & (mx > -jnp.inf)
            first = jnp.min(jnp.where(cand, slotf, float(GATE_SLOTS)), axis=0, keepdims=True)
            pick = slotf == first
            sel = jnp.where(pick, 1.0, sel)
            gh = jnp.where(pick, -jnp.inf, gh)
        sel_ref[hs, :] = sel

    km = jnp.sum(kr, axis=0, keepdims=True) * (1.0 / MOBA_BLOCK)
    lane_row = lax.broadcasted_iota(jnp.int32, (1, d_attn), 1)
    for h in range(ATTN_HEADS):
        km_sc[pl.ds(h * GATE_SLOTS + tb, 1), :] = jnp.where(_div_pow2(lane_row, HEAD_DIM) == h, km, 0.0)


def _proj_rope(x, w, layer, cos_t, sin_t, tm, tn, gate, blocks_per_seq):
    n, d = x.shape
    d_attn = ATTN_HEADS * HEAD_DIM
    n_plain = w.shape[-1] - 3 * d_attn
    assert n_plain % tn == 0
    n_tab = cos_t.shape[0] // tm
    tab_spec = pl.BlockSpec((tm, LANES), lambda i: (i % n_tab, 0))
    row_spec = pl.BlockSpec((tm, d_attn), lambda i: (i, 0))
    kern = functools.partial(_proj_rope_kernel, tn=tn, gate=gate, blocks_per_seq=blocks_per_seq)
    if gate:
        n_col = ATTN_HEADS * GATE_SLOTS
        bsz = n // (tm * blocks_per_seq)
        t = tm * blocks_per_seq
        nb = blocks_per_seq
        col_spec = lambda rows: pl.BlockSpec((None, rows, tm), lambda i: (i // nb, 0, i % nb))
        out_specs = [col_spec(d_attn), col_spec(d_attn), col_spec(d_attn), row_spec, col_spec(d_attn),
                     col_spec(n_col)]
        out_shape = [jax.ShapeDtypeStruct((bsz, d_attn, t), BF16), jax.ShapeDtypeStruct((bsz, d_attn, t), F32),
                     jax.ShapeDtypeStruct((bsz, d_attn, t), F32), jax.ShapeDtypeStruct((n, d_attn), BF16),
                     jax.ShapeDtypeStruct((bsz, d_attn, t), BF16), jax.ShapeDtypeStruct((bsz, n_col, t), F32)]
        scratch = [pltpu.VMEM((n_col, d_attn), F32)]
    else:
        out_specs = [row_spec] * 3
        out_shape = [jax.ShapeDtypeStruct((n, d_attn), F32)] * 3
        scratch = []
    return pl.pallas_call(
        kern,
        grid=(n // tm,),
        in_specs=[pl.BlockSpec((tm, d), lambda i: (i, 0)), _resident((d, w.shape[-1]), layer), tab_spec, tab_spec],
        out_specs=[pl.BlockSpec((tm, n_plain), lambda i: (i, 0))] + out_specs,
        out_shape=[jax.ShapeDtypeStruct((n, n_plain), F32)] + out_shape,
        scratch_shapes=scratch,
        compiler_params=_cparams(1),
        name="proj_rope_gate" if gate else "proj_rope",
    )(x, w, cos_t, sin_t)


def _attn_kernel(qt_ref, k_ref, vt_ref, sel_ref, o_ref, acc_sc):
    qi = pl.program_id(2)
    width, blk = qt_ref.shape
    nh = width // HEAD_DIM
    qt = qt_ref[...]
    row_head = _div_pow2(lax.broadcasted_iota(jnp.int32, qt.shape, 0), HEAD_DIM)
    zero = jnp.zeros_like(qt)
    q_heads = jnp.concatenate([jnp.where(row_head == e, qt, zero) for e in range(nh)], axis=1)
    keyi = lax.broadcasted_iota(jnp.int32, (blk, blk), 0)
    qcol = lax.broadcasted_iota(jnp.int32, (blk, blk), 1)
    causal = keyi <= qcol

    r0 = pl.multiple_of(qi * blk, blk)
    ko = k_ref[pl.ds(r0, blk), :]
    vo = vt_ref[:, pl.ds(r0, blk)]
    s_own = jnp.dot(ko, q_heads, preferred_element_type=F32)
    ones_own = jnp.ones((SUBLANES, blk), BF16)
    stats = []
    for e in range(nh):
        s = jnp.where(causal, s_own[:, e * blk:(e + 1) * blk], NEG_INF)
        mx = jnp.max(s, axis=0, keepdims=True)
        p = jnp.exp2(s - mx).astype(BF16)
        stats += [mx, jnp.dot(ones_own, p, preferred_element_type=F32)[0:1]]
        acc_sc[e] = jnp.dot(vo[e * HEAD_DIM:(e + 1) * HEAD_DIM, :], p, preferred_element_type=F32)

    ones_pair = jnp.ones((SUBLANES, 2 * blk), BF16)

    def body(j, carry):
        rn = pl.multiple_of(j * (2 * blk), 2 * blk)
        kn = k_ref[pl.ds(rn, 2 * blk), :]
        vn = vt_ref[:, pl.ds(rn, 2 * blk)]
        s_all = jnp.dot(kn, q_heads, preferred_element_type=F32)
        out = []
        for e in range(nh):
            mx, l = carry[2 * e], carry[2 * e + 1]
            mn = mx
            halves = []
            for c in range(2):
                keep = sel_ref[pl.ds(e * GATE_SLOTS + 2 * j + c, 1), :] > 0.5
                sc = s_all[c * blk:(c + 1) * blk, e * blk:(e + 1) * blk]
                mn = jnp.maximum(mn, jnp.where(keep, jnp.max(sc, axis=0, keepdims=True), NEG_INF))
                halves.append((sc, keep))
            alpha = jnp.exp2(mx - mn)
            p = jnp.concatenate([jnp.exp2(sc - jnp.where(keep, mn, -NEG_INF)).astype(BF16)
                                 for sc, keep in halves], axis=0)
            l = alpha * l + jnp.dot(ones_pair, p, preferred_element_type=F32)[0:1]
            acc_sc[e] = alpha * acc_sc[e] + jnp.dot(vn[e * HEAD_DIM:(e + 1) * HEAD_DIM, :], p,
                                                    preferred_element_type=F32)
            out += [mn, l]
        return tuple(out)

    res = lax.fori_loop(1, (qi + 1) // 2, body, body(0, tuple(stats)))
    ot = jnp.concatenate([acc_sc[e] / res[2 * e + 1] for e in range(nh)], axis=0)
    o_ref[...] = ot.T.astype(o_ref.dtype)


def _attn_prompt(qt, kb, vt, selt, bsz, t):
    d_attn = ATTN_HEADS * HEAD_DIM
    nq = t // MOBA_BLOCK
    kb3 = kb.reshape(bsz, t, d_attn)
    width = MXU_DIM
    nh = width // HEAD_DIM
    return pl.pallas_call(
        _attn_kernel,
        grid=(bsz, d_attn // width, nq),
        in_specs=[pl.BlockSpec((None, width, MOBA_BLOCK), lambda b, hg, qi: (b, hg, qi)),
                  pl.BlockSpec((None, t, width), lambda b, hg, qi: (b, 0, hg)),
                  pl.BlockSpec((None, width, t), lambda b, hg, qi: (b, hg, 0)),
                  pl.BlockSpec((None, nh * GATE_SLOTS, MOBA_BLOCK), lambda b, hg, qi: (b, hg, qi))],
        out_specs=pl.BlockSpec((MOBA_BLOCK, width), lambda b, hg, qi: (b * nq + qi, hg)),
        out_shape=jax.ShapeDtypeStruct((bsz * t, d_attn), BF16),
        scratch_shapes=[pltpu.VMEM((nh, HEAD_DIM, MOBA_BLOCK), F32)],
        compiler_params=_cparams(3),
        name="attn_prompt",
    )(qt, kb3, vt, selt)


def _sattn_kernel(pt_ref, q_ref, kn_ref, vn_ref, *rest, nblk, blocks_per_step):
    n_pages = 2 * blocks_per_step
    k_refs = rest[:n_pages]
    v_refs = rest[n_pages:2 * n_pages]
    o_ref, g_sc, m_sc, l_sc, o_sc = rest[2 * n_pages:]
    step = pl.program_id(1)
    tq, d_attn = q_ref.shape
    page = k_refs[0].shape[-1]
    rows = ATTN_HEADS * tq
    scale = HEAD_DIM ** -0.5

    def heads_to_rows(x):
        return jnp.concatenate([x[:, h * HEAD_DIM:(h + 1) * HEAD_DIM] for h in range(ATTN_HEADS)], axis=0)

    qr = q_ref[...]
    qt = qr.T
    qs = (heads_to_rows(qr) * scale).astype(BF16)
    lane = lax.broadcasted_iota(jnp.int32, (rows, LANES), 1)
    row_head = _div_pow2(lax.broadcasted_iota(jnp.int32, (rows, d_attn), 0), tq)
    lane_head = _div_pow2(lax.broadcasted_iota(jnp.int32, (rows, d_attn), 1), HEAD_DIM)
    qbd = jnp.where(row_head == lane_head, jnp.concatenate([qr] * ATTN_HEADS, axis=0), 0.0)
    qbs = (qbd * scale).astype(BF16)

    @pl.when(step == 0)
    def _():
        m_sc[...] = jnp.zeros_like(m_sc)
        l_sc[...] = jnp.zeros_like(l_sc)

    m_acc, l_acc = m_sc[...], l_sc[...]
    for c in range(blocks_per_step):
        n = step * blocks_per_step + c
        k0 = k_refs[2 * c][...].reshape(d_attn, page)
        k1 = k_refs[2 * c + 1][...].reshape(d_attn, page)
        v0 = v_refs[2 * c][...].reshape(d_attn, page).astype(BF16)
        v1 = v_refs[2 * c + 1][...].reshape(d_attn, page).astype(BF16)
        kmean = jnp.sum(k0 + k1, axis=1, keepdims=True) * (1.0 / MOBA_BLOCK)
        g_sc[n] = jnp.sum((kmean * qt).reshape(ATTN_HEADS, HEAD_DIM, tq), axis=1)
        s = jnp.concatenate([jnp.dot(qbs, k0.astype(BF16), preferred_element_type=F32),
                             jnp.dot(qbs, k1.astype(BF16), preferred_element_type=F32)], axis=1)
        mx = jnp.max(s, axis=1, keepdims=True)
        p = jnp.exp(s - mx)
        l = jnp.sum(p, axis=1, keepdims=True)
        pb = p.astype(BF16)
        o_full = (lax.dot_general(pb[:, :page], v0, _NT, preferred_element_type=F32)
                  + lax.dot_general(pb[:, page:], v1, _NT, preferred_element_type=F32))
        o_sc[n] = jnp.concatenate([o_full[h * tq:(h + 1) * tq, h * HEAD_DIM:(h + 1) * HEAD_DIM]
                                   for h in range(ATTN_HEADS)], axis=0)
        hit = lane == n
        m_acc = jnp.where(hit, mx, m_acc)
        l_acc = jnp.where(hit, l, l_acc)
    m_sc[...] = m_acc
    l_sc[...] = l_acc

    @pl.when(step == pl.num_programs(1) - 1)
    def _():
        remaining = [g_sc[nn] for nn in range(nblk)]
        chosen = [jnp.zeros((ATTN_HEADS, tq), F32) for _ in range(nblk)]
        for _ in range(MOBA_TOPK):
            gmx = remaining[0]
            for nn in range(1, nblk):
                gmx = jnp.maximum(gmx, remaining[nn])
            open_ = gmx > -jnp.inf
            for nn in range(nblk):
                pick = (remaining[nn] == gmx) & open_
                open_ = open_ & jnp.logical_not(pick)
                chosen[nn] = jnp.where(pick, 1.0, chosen[nn])
                remaining[nn] = jnp.where(pick, -jnp.inf, remaining[nn])
        own_q = lax.broadcasted_iota(jnp.int32, (rows, tq), 1) == (
            lax.broadcasted_iota(jnp.int32, (rows, tq), 0) & (tq - 1))
        picked = jnp.zeros((rows, LANES), F32)
        for nn in range(nblk):
            per_row = jnp.broadcast_to(chosen[nn][:, None, :], (ATTN_HEADS, tq, tq)).reshape(rows, tq)
            flag = jnp.sum(jnp.where(own_q, per_row, 0.0), axis=1, keepdims=True)
            picked = jnp.where(lane == nn, flag, picked)
        selw = picked > 0.5
        m_all = m_sc[...]
        l_all = l_sc[...]
        knew = heads_to_rows(kn_ref[...]).astype(BF16)
        vnew = heads_to_rows(vn_ref[...]).astype(BF16)
        s_own = lax.dot_general(qs, knew, _NT, preferred_element_type=F32)
        ri = lax.broadcasted_iota(jnp.int32, (rows, rows), 0)
        ci = lax.broadcasted_iota(jnp.int32, (rows, rows), 1)
        ok = (_div_pow2(ri, tq) == _div_pow2(ci, tq)) & ((ci & (tq - 1)) <= (ri & (tq - 1)))
        s_own = jnp.where(ok, s_own, NEG_INF)
        m_sel = jnp.max(jnp.where(selw, m_all, -jnp.inf), axis=1, keepdims=True)
        mf = jnp.maximum(jnp.max(s_own, axis=1, keepdims=True), m_sel)
        w = jnp.where(selw, jnp.exp(m_all - mf), 0.0)
        p_own = jnp.exp(s_own - mf)
        lf = jnp.sum(w * l_all, axis=1, keepdims=True) + jnp.sum(p_own, axis=1, keepdims=True)
        of = jnp.dot(p_own.astype(BF16), vnew, preferred_element_type=F32)
        for nn in range(nblk):
            of = of + w[:, nn:nn + 1] * o_sc[nn]
        out = of / lf
        o_ref[...] = jnp.concatenate([out[h * tq:(h + 1) * tq, :] for h in range(ATTN_HEADS)],
                                     axis=1).astype(o_ref.dtype)


def _attn_sample(qr3, kr3, vr3, cache_kt, cache_vt, page_table, layer):
    bsz, tq, d_attn = qr3.shape
    page = cache_kt.shape[4]
    pages_per_blk = MOBA_BLOCK // page
    assert pages_per_blk == 2 and cache_kt.shape[2:4] == (ATTN_HEADS, HEAD_DIM) and page == LANES
    nblk = page_table.shape[1] // pages_per_blk
    rows = ATTN_HEADS * tq
    blocks_per_step = max(c for c in (8, 4, 2, 1) if nblk % c == 0)
    pages_per_step = pages_per_blk * blocks_per_step
    tok_spec = pl.BlockSpec((None, tq, d_attn), lambda b, n, pt: (b, 0, 0))

    def page_spec(j):
        return pl.BlockSpec((None, None, ATTN_HEADS, HEAD_DIM, page),
                            lambda b, n, pt, j=j: (layer, pt[b, pages_per_step * n + j], 0, 0, 0))

    page_specs = [page_spec(j) for j in range(pages_per_step)]
    grid_spec = pltpu.PrefetchScalarGridSpec(
        num_scalar_prefetch=1,
        grid=(bsz, nblk // blocks_per_step),
        in_specs=[tok_spec, tok_spec, tok_spec] + page_specs + page_specs,
        out_specs=pl.BlockSpec((None, tq, d_attn), lambda b, n, pt: (b, 0, 0)),
        scratch_shapes=[pltpu.VMEM((nblk, ATTN_HEADS, tq), F32), pltpu.VMEM((rows, LANES), F32),
                        pltpu.VMEM((rows, LANES), F32), pltpu.VMEM((nblk, rows, HEAD_DIM), F32)],
    )
    return pl.pallas_call(
        functools.partial(_sattn_kernel, nblk=nblk, blocks_per_step=blocks_per_step),
        grid_spec=grid_spec,
        out_shape=jax.ShapeDtypeStruct((bsz, tq, d_attn), BF16),
        compiler_params=_cparams(2),
        name="attn_sample",
    )(page_table, qr3, kr3, vr3,
      *([cache_kt] * pages_per_step), *([cache_vt] * pages_per_step))


def _merge_kernel(x_ref, ya_ref, ys_ref, yc_ref, g0_ref, g1_ref, g2_ref,
                  wr_ref, ws_ref, wa_ref, wo_ref, lg_ref, lb_ref, o_ref, *, alpha):
    merged = jax.nn.sigmoid(g0_ref[...]) * jnp.dot(ya_ref[...], wr_ref[...], preferred_element_type=F32)
    merged = merged + jax.nn.sigmoid(g1_ref[...]) * jnp.dot(ys_ref[...], ws_ref[...],
                                                           preferred_element_type=F32)
    merged = merged + jax.nn.sigmoid(g2_ref[...]) * jnp.dot(yc_ref[...], wa_ref[...],
                                                           preferred_element_type=F32)
    z = alpha * x_ref[...] + jnp.dot(merged.astype(BF16), wo_ref[...], preferred_element_type=F32)
    o_ref[...] = _layer_norm(z, lg_ref[...], lb_ref[...])


def _merge(x, ya, ys, yc, proj2, lw, tm, gate_col0, alpha):
    n, d = x.shape
    row = lambda w: pl.BlockSpec((tm, w), lambda i: (i, 0))
    gspec = lambda c: pl.BlockSpec((tm, d), lambda i, c=c: (i, c))
    layer = lw["layer"]
    mats = [lw["w_br_rnn"], lw["w_br_ssm"], lw["w_br_attn"], lw["w_out"]]
    ws = mats + [lw["ln1_g"], lw["ln1_b"]]
    return pl.pallas_call(
        functools.partial(_merge_kernel, alpha=alpha),
        grid=(n // tm,),
        in_specs=[row(d), row(ya.shape[1]), row(ys.shape[1]), row(yc.shape[1]),
                  gspec(gate_col0), gspec(gate_col0 + 1), gspec(gate_col0 + 2)]
        + [_resident(a.shape[1:], layer) for a in mats] + [_resident((1, d)), _resident((1, d))],
        out_specs=row(d),
        out_shape=jax.ShapeDtypeStruct((n, d), F32),
        compiler_params=_cparams(1),
        name="merge_ln",
    )(x, ya, ys, yc, proj2, proj2, proj2, *ws)


def _ffn_kernel(x_ref, wi_ref, wo_ref, lg_ref, lb_ref, o_ref, h_sc, *, alpha, tf):
    d_ff = wo_ref.shape[0]
    x = x_ref[...]
    xb = x.astype(BF16)
    for j in range(d_ff // tf):
        hg = jnp.dot(xb, wi_ref[:, j * tf:(j + 1) * tf], preferred_element_type=F32)
        hu = jnp.dot(xb, wi_ref[:, d_ff + j * tf:d_ff + (j + 1) * tf], preferred_element_type=F32)
        h_sc[:, j * tf:(j + 1) * tf] = ((hg * jax.nn.sigmoid(hg)) * hu).astype(BF16)
    y = jnp.dot(h_sc[...], wo_ref[...], preferred_element_type=F32)
    o_ref[...] = _layer_norm(alpha * x + y, lg_ref[...], lb_ref[...])


def _ffn(x, lw, tm, alpha):
    n, d = x.shape
    d_ff = lw["w_ffn_out"].shape[1]
    layer = lw["layer"]
    tf = max(c for c in range(LANES, 6 * MXU_DIM + 1, LANES) if d_ff % c == 0)
    return pl.pallas_call(
        functools.partial(_ffn_kernel, alpha=alpha, tf=tf),
        grid=(n // tm,),
        in_specs=[pl.BlockSpec((tm, d), lambda i: (i, 0)),
                  _resident((d, 2 * d_ff), layer), _resident((d_ff, d), layer),
                  _resident((1, d)), _resident((1, d))],
        out_specs=pl.BlockSpec((tm, d), lambda i: (i, 0)),
        out_shape=jax.ShapeDtypeStruct((n, d), F32),
        scratch_shapes=[pltpu.VMEM((tm, d_ff), BF16)],
        compiler_params=_cparams(1),
        name="ffn_ln",
    )(x, lw["w_ffn_in"], lw["w_ffn_out"], lw["ln2_g"], lw["ln2_b"])


def _block_diag(w, per_block):
    n, r, c = w.shape
    eye = jnp.eye(per_block, dtype=w.dtype)
    out = jnp.einsum("kgrc,gh->kgrhc", w.reshape(n // per_block, per_block, r, c), eye)
    return out.reshape(n // per_block, per_block * r, per_block * c)


BIG_WEIGHTS = ("w_in", "w_br_rnn", "w_br_ssm", "w_br_attn", "w_out", "w_ffn_in", "w_ffn_out")


def _permute_w_in(w, d_rnn, d_ssm, d_attn):
    a, b, c = d_rnn, d_rnn + d_ssm, d_rnn + d_ssm + 3 * d_attn
    return jnp.concatenate([w[..., :a], w[..., c:], w[..., a:b], w[..., b:c]], axis=-1)


def _prep_layer(l, p, big):
    heads_per_tile = MXU_DIM // (p["rg_w_a"].shape[-1])
    row = lambda a: a[l].reshape(1, -1)
    groups = p["s5_b_re"].shape[1]
    lw = {
        "layer": l,
        "w_in": big["w_in"],
        "conv_w": p["conv_w"][l],
        "conv_b": row(p["conv_b"]),
        "rg_wbd": jnp.concatenate([_block_diag(p["rg_w_a"][l], heads_per_tile),
                                   _block_diag(p["rg_w_x"][l], heads_per_tile)], axis=2).astype(BF16),
        "rg_b_a": row(p["rg_b_a"]), "rg_b_x": row(p["rg_b_x"]), "rg_lambda": row(p["rg_lambda"]),
        "s5_lam_re": row(p["s5_lambda_re"]), "s5_lam_im": row(p["s5_lambda_im"]),
        "s5_log_dt": jnp.repeat(p["s5_log_step"][l], SSM_STATE).reshape(1, -1),
        "s5_wbr": _block_diag(jnp.swapaxes(p["s5_b_re"][l], 1, 2), groups // 2),
        "s5_wbi": _block_diag(jnp.swapaxes(p["s5_b_im"][l], 1, 2), groups // 2),
        "s5_cre": _block_diag(jnp.swapaxes(p["s5_c_re"][l], 1, 2), LANES // SSM_GROUP).astype(BF16),
        "s5_cim": _block_diag(jnp.swapaxes(p["s5_c_im"][l], 1, 2), LANES // SSM_GROUP).astype(BF16),
        "s5_d": row(p["s5_d"]),
        "s5_w_glu": p["s5_w_glu"][l].astype(BF16),
        "s5_b_glu": row(p["s5_b_glu"]),
        "w_br_rnn": big["w_br_rnn"],
        "w_br_ssm": big["w_br_ssm"],
        "w_br_attn": big["w_br_attn"],
        "w_out": big["w_out"],
        "ln1_g": row(p["ln1_g"]), "ln1_b": row(p["ln1_b"]),
        "w_ffn_in": big["w_ffn_in"],
        "w_ffn_out": big["w_ffn_out"],
        "ln2_g": row(p["ln2_g"]), "ln2_b": row(p["ln2_b"]),
    }
    return lw


def _rope_tables(pos0, t):
    half = HEAD_DIM // 2
    inv = jnp.power(ROPE_THETA, -jnp.arange(half, dtype=F32) * (2.0 / HEAD_DIM))
    ang = (pos0 + jnp.arange(t)).astype(F32)[:, None] * inv
    cos, sin = jnp.cos(ang), jnp.sin(ang)
    reps = LANES // HEAD_DIM
    cos_t = jnp.tile(jnp.concatenate([cos, cos], axis=1), (1, reps))
    sin_t = jnp.tile(jnp.concatenate([-sin, sin], axis=1), (1, reps))
    return cos_t, sin_t


def _largest_tile(n, cap):
    t = min(n, cap)
    while n % t:
        t //= 2
    return t


def _trunk_layer(x3, lw, alpha, cbuf8, h0, s5r0, s5i0, *, prompt, pos0=0, cache=None):
    bsz, t, d = x3.shape
    n = bsz * t
    d_rnn = h0.shape[-1]
    d_ssm = lw["s5_d"].shape[-1]
    d_attn = ATTN_HEADS * HEAD_DIM
    x2 = x3.reshape(n, d)
    tm_rope = MOBA_BLOCK
    tn_proj = 6 * MXU_DIM
    assert n % tm_rope == 0
    if prompt:
        assert t % MOBA_BLOCK == 0 and MOBA_BLOCK < t <= GATE_SLOTS * MOBA_BLOCK
        cos_t, sin_t = _rope_tables(0, t)
        proj2, qt, krt, vrt, kb, vt, selt = _proj_rope(x2, lw["w_in"], lw["layer"], cos_t, sin_t, tm_rope, tn_proj,
                                                       True, t // MOBA_BLOCK)
    else:
        assert tm_rope % t == 0
        cos_t, sin_t = _rope_tables(pos0, t)
        cos_t = jnp.tile(cos_t, (tm_rope // t, 1))
        sin_t = jnp.tile(sin_t, (tm_rope // t, 1))
        proj2, qr, kr, vr = _proj_rope(x2, lw["w_in"], lw["layer"], cos_t, sin_t, tm_rope, tn_proj, False, 1)
    proj3 = proj2.reshape(bsz, t, -1)

    if prompt:
        bb, tc = 1, MOBA_BLOCK
    else:
        bb, tc = _largest_tile(bsz, 32), t
    assert tc % SUBLANES == 0 and tc & (tc - 1) == 0 and t % tc == 0
    ya, conv_new, h_new = _rglru(proj3, cbuf8, h0, lw, bb, tc)
    ys, s5r, s5i = _s5(proj3, s5r0, s5i0, lw, bb, tc, (d_rnn + 3 * d) // d_ssm)

    if prompt:
        yc = _attn_prompt(qt, kb, vt, selt, bsz, t)
        new_k = krt.reshape(bsz, ATTN_HEADS, HEAD_DIM, t).transpose(0, 3, 1, 2)
        new_v = vrt.reshape(bsz, ATTN_HEADS, HEAD_DIM, t).transpose(0, 3, 1, 2)
    else:
        cache_kt, cache_vt, page_table, layer = cache
        yc = _attn_sample(qr.reshape(bsz, t, d_attn), kr.reshape(bsz, t, d_attn), vr.reshape(bsz, t, d_attn),
                          cache_kt, cache_vt, page_table, layer).reshape(n, d_attn)
        new_k = kr.reshape(bsz, t, ATTN_HEADS, HEAD_DIM)
        new_v = vr.reshape(bsz, t, ATTN_HEADS, HEAD_DIM)

    gate_col0 = d_rnn // d
    x1 = _merge(x2, ya.reshape(n, d_rnn), ys.reshape(n, d_ssm), yc, proj2, lw, _largest_tile(n, 512),
                gate_col0, alpha)
    x_out = _ffn(x1, lw, _largest_tile(n, 256), alpha)
    return (x_out.reshape(bsz, t, d), new_k, new_v, conv_new, h_new.reshape(bsz, d_rnn), s5r, s5i)


def kernel(x_prompt, x_sample, cache_k, cache_v, state_conv, state_rglru, state_s5_re, state_s5_im, page_table,
           w_in, conv_w, conv_b, rg_w_a, rg_b_a, rg_w_x, rg_b_x, rg_lambda,
           s5_lambda_re, s5_lambda_im, s5_b_re, s5_b_im, s5_c_re, s5_c_im, s5_d, s5_log_step, s5_w_glu, s5_b_glu,
           w_br_rnn, w_br_ssm, w_br_attn, w_out, ln1_g, ln1_b, w_ffn_in, w_ffn_out, ln2_g, ln2_b):
    params = dict(w_in=w_in, conv_w=conv_w, conv_b=conv_b, rg_w_a=rg_w_a, rg_b_a=rg_b_a, rg_w_x=rg_w_x,
                  rg_b_x=rg_b_x, rg_lambda=rg_lambda, s5_lambda_re=s5_lambda_re, s5_lambda_im=s5_lambda_im,
                  s5_b_re=s5_b_re, s5_b_im=s5_b_im, s5_c_re=s5_c_re, s5_c_im=s5_c_im, s5_d=s5_d,
                  s5_log_step=s5_log_step, s5_w_glu=s5_w_glu, s5_b_glu=s5_b_glu, w_br_rnn=w_br_rnn,
                  w_br_ssm=w_br_ssm, w_br_attn=w_br_attn, w_out=w_out, ln1_g=ln1_g, ln1_b=ln1_b,
                  w_ffn_in=w_ffn_in, w_ffn_out=w_ffn_out, ln2_g=ln2_g, ln2_b=ln2_b)
    depth = w_in.shape[0]
    alpha = (2.0 * depth) ** 0.25
    bp = x_prompt.shape[0]
    bs = x_sample.shape[0]
    d_rnn = state_rglru.shape[-1]
    groups, n_p = state_s5_re.shape[-2:]
    n_state = groups * n_p
    n_pages = page_table.shape[1]
    page = cache_k.shape[2]
    past_len = n_pages * page
    d_attn = ATTN_HEADS * HEAD_DIM
    assert past_len % MOBA_BLOCK == 0 and x_sample.shape[1] <= MOBA_BLOCK
    cache_kt = cache_k.transpose(0, 1, 3, 4, 2)
    cache_vt = cache_v.transpose(0, 1, 3, 4, 2)

    zeros_p = lambda *s: jnp.zeros((bp,) + s, F32)
    yp, ys = x_prompt, x_sample
    outs_p, outs_s = [], []
    big = {k: params[k].astype(BF16) for k in BIG_WEIGHTS}
    big["w_in"] = _permute_w_in(big["w_in"], d_rnn, n_state // SSM_STATE * SSM_GROUP, d_attn)
    for l in range(depth):
        lw = _prep_layer(l, params, big)
        res = _trunk_layer(yp, lw, alpha, zeros_p(SUBLANES, d_rnn), zeros_p(1, d_rnn),
                           zeros_p(1, n_state), zeros_p(1, n_state), prompt=True)
        yp = res[0]
        outs_p.append(res[1:])
        cbuf8 = jnp.pad(state_conv[l], ((0, 0), (SUBLANES - (CONV_WIDTH - 1), 0), (0, 0)))
        res = _trunk_layer(ys, lw, alpha, cbuf8, state_rglru[l].reshape(bs, 1, d_rnn),
                           state_s5_re[l].reshape(bs, 1, n_state), state_s5_im[l].reshape(bs, 1, n_state),
                           prompt=False, pos0=past_len, cache=(cache_kt, cache_vt, page_table, l))
        ys = res[0]
        outs_s.append(res[1:])

    def stack(outs, i, shape=None):
        arrs = [o[i] if shape is None else o[i].reshape(shape) for o in outs]
        return jnp.stack(arrs)

    return (yp, ys,
            stack(outs_p, 0), stack(outs_p, 1), stack(outs_p, 2), stack(outs_p, 3),
            stack(outs_p, 4, (bp, groups, n_p)), stack(outs_p, 5, (bp, groups, n_p)),
            stack(outs_s, 0), stack(outs_s, 1), stack(outs_s, 2), stack(outs_s, 3),
            stack(outs_s, 4, (bs, groups, n_p)), stack(outs_s, 5, (bs, groups, n_p)))
```

```python
import functools

import jax
import jax.numpy as jnp
import numpy as np
from jax import lax
from jax.experimental import pallas as pl
from jax.experimental.pallas import tpu as pltpu

F32 = jnp.float32
BF16 = jnp.bfloat16

ATTN_HEADS = 8
HEAD_DIM = 64
RNN_HEADS = 16
CONV_WIDTH = 4
LRU_C = 8.0
SSM_GROUP = 16
SSM_STATE = 64
MOBA_BLOCK = 256
MOBA_TOPK = 3
ROPE_THETA = 10000.0
LN_EPS = 1e-5
NEG_INF = -1e30
LOG2_E = 1.4426950408889634

LANES = 128
SUBLANES = 8
MXU_DIM = 256
VMEM_LIMIT_BYTES = 56 * 1024 * 1024

GATE_SLOTS = 32
SCAN_SHIFTS = (1, 2, 4)
_NT = (((1,), (1,)), ((), ()))


def _cparams(n_axes):
    return pltpu.CompilerParams(dimension_semantics=("arbitrary",) * n_axes,
                                vmem_limit_bytes=VMEM_LIMIT_BYTES)


def _resident(shape, layer=None):
    if layer is None:
        return pl.BlockSpec(shape, lambda *_: (0,) * len(shape), pipeline_mode=pl.Buffered(1))
    return pl.BlockSpec((None,) + tuple(shape), lambda *_: (layer,) + (0,) * len(shape),
                        pipeline_mode=pl.Buffered(1))


def _div_pow2(x, d):
    assert d & (d - 1) == 0
    return lax.shift_right_logical(x, d.bit_length() - 1)


def _layer_norm(z, g, b):
    mu = jnp.mean(z, axis=-1, keepdims=True)
    zc = z - mu
    var = jnp.mean(zc * zc, axis=-1, keepdims=True)
    return zc * lax.rsqrt(var + LN_EPS) * g + b


def _rglru_kernel(x_ref, cbuf_ref, h0_ref, cw_ref, cb_ref, wbd_ref, ba_ref, bx_ref, lam_ref,
                  ya_ref, cnew_ref, hnew_ref, xbuf, hcar, *, bb, tc):
    ti = pl.program_id(1)
    nt = pl.num_programs(1)
    c_dim = x_ref.shape[-1]
    m = bb * tc

    @pl.when(ti == 0)
    def _():
        xbuf[:, 0:SUBLANES, :] = cbuf_ref[...]
        hcar[...] = h0_ref[...]

    x = x_ref[...]
    xbuf[:, SUBLANES:SUBLANES + tc, :] = x
    cw = cw_ref[...]
    xc = cb_ref[...] + cw[0:1] * xbuf[:, 5:5 + tc, :]
    xc = xc + cw[1:2] * xbuf[:, 6:6 + tc, :]
    xc = xc + cw[2:3] * xbuf[:, 7:7 + tc, :]
    xc = xc + cw[3:4] * x
    xc2 = xc.reshape(m, c_dim)
    xcb = xc2.astype(BF16)

    nl = -lam_ref[...]
    softplus = jnp.maximum(nl, 0.0) + jnp.log1p(jnp.exp(-jnp.abs(nl)))
    c_row = -LRU_C * softplus
    assert tc == SUBLANES or bb == 1
    row = lax.broadcasted_iota(jnp.int32, (m // SUBLANES, SUBLANES, MXU_DIM), 1)
    hc = hcar[...]
    for g in range(c_dim // MXU_DIM):
        sl = slice(g * MXU_DIM, (g + 1) * MXU_DIM)
        ga = jnp.dot(xcb[:, sl], wbd_ref[g], preferred_element_type=F32)
        r = jax.nn.sigmoid(ga[:, :MXU_DIM] + ba_ref[:, sl])
        i = jax.nn.sigmoid(ga[:, MXU_DIM:] + bx_ref[:, sl])
        log_a = c_row[:, sl] * r
        a = jnp.exp(log_a)
        mult = jnp.sqrt(jnp.maximum(-jnp.tanh(log_a) * (a * a + 1.0), 0.0))
        b = mult * (i * xc2[:, sl])
        a = a.reshape(m // SUBLANES, SUBLANES, MXU_DIM)
        b = b.reshape(m // SUBLANES, SUBLANES, MXU_DIM)
        s = 1
        while s < SUBLANES:
            msk = row >= s
            a_sh = jnp.where(msk, pltpu.roll(a, s, 1), 1.0)
            b_sh = jnp.where(msk, pltpu.roll(b, s, 1), 0.0)
            b = a * b_sh + b
            a = a * a_sh
            s *= 2
        a = a.reshape(m, MXU_DIM)
        b = b.reshape(m, MXU_DIM)
        if tc == SUBLANES:
            hcg = jnp.broadcast_to(hc[:, :, sl], (bb, tc, MXU_DIM)).reshape(m, MXU_DIM)
            h = b + a * hcg
        else:
            carry = hc[0, :, sl]
            groups = []
            for j in range(tc // SUBLANES):
                rs = slice(j * SUBLANES, (j + 1) * SUBLANES)
                hj = b[rs] + a[rs] * carry
                groups.append(hj)
                carry = hj[SUBLANES - 1:SUBLANES]
            h = jnp.concatenate(groups, axis=0)
        h3 = h.reshape(bb, tc, MXU_DIM)
        ya_ref[:, :, sl] = h3.astype(ya_ref.dtype)
        hcar[:, :, sl] = h3[:, tc - 1:tc, :]

    xbuf[:, 0:SUBLANES, :] = xbuf[:, tc:tc + SUBLANES, :]

    @pl.when(ti == nt - 1)
    def _():
        cnew_ref[...] = xbuf[:, 5:8, :]
        hnew_ref[...] = hcar[...]


def _rglru(proj3, cbuf8, h0, lw, bb, tc):
    bsz, t, _ = proj3.shape
    c_dim = h0.shape[-1]
    full = lambda *shape: pl.BlockSpec(shape, lambda bi, ti: (0,) * len(shape))
    kern = functools.partial(_rglru_kernel, bb=bb, tc=tc)
    return pl.pallas_call(
        kern,
        grid=(bsz // bb, t // tc),
        in_specs=[pl.BlockSpec((bb, tc, c_dim), lambda bi, ti: (bi, ti, 0)),
                  pl.BlockSpec((bb, SUBLANES, c_dim), lambda bi, ti: (bi, 0, 0)),
                  pl.BlockSpec((bb, 1, c_dim), lambda bi, ti: (bi, 0, 0)),
                  full(CONV_WIDTH, c_dim), full(1, c_dim),
                  full(c_dim // MXU_DIM, MXU_DIM, 2 * MXU_DIM),
                  full(1, c_dim), full(1, c_dim), full(1, c_dim)],
        out_specs=[pl.BlockSpec((bb, tc, c_dim), lambda bi, ti: (bi, ti, 0)),
                   pl.BlockSpec((bb, CONV_WIDTH - 1, c_dim), lambda bi, ti: (bi, 0, 0)),
                   pl.BlockSpec((bb, 1, c_dim), lambda bi, ti: (bi, 0, 0))],
        out_shape=[jax.ShapeDtypeStruct((bsz, t, c_dim), BF16),
                   jax.ShapeDtypeStruct((bsz, CONV_WIDTH - 1, c_dim), F32),
                   jax.ShapeDtypeStruct((bsz, 1, c_dim), F32)],
        scratch_shapes=[pltpu.VMEM((bb, tc + SUBLANES, c_dim), F32),
                        pltpu.VMEM((bb, 1, c_dim), F32)],
        compiler_params=_cparams(2),
        name="rglru",
    )(proj3, cbuf8, h0, lw["conv_w"], lw["conv_b"], lw["rg_wbd"], lw["rg_b_a"], lw["rg_b_x"],
      lw["rg_lambda"])


def _cmul(ar, ai, br, bi):
    return ar * br - ai * bi, ar * bi + ai * br


def _s5_segment_scan(hr_sc, hi_sc, pwr_sc, pwi_sc, cr_sc, ci_sc, seg_len, lane_chunk):
    n_state = hr_sc.shape[-1]
    for lc in range(n_state // lane_chunk):
        ls = slice(lc * lane_chunk, (lc + 1) * lane_chunk)
        a1r = jnp.broadcast_to(pwr_sc[0:1, ls], (SUBLANES, lane_chunk))
        a1i = jnp.broadcast_to(pwi_sc[0:1, ls], (SUBLANES, lane_chunk))

        def local_step(j, h, ls=ls, a1r=a1r, a1i=a1i):
            r0 = pl.multiple_of(j * SUBLANES, SUBLANES)
            dr, di = _cmul(a1r, a1i, h[0], h[1])
            xr = dr + hr_sc[0, pl.ds(r0, SUBLANES), ls]
            xi = di + hi_sc[0, pl.ds(r0, SUBLANES), ls]
            hr_sc[0, pl.ds(r0, SUBLANES), ls] = xr
            hi_sc[0, pl.ds(r0, SUBLANES), ls] = xi
            return xr, xi

        zero = jnp.zeros((SUBLANES, lane_chunk), F32)
        er, ei = lax.fori_loop(0, seg_len, local_step, (zero, zero))

        alr = pwr_sc[seg_len - 1:seg_len, ls]
        ali = pwi_sc[seg_len - 1:seg_len, ls]
        cr, ci = cr_sc[0, :, ls], ci_sc[0, :, ls]
        entry_r, entry_i = [], []
        for s in range(SUBLANES):
            entry_r.append(cr)
            entry_i.append(ci)
            dr, di = _cmul(alr, ali, cr, ci)
            cr, ci = er[s:s + 1] + dr, ei[s:s + 1] + di
        cr_sc[0, :, ls] = cr
        ci_sc[0, :, ls] = ci
        hin_r = jnp.concatenate(entry_r, axis=0)
        hin_i = jnp.concatenate(entry_i, axis=0)

        def fix_step(j, d, ls=ls, a1r=a1r, a1i=a1i):
            r0 = pl.multiple_of(j * SUBLANES, SUBLANES)
            hr_sc[0, pl.ds(r0, SUBLANES), ls] += d[0]
            hi_sc[0, pl.ds(r0, SUBLANES), ls] += d[1]
            return _cmul(a1r, a1i, d[0], d[1])

        lax.fori_loop(0, seg_len, fix_step, _cmul(a1r, a1i, hin_r, hin_i))


def _s5_kernel(u_ref, h0r_ref, h0i_ref, lamr_ref, lami_ref, ldt_ref, wbr_ref, wbi_ref,
               cre_ref, cim_ref, d_ref, wglu_ref, bglu_ref,
               ys_ref, sr_ref, si_ref,
               wb_sc, apr_sc, api_sc, lvr_sc, lvi_sc, pwr_sc, pwi_sc, col_sc, hr_sc, hi_sc, cr_sc, ci_sc,
               *, bb, tc, lane_chunk):
    bi = pl.program_id(0)
    ti = pl.program_id(1)
    nt = pl.num_programs(1)
    m = bb * tc
    n_state = hr_sc.shape[-1]
    d_ssm = u_ref.shape[-1]
    half = n_state // 2
    segmented = tc > SUBLANES
    seg_len = tc // SUBLANES
    assert not segmented or bb == 1

    @pl.when((bi == 0) & (ti == 0))
    def _():
        dt = jnp.exp(ldt_ref[...])
        lr = lamr_ref[...]
        li = lami_ref[...]
        mag = jnp.exp(lr * dt)
        abr = mag * jnp.cos(li * dt)
        abi = mag * jnp.sin(li * dt)
        nr = abr - 1.0
        den = lr * lr + li * li
        zr = (nr * lr + abi * li) / den
        zi = (abi * lr - nr * li) / den
        for kb in range(2):
            ks = slice(kb * half, (kb + 1) * half)
            br = wbr_ref[kb]
            bim = wbi_ref[kb]
            wb_sc[kb, :, 0:half] = (zr[:, ks] * br - zi[:, ks] * bim).astype(BF16)
            wb_sc[kb, :, half:n_state] = (zr[:, ks] * bim + zi[:, ks] * br).astype(BF16)
        row8 = lax.broadcasted_iota(jnp.int32, (SUBLANES, n_state), 0)
        pr = jnp.broadcast_to(abr, (SUBLANES, n_state))
        pi = jnp.broadcast_to(abi, (SUBLANES, n_state))
        for s in SCAN_SHIFTS:
            msk = row8 >= s
            qr = jnp.where(msk, pltpu.roll(pr, s, 0), 1.0)
            qi = jnp.where(msk, pltpu.roll(pi, s, 0), 0.0)
            pr, pi = _cmul(pr, pi, qr, qi)
        apr_sc[...] = pr
        api_sc[...] = pi
        for idx, s in enumerate(SCAN_SHIFTS):
            inside = row8 >= s
            lvr_sc[idx] = jnp.where(inside, jnp.broadcast_to(pr[s - 1:s], (SUBLANES, n_state)), 0.0)
            lvi_sc[idx] = jnp.where(inside, jnp.broadcast_to(pi[s - 1:s], (SUBLANES, n_state)), 0.0)
        if segmented:
            pwr_sc[0:SUBLANES] = pr
            pwi_sc[0:SUBLANES] = pi
            have = SUBLANES
            while have < seg_len:
                tr, ti_ = _cmul(pwr_sc[have - 1:have], pwi_sc[have - 1:have], pwr_sc[0:have], pwi_sc[0:have])
                pwr_sc[have:2 * have] = tr
                pwi_sc[have:2 * have] = ti_
                have *= 2

    @pl.when(ti == 0)
    def _():
        cr_sc[...] = h0r_ref[...]
        ci_sc[...] = h0i_ref[...]

    n_col = d_ssm // LANES
    if segmented:
        for c in range(n_col):
            col_sc[c] = u_ref[0, :, c * LANES:(c + 1) * LANES]
        u = jnp.concatenate(
            [jnp.concatenate([col_sc[c, pl.ds(j, SUBLANES, stride=seg_len), :] for j in range(seg_len)], axis=0)
             for c in range(n_col)], axis=1)
    else:
        u = u_ref[...].reshape(m, d_ssm)
    ub = u.astype(BF16)
    k_half = d_ssm // 2
    for kb in range(2):
        bu = jnp.dot(ub[:, kb * k_half:(kb + 1) * k_half], wb_sc[kb], preferred_element_type=F32)
        hr_sc[:, :, kb * half:(kb + 1) * half] = bu[:, :half].reshape(bb, tc, half)
        hi_sc[:, :, kb * half:(kb + 1) * half] = bu[:, half:].reshape(bb, tc, half)

    if segmented:
        _s5_segment_scan(hr_sc, hi_sc, pwr_sc, pwi_sc, cr_sc, ci_sc, seg_len, lane_chunk)
    rows8 = bb * SUBLANES
    for lc in range(0 if segmented else n_state // lane_chunk):
        ls = slice(lc * lane_chunk, (lc + 1) * lane_chunk)

        def per_seq(tab):
            return jnp.broadcast_to(tab[None], (bb, SUBLANES, lane_chunk)).reshape(rows8, lane_chunk)

        steps = tuple((s, per_seq(lvr_sc[idx, :, ls]), per_seq(lvi_sc[idx, :, ls]))
                      for idx, s in enumerate(SCAN_SHIFTS))
        prt = per_seq(apr_sc[:, ls])
        pit = per_seq(api_sc[:, ls])

        def body(j, carry, ls=ls, steps=steps, prt=prt, pit=pit):
            cr, ci = carry
            r0 = pl.multiple_of(j * SUBLANES, SUBLANES)
            xr = hr_sc[:, pl.ds(r0, SUBLANES), ls].reshape(rows8, lane_chunk)
            xi = hi_sc[:, pl.ds(r0, SUBLANES), ls].reshape(rows8, lane_chunk)
            for s, ar, ai in steps:
                dr, di = _cmul(ar, ai, pltpu.roll(xr, s, 0), pltpu.roll(xi, s, 0))
                xr = xr + dr
                xi = xi + di
            crb = jnp.broadcast_to(cr, (bb, SUBLANES, lane_chunk)).reshape(rows8, lane_chunk)
            cib = jnp.broadcast_to(ci, (bb, SUBLANES, lane_chunk)).reshape(rows8, lane_chunk)
            dr, di = _cmul(prt, pit, crb, cib)
            xr3 = (xr + dr).reshape(bb, SUBLANES, lane_chunk)
            xi3 = (xi + di).reshape(bb, SUBLANES, lane_chunk)
            hr_sc[:, pl.ds(r0, SUBLANES), ls] = xr3
            hi_sc[:, pl.ds(r0, SUBLANES), ls] = xi3
            return xr3[:, SUBLANES - 1:SUBLANES, :], xi3[:, SUBLANES - 1:SUBLANES, :]

        cr, ci = lax.fori_loop(0, tc // SUBLANES, body, (cr_sc[:, :, ls], ci_sc[:, :, ls]))
        cr_sc[:, :, ls] = cr
        ci_sc[:, :, ls] = ci

    n_pack = cre_ref.shape[0]
    k_pack = n_state // n_pack
    parts = []
    for p4 in range(n_pack):
        ks = slice(p4 * k_pack, (p4 + 1) * k_pack)
        hrb = hr_sc[:, :, ks].reshape(m, k_pack).astype(BF16)
        hib = hi_sc[:, :, ks].reshape(m, k_pack).astype(BF16)
        parts.append(jnp.dot(hrb, cre_ref[p4], preferred_element_type=F32)
                     - jnp.dot(hib, cim_ref[p4], preferred_element_type=F32))
    y = jnp.concatenate(parts, axis=1) + d_ref[...] * u
    g = y * (0.5 * (1.0 + jnp.tanh(np.sqrt(2.0 / np.pi).astype(np.float32) * (y + 0.044715 * (y * y * y)))))
    z = jnp.dot(g.astype(BF16), wglu_ref[...], preferred_element_type=F32) + bglu_ref[...]
    out = g * jax.nn.sigmoid(z)
    if segmented:
        for c in range(n_col):
            col_sc[c] = out[:, c * LANES:(c + 1) * LANES]
        for c in range(n_col):
            for s in range(SUBLANES):
                ys_ref[0, s * seg_len:(s + 1) * seg_len, c * LANES:(c + 1) * LANES] = (
                    col_sc[c, pl.ds(s, seg_len, stride=SUBLANES), :].astype(ys_ref.dtype))
    else:
        ys_ref[...] = out.reshape(bb, tc, d_ssm).astype(ys_ref.dtype)

    @pl.when(ti == nt - 1)
    def _():
        sr_ref[...] = cr_sc[...]
        si_ref[...] = ci_sc[...]


def _s5(proj3, h0r, h0i, lw, bb, tc, col_block):
    bsz, t, _ = proj3.shape
    n_state = h0r.shape[-1]
    d_ssm = lw["s5_d"].shape[-1]
    n_pack = lw["s5_cre"].shape[0]
    full = lambda *shape: pl.BlockSpec(shape, lambda bi, ti: (0,) * len(shape))
    lane_chunk = max(LANES, 4 * LANES // bb)
    n_pow = max(SUBLANES, tc // SUBLANES)
    assert n_pow & (n_pow - 1) == 0
    kern = functools.partial(_s5_kernel, bb=bb, tc=tc, lane_chunk=lane_chunk)
    state_spec = pl.BlockSpec((bb, 1, n_state), lambda bi, ti: (bi, 0, 0))
    return pl.pallas_call(
        kern,
        grid=(bsz // bb, t // tc),
        in_specs=[pl.BlockSpec((bb, tc, d_ssm), lambda bi, ti: (bi, ti, col_block)),
                  state_spec, state_spec,
                  full(1, n_state), full(1, n_state), full(1, n_state),
                  full(2, d_ssm // 2, n_state // 2), full(2, d_ssm // 2, n_state // 2),
                  full(n_pack, n_state // n_pack, d_ssm // n_pack),
                  full(n_pack, n_state // n_pack, d_ssm // n_pack),
                  full(1, d_ssm), full(d_ssm, d_ssm), full(1, d_ssm)],
        out_specs=[pl.BlockSpec((bb, tc, d_ssm), lambda bi, ti: (bi, ti, 0)), state_spec, state_spec],
        out_shape=[jax.ShapeDtypeStruct((bsz, t, d_ssm), BF16),
                   jax.ShapeDtypeStruct((bsz, 1, n_state), F32),
                   jax.ShapeDtypeStruct((bsz, 1, n_state), F32)],
        scratch_shapes=[pltpu.VMEM((2, d_ssm // 2, n_state), BF16),
                        pltpu.VMEM((SUBLANES, n_state), F32), pltpu.VMEM((SUBLANES, n_state), F32),
                        pltpu.VMEM((len(SCAN_SHIFTS), SUBLANES, n_state), F32),
                        pltpu.VMEM((len(SCAN_SHIFTS), SUBLANES, n_state), F32),
                        pltpu.VMEM((n_pow, n_state), F32), pltpu.VMEM((n_pow, n_state), F32),
                        pltpu.VMEM((d_ssm // LANES, tc, LANES), F32),
                        pltpu.VMEM((bb, tc, n_state), F32), pltpu.VMEM((bb, tc, n_state), F32),
                        pltpu.VMEM((bb, 1, n_state), F32), pltpu.VMEM((bb, 1, n_state), F32)],
        compiler_params=_cparams(2),
        name="s5",
    )(proj3, h0r, h0i, lw["s5_lam_re"], lw["s5_lam_im"], lw["s5_log_dt"], lw["s5_wbr"], lw["s5_wbi"],
      lw["s5_cre"], lw["s5_cim"], lw["s5_d"], lw["s5_w_glu"], lw["s5_b_glu"])


def _proj_rope_kernel(x_ref, w_ref, cos_ref, sin_ref, *refs, tn, gate, blocks_per_seq, n_prev):
    prev, o_ref, rest = refs[:2 * n_prev], refs[2 * n_prev], list(refs[2 * n_prev + 1:])
    if n_prev:
        k_stack, v_stack = rest[1], rest[2]
        for p in range(n_prev):
            k_stack[p] = prev[p][...]
            v_stack[p] = prev[n_prev + p][...]
        rest[1], rest[2] = k_stack.at[n_prev], v_stack.at[n_prev]
    xb = x_ref[...].astype(BF16)
    n_plain = o_ref.shape[1]
    d_attn = ATTN_HEADS * HEAD_DIM
    qkv = jnp.dot(xb, w_ref[:, n_plain:n_plain + 3 * d_attn], preferred_element_type=F32)
    _rope_stage(qkv[:, :d_attn], qkv[:, d_attn:2 * d_attn], qkv[:, 2 * d_attn:], cos_ref, sin_ref, rest,
                gate=gate, blocks_per_seq=blocks_per_seq)
    for j in range(n_plain // tn):
        cs = slice(j * tn, (j + 1) * tn)
        o_ref[:, cs] = jnp.dot(xb, w_ref[:, cs], preferred_element_type=F32)


def _rope_stage(q, k, v, cos_ref, sin_ref, rest, *, gate, blocks_per_seq):
    if gate:
        qs_ref, kr_ref, vr_ref, kb_ref, vb_ref, sel_ref, km_sc = rest
    else:
        qs_ref, kr_ref, vr_ref = rest
    tm, d_attn = q.shape
    reps = d_attn // LANES
    cos = jnp.concatenate([cos_ref[...]] * reps, axis=1)
    sin = jnp.concatenate([sin_ref[...]] * reps, axis=1)
    lane = lax.broadcasted_iota(jnp.int32, (tm, d_attn), 1)
    first_half = (lane & (HEAD_DIM - 1)) < HEAD_DIM // 2

    def rot(x):
        partner = jnp.where(first_half, pltpu.roll(x, d_attn - HEAD_DIM // 2, 1),
                            pltpu.roll(x, HEAD_DIM // 2, 1))
        return x * cos + partner * sin

    qr = rot(q)
    kr = rot(k)
    if not gate:
        qs_ref[...] = qr
        kr_ref[...] = kr
        vr_ref[...] = v
        return
    krt = kr.T
    vt = v.T
    kr_ref[...] = krt
    vr_ref[...] = vt
    qs_ref[...] = (qr * (HEAD_DIM ** -0.5 * LOG2_E)).T.astype(BF16)
    kb_ref[...] = kr.astype(BF16)
    vb_ref[...] = vt.astype(BF16)

    tb = pl.program_id(0) % blocks_per_seq

    @pl.when(tb == 0)
    def _():
        km_sc[...] = jnp.zeros_like(km_sc)

    gates = lax.dot_general(km_sc[...], qr, _NT, precision=lax.Precision.HIGHEST,
                            preferred_element_type=F32)
    slot = lax.broadcasted_iota(jnp.int32, (GATE_SLOTS, tm), 0)
    slotf = slot.astype(F32)
    for h in range(ATTN_HEADS):
        hs = slice(h * GATE_SLOTS, (h + 1) * GATE_SLOTS)
        gh = jnp.where(slot < tb, gates[hs, :], -jnp.inf)
        sel = jnp.zeros((GATE_SLOTS, tm), F32)
        for _ in range(MOBA_TOPK):
            mx = jnp.max(gh, axis=0, keepdims=True)
            cand = (gh == mx) & (mx > -jnp.inf)
            first = jnp.min(jnp.where(cand, slotf, float(GATE_SLOTS)), axis=0, keepdims=True)
            pick = slotf == first
            sel = jnp.where(pick, 1.0, sel)
            gh = jnp.where(pick, -jnp.inf, gh)
        sel_ref[hs, :] = sel

    km = jnp.sum(kr, axis=0, keepdims=True) * (1.0 / MOBA_BLOCK)
    lane_row = lax.broadcasted_iota(jnp.int32, (1, d_attn), 1)
    for h in range(ATTN_HEADS):
        km_sc[pl.ds(h * GATE_SLOTS + tb, 1), :] = jnp.where(_div_pow2(lane_row, HEAD_DIM) == h, km, 0.0)


def _proj_rope(x, w, layer, cos_t, sin_t, tm, tn, gate, blocks_per_seq, prev_kv=()):
    n, d = x.shape
    d_attn = ATTN_HEADS * HEAD_DIM
    n_plain = w.shape[-1] - 3 * d_attn
    assert n_plain % tn == 0 and len(prev_kv) % 2 == 0 and (gate or not prev_kv)
    n_prev = len(prev_kv) // 2
    n_tab = cos_t.shape[0] // tm
    tab_spec = pl.BlockSpec((tm, LANES), lambda i: (i % n_tab, 0))
    row_spec = pl.BlockSpec((tm, d_attn), lambda i: (i, 0))
    kern = functools.partial(_proj_rope_kernel, tn=tn, gate=gate, blocks_per_seq=blocks_per_seq, n_prev=n_prev)
    prev_specs = []
    if gate:
        n_col = ATTN_HEADS * GATE_SLOTS
        bsz = n // (tm * blocks_per_seq)
        t = tm * blocks_per_seq
        nb = blocks_per_seq
        col_spec = lambda rows: pl.BlockSpec((None, rows, tm), lambda i: (i // nb, 0, i % nb))
        kv_spec, kv_shape = col_spec(d_attn), jax.ShapeDtypeStruct((bsz, d_attn, t), F32)
        if n_prev:
            prev_specs = [col_spec(d_attn)] * (2 * n_prev)
            kv_spec = pl.BlockSpec((n_prev + 1, None, d_attn, tm), lambda i: (0, i // nb, 0, i % nb))
            kv_shape = jax.ShapeDtypeStruct((n_prev + 1, bsz, d_attn, t), F32)
        out_specs = [col_spec(d_attn), kv_spec, kv_spec, row_spec, col_spec(d_attn), col_spec(n_col)]
        out_shape = [jax.ShapeDtypeStruct((bsz, d_attn, t), BF16), kv_shape, kv_shape,
                     jax.ShapeDtypeStruct((n, d_attn), BF16),
                     jax.ShapeDtypeStruct((bsz, d_attn, t), BF16), jax.ShapeDtypeStruct((bsz, n_col, t), F32)]
        scratch = [pltpu.VMEM((n_col, d_attn), F32)]
    else:
        out_specs = [row_spec] * 3
        out_shape = [jax.ShapeDtypeStruct((n, d_attn), F32)] * 3
        scratch = []
    return pl.pallas_call(
        kern,
        grid=(n // tm,),
        in_specs=[pl.BlockSpec((tm, d), lambda i: (i, 0)), _resident((d, w.shape[-1]), layer), tab_spec, tab_spec]
        + prev_specs,
        out_specs=[pl.BlockSpec((tm, n_plain), lambda i: (i, 0))] + out_specs,
        out_shape=[jax.ShapeDtypeStruct((n, n_plain), F32)] + out_shape,
        scratch_shapes=scratch,
        compiler_params=_cparams(1),
        name="proj_rope_gate" if gate else "proj_rope",
    )(x, w, cos_t, sin_t, *prev_kv)


def _attn_kernel(qt_ref, k_ref, vt_ref, sel_ref, o_ref, acc_sc):
    qi = pl.program_id(2)
    width, blk = qt_ref.shape
    nh = width // HEAD_DIM
    qt = qt_ref[...]
    row_head = _div_pow2(lax.broadcasted_iota(jnp.int32, qt.shape, 0), HEAD_DIM)
    zero = jnp.zeros_like(qt)
    q_heads = jnp.concatenate([jnp.where(row_head == e, qt, zero) for e in range(nh)], axis=1)
    keyi = lax.broadcasted_iota(jnp.int32, (blk, blk), 0)
    qcol = lax.broadcasted_iota(jnp.int32, (blk, blk), 1)
    causal = keyi <= qcol

    r0 = pl.multiple_of(qi * blk, blk)
    ko = k_ref[pl.ds(r0, blk), :]
    vo = vt_ref[:, pl.ds(r0, blk)]
    s_own = jnp.dot(ko, q_heads, preferred_element_type=F32)
    ones_own = jnp.ones((SUBLANES, blk), BF16)
    stats = []
    for e in range(nh):
        s = jnp.where(causal, s_own[:, e * blk:(e + 1) * blk], NEG_INF)
        mx = jnp.max(s, axis=0, keepdims=True)
        p = jnp.exp2(s - mx).astype(BF16)
        stats += [mx, jnp.dot(ones_own, p, preferred_element_type=F32)[0:1]]
        acc_sc[e] = jnp.dot(vo[e * HEAD_DIM:(e + 1) * HEAD_DIM, :], p, preferred_element_type=F32)

    ones_pair = jnp.ones((SUBLANES, 2 * blk), BF16)

    def body(j, carry):
        rn = pl.multiple_of(j * (2 * blk), 2 * blk)
        kn = k_ref[pl.ds(rn, 2 * blk), :]
        vn = vt_ref[:, pl.ds(rn, 2 * blk)]
        s_all = jnp.dot(kn, q_heads, preferred_element_type=F32)
        out = []
        for e in range(nh):
            mx, l = carry[2 * e], carry[2 * e + 1]
            mn = mx
            halves = []
            for c in range(2):
                keep = sel_ref[pl.ds(e * GATE_SLOTS + 2 * j + c, 1), :] > 0.5
                sc = s_all[c * blk:(c + 1) * blk, e * blk:(e + 1) * blk]
                mn = jnp.maximum(mn, jnp.where(keep, jnp.max(sc, axis=0, keepdims=True), NEG_INF))
                halves.append((sc, keep))
            alpha = jnp.exp2(mx - mn)
            p = jnp.concatenate([jnp.exp2(sc - jnp.where(keep, mn, -NEG_INF)).astype(BF16)
                                 for sc, keep in halves], axis=0)
            l = alpha * l + jnp.dot(ones_pair, p, preferred_element_type=F32)[0:1]
            acc_sc[e] = alpha * acc_sc[e] + jnp.dot(vn[e * HEAD_DIM:(e + 1) * HEAD_DIM, :], p,
                                                    preferred_element_type=F32)
            out += [mn, l]
        return tuple(out)

    res = lax.fori_loop(1, (qi + 1) // 2, body, body(0, tuple(stats)))
    ot = jnp.concatenate([acc_sc[e] / res[2 * e + 1] for e in range(nh)], axis=0)
    o_ref[...] = ot.T.astype(o_ref.dtype)


def _attn_prompt(qt, kb, vt, selt, bsz, t):
    d_attn = ATTN_HEADS * HEAD_DIM
    nq = t // MOBA_BLOCK
    kb3 = kb.reshape(bsz, t, d_attn)
    width = MXU_DIM
    nh = width // HEAD_DIM
    return pl.pallas_call(
        _attn_kernel,
        grid=(bsz, d_attn // width, nq),
        in_specs=[pl.BlockSpec((None, width, MOBA_BLOCK), lambda b, hg, qi: (b, hg, qi)),
                  pl.BlockSpec((None, t, width), lambda b, hg, qi: (b, 0, hg)),
                  pl.BlockSpec((None, width, t), lambda b, hg, qi: (b, hg, 0)),
                  pl.BlockSpec((None, nh * GATE_SLOTS, MOBA_BLOCK), lambda b, hg, qi: (b, hg, qi))],
        out_specs=pl.BlockSpec((MOBA_BLOCK, width), lambda b, hg, qi: (b * nq + qi, hg)),
        out_shape=jax.ShapeDtypeStruct((bsz * t, d_attn), BF16),
        scratch_shapes=[pltpu.VMEM((nh, HEAD_DIM, MOBA_BLOCK), F32)],
        compiler_params=_cparams(3),
        name="attn_prompt",
    )(qt, kb3, vt, selt)


def _sattn_kernel(pt_ref, q_ref, kn_ref, vn_ref, *rest, nblk, blocks_per_step):
    n_pages = 2 * blocks_per_step
    k_refs = rest[:n_pages]
    v_refs = rest[n_pages:2 * n_pages]
    o_ref, g_sc, m_sc, l_sc, o_sc = rest[2 * n_pages:]
    step = pl.program_id(1)
    tq, d_attn = q_ref.shape
    page = k_refs[0].shape[-1]
    rows = ATTN_HEADS * tq
    scale = HEAD_DIM ** -0.5

    def heads_to_rows(x):
        return jnp.concatenate([x[:, h * HEAD_DIM:(h + 1) * HEAD_DIM] for h in range(ATTN_HEADS)], axis=0)

    qr = q_ref[...]
    qt = qr.T
    qs = (heads_to_rows(qr) * scale).astype(BF16)
    lane = lax.broadcasted_iota(jnp.int32, (rows, LANES), 1)
    row_head = _div_pow2(lax.broadcasted_iota(jnp.int32, (rows, d_attn), 0), tq)
    lane_head = _div_pow2(lax.broadcasted_iota(jnp.int32, (rows, d_attn), 1), HEAD_DIM)
    qbd = jnp.where(row_head == lane_head, jnp.concatenate([qr] * ATTN_HEADS, axis=0), 0.0)
    qbs = (qbd * scale).astype(BF16)

    @pl.when(step == 0)
    def _():
        m_sc[...] = jnp.zeros_like(m_sc)
        l_sc[...] = jnp.zeros_like(l_sc)

    m_acc, l_acc = m_sc[...], l_sc[...]
    for c in range(blocks_per_step):
        n = step * blocks_per_step + c
        k0 = k_refs[2 * c][...].reshape(d_attn, page)
        k1 = k_refs[2 * c + 1][...].reshape(d_attn, page)
        v0 = v_refs[2 * c][...].reshape(d_attn, page).astype(BF16)
        v1 = v_refs[2 * c + 1][...].reshape(d_attn, page).astype(BF16)
        kmean = jnp.sum(k0 + k1, axis=1, keepdims=True) * (1.0 / MOBA_BLOCK)
        g_sc[n] = jnp.sum((kmean * qt).reshape(ATTN_HEADS, HEAD_DIM, tq), axis=1)
        s = jnp.concatenate([jnp.dot(qbs, k0.astype(BF16), preferred_element_type=F32),
                             jnp.dot(qbs, k1.astype(BF16), preferred_element_type=F32)], axis=1)
        mx = jnp.max(s, axis=1, keepdims=True)
        p = jnp.exp(s - mx)
        l = jnp.sum(p, axis=1, keepdims=True)
        pb = p.astype(BF16)
        o_full = (lax.dot_general(pb[:, :page], v0, _NT, preferred_element_type=F32)
                  + lax.dot_general(pb[:, page:], v1, _NT, preferred_element_type=F32))
        o_sc[n] = jnp.concatenate([o_full[h * tq:(h + 1) * tq, h * HEAD_DIM:(h + 1) * HEAD_DIM]
                                   for h in range(ATTN_HEADS)], axis=0)
        hit = lane == n
        m_acc = jnp.where(hit, mx, m_acc)
        l_acc = jnp.where(hit, l, l_acc)
    m_sc[...] = m_acc
    l_sc[...] = l_acc

    @pl.when(step == pl.num_programs(1) - 1)
    def _():
        remaining = [g_sc[nn] for nn in range(nblk)]
        chosen = [jnp.zeros((ATTN_HEADS, tq), F32) for _ in range(nblk)]
        for _ in range(MOBA_TOPK):
            gmx = remaining[0]
            for nn in range(1, nblk):
                gmx = jnp.maximum(gmx, remaining[nn])
            open_ = gmx > -jnp.inf
            for nn in range(nblk):
                pick = (remaining[nn] == gmx) & open_
                open_ = open_ & jnp.logical_not(pick)
                chosen[nn] = jnp.where(pick, 1.0, chosen[nn])
                remaining[nn] = jnp.where(pick, -jnp.inf, remaining[nn])
        own_q = lax.broadcasted_iota(jnp.int32, (rows, tq), 1) == (
            lax.broadcasted_iota(jnp.int32, (rows, tq), 0) & (tq - 1))
        picked = jnp.zeros((rows, LANES), F32)
        for nn in range(nblk):
            per_row = jnp.broadcast_to(chosen[nn][:, None, :], (ATTN_HEADS, tq, tq)).reshape(rows, tq)
            flag = jnp.sum(jnp.where(own_q, per_row, 0.0), axis=1, keepdims=True)
            picked = jnp.where(lane == nn, flag, picked)
        selw = picked > 0.5
        m_all = m_sc[...]
        l_all = l_sc[...]
        knew = heads_to_rows(kn_ref[...]).astype(BF16)
        vnew = heads_to_rows(vn_ref[...]).astype(BF16)
        s_own = lax.dot_general(qs, knew, _NT, preferred_element_type=F32)
        ri = lax.broadcasted_iota(jnp.int32, (rows, rows), 0)
        ci = lax.broadcasted_iota(jnp.int32, (rows, rows), 1)
        ok = (_div_pow2(ri, tq) == _div_pow2(ci, tq)) & ((ci & (tq - 1)) <= (ri & (tq - 1)))
        s_own = jnp.where(ok, s_own, NEG_INF)
        m_sel = jnp.max(jnp.where(selw, m_all, -jnp.inf), axis=1, keepdims=True)
        mf = jnp.maximum(jnp.max(s_own, axis=1, keepdims=True), m_sel)
        w = jnp.where(selw, jnp.exp(m_all - mf), 0.0)
        p_own = jnp.exp(s_own - mf)
        lf = jnp.sum(w * l_all, axis=1, keepdims=True) + jnp.sum(p_own, axis=1, keepdims=True)
        of = jnp.dot(p_own.astype(BF16), vnew, preferred_element_type=F32)
        for nn in range(nblk):
            of = of + w[:, nn:nn + 1] * o_sc[nn]
        out = of / lf
        o_ref[...] = jnp.concatenate([out[h * tq:(h + 1) * tq, :] for h in range(ATTN_HEADS)],
                                     axis=1).astype(o_ref.dtype)


def _attn_sample(qr3, kr3, vr3, cache_kt, cache_vt, page_table, layer):
    bsz, tq, d_attn = qr3.shape
    page = cache_kt.shape[4]
    pages_per_blk = MOBA_BLOCK // page
    assert pages_per_blk == 2 and cache_kt.shape[2:4] == (ATTN_HEADS, HEAD_DIM) and page == LANES
    nblk = page_table.shape[1] // pages_per_blk
    rows = ATTN_HEADS * tq
    blocks_per_step = max(c for c in (8, 4, 2, 1) if nblk % c == 0)
    pages_per_step = pages_per_blk * blocks_per_step
    tok_spec = pl.BlockSpec((None, tq, d_attn), lambda b, n, pt: (b, 0, 0))

    def page_spec(j):
        return pl.BlockSpec((None, None, ATTN_HEADS, HEAD_DIM, page),
                            lambda b, n, pt, j=j: (layer, pt[b, pages_per_step * n + j], 0, 0, 0))

    page_specs = [page_spec(j) for j in range(pages_per_step)]
    grid_spec = pltpu.PrefetchScalarGridSpec(
        num_scalar_prefetch=1,
        grid=(bsz, nblk // blocks_per_step),
        in_specs=[tok_spec, tok_spec, tok_spec] + page_specs + page_specs,
        out_specs=pl.BlockSpec((None, tq, d_attn), lambda b, n, pt: (b, 0, 0)),
        scratch_shapes=[pltpu.VMEM((nblk, ATTN_HEADS, tq), F32), pltpu.VMEM((rows, LANES), F32),
                        pltpu.VMEM((rows, LANES), F32), pltpu.VMEM((nblk, rows, HEAD_DIM), F32)],
    )
    return pl.pallas_call(
        functools.partial(_sattn_kernel, nblk=nblk, blocks_per_step=blocks_per_step),
        grid_spec=grid_spec,
        out_shape=jax.ShapeDtypeStruct((bsz, tq, d_attn), BF16),
        compiler_params=_cparams(2),
        name="attn_sample",
    )(page_table, qr3, kr3, vr3,
      *([cache_kt] * pages_per_step), *([cache_vt] * pages_per_step))


def _mixer_kernel(x_ref, ya_ref, ys_ref, yc_ref, g0_ref, g1_ref, g2_ref,
                  wr_ref, ws_ref, wa_ref, wo_ref, l1g_ref, l1b_ref,
                  wi_ref, wf_ref, l2g_ref, l2b_ref, o_ref, h_sc, *, alpha, tf):
    merged = jax.nn.sigmoid(g0_ref[...]) * jnp.dot(ya_ref[...], wr_ref[...], preferred_element_type=F32)
    merged = merged + jax.nn.sigmoid(g1_ref[...]) * jnp.dot(ys_ref[...], ws_ref[...],
                                                           preferred_element_type=F32)
    merged = merged + jax.nn.sigmoid(g2_ref[...]) * jnp.dot(yc_ref[...], wa_ref[...],
                                                           preferred_element_type=F32)
    z = alpha * x_ref[...] + jnp.dot(merged.astype(BF16), wo_ref[...], preferred_element_type=F32)
    x1 = _layer_norm(z, l1g_ref[...], l1b_ref[...])

    d_ff = wf_ref.shape[0]
    xb = x1.astype(BF16)
    for j in range(d_ff // tf):
        hg = jnp.dot(xb, wi_ref[:, j * tf:(j + 1) * tf], preferred_element_type=F32)
        hu = jnp.dot(xb, wi_ref[:, d_ff + j * tf:d_ff + (j + 1) * tf], preferred_element_type=F32)
        h_sc[:, j * tf:(j + 1) * tf] = ((hg * jax.nn.sigmoid(hg)) * hu).astype(BF16)
    y = jnp.dot(h_sc[...], wf_ref[...], preferred_element_type=F32)
    o_ref[...] = _layer_norm(alpha * x1 + y, l2g_ref[...], l2b_ref[...])


def _mixer(x, ya, ys, yc, proj2, lw, tm, gate_col0, alpha):
    n, d = x.shape
    d_ff = lw["w_ffn_out"].shape[1]
    layer = lw["layer"]
    tf = max(c for c in range(LANES, 6 * MXU_DIM + 1, LANES) if d_ff % c == 0)
    row = lambda w: pl.BlockSpec((tm, w), lambda i: (i, 0))
    gspec = lambda c: pl.BlockSpec((tm, d), lambda i, c=c: (i, c))
    vec = _resident((1, d))
    mats = [lw["w_br_rnn"], lw["w_br_ssm"], lw["w_br_attn"], lw["w_out"]]
    ffn_mats = [lw["w_ffn_in"], lw["w_ffn_out"]]
    return pl.pallas_call(
        functools.partial(_mixer_kernel, alpha=alpha, tf=tf),
        grid=(n // tm,),
        in_specs=[row(d), row(ya.shape[1]), row(ys.shape[1]), row(yc.shape[1]),
                  gspec(gate_col0), gspec(gate_col0 + 1), gspec(gate_col0 + 2)]
        + [_resident(a.shape[1:], layer) for a in mats] + [vec, vec]
        + [_resident(a.shape[1:], layer) for a in ffn_mats] + [vec, vec],
        out_specs=row(d),
        out_shape=jax.ShapeDtypeStruct((n, d), F32),
        scratch_shapes=[pltpu.VMEM((tm, d_ff), BF16)],
        compiler_params=_cparams(1),
        name="mixer",
    )(x, ya, ys, yc, proj2, proj2, proj2, *mats, lw["ln1_g"], lw["ln1_b"],
      *ffn_mats, lw["ln2_g"], lw["ln2_b"])


def _block_diag(w, per_block):
    n, r, c = w.shape
    eye = jnp.eye(per_block, dtype=w.dtype)
    out = jnp.einsum("kgrc,gh->kgrhc", w.reshape(n // per_block, per_block, r, c), eye)
    return out.reshape(n // per_block, per_block * r, per_block * c)


BIG_WEIGHTS = ("w_in", "w_br_rnn", "w_br_ssm", "w_br_attn", "w_out", "w_ffn_in", "w_ffn_out")


def _permute_w_in(w, d_rnn, d_ssm, d_attn):
    a, b, c = d_rnn, d_rnn + d_ssm, d_rnn + d_ssm + 3 * d_attn
    return jnp.concatenate([w[..., :a], w[..., c:], w[..., a:b], w[..., b:c]], axis=-1)


def _prep_layer(l, p, big):
    heads_per_tile = MXU_DIM // (p["rg_w_a"].shape[-1])
    row = lambda a: a[l].reshape(1, -1)
    groups = p["s5_b_re"].shape[1]
    lw = {
        "layer": l,
        "w_in": big["w_in"],
        "conv_w": p["conv_w"][l],
        "conv_b": row(p["conv_b"]),
        "rg_wbd": jnp.concatenate([_block_diag(p["rg_w_a"][l], heads_per_tile),
                                   _block_diag(p["rg_w_x"][l], heads_per_tile)], axis=2).astype(BF16),
        "rg_b_a": row(p["rg_b_a"]), "rg_b_x": row(p["rg_b_x"]), "rg_lambda": row(p["rg_lambda"]),
        "s5_lam_re": row(p["s5_lambda_re"]), "s5_lam_im": row(p["s5_lambda_im"]),
        "s5_log_dt": jnp.repeat(p["s5_log_step"][l], SSM_STATE).reshape(1, -1),
        "s5_wbr": _block_diag(jnp.swapaxes(p["s5_b_re"][l], 1, 2), groups // 2),
        "s5_wbi": _block_diag(jnp.swapaxes(p["s5_b_im"][l], 1, 2), groups // 2),
        "s5_cre": _block_diag(jnp.swapaxes(p["s5_c_re"][l], 1, 2), LANES // SSM_GROUP).astype(BF16),
        "s5_cim": _block_diag(jnp.swapaxes(p["s5_c_im"][l], 1, 2), LANES // SSM_GROUP).astype(BF16),
        "s5_d": row(p["s5_d"]),
        "s5_w_glu": p["s5_w_glu"][l].astype(BF16),
        "s5_b_glu": row(p["s5_b_glu"]),
        "w_br_rnn": big["w_br_rnn"],
        "w_br_ssm": big["w_br_ssm"],
        "w_br_attn": big["w_br_attn"],
        "w_out": big["w_out"],
        "ln1_g": row(p["ln1_g"]), "ln1_b": row(p["ln1_b"]),
        "w_ffn_in": big["w_ffn_in"],
        "w_ffn_out": big["w_ffn_out"],
        "ln2_g": row(p["ln2_g"]), "ln2_b": row(p["ln2_b"]),
    }
    return lw


def _rope_tables(pos0, t):
    half = HEAD_DIM // 2
    inv = jnp.power(ROPE_THETA, -jnp.arange(half, dtype=F32) * (2.0 / HEAD_DIM))
    ang = (pos0 + jnp.arange(t)).astype(F32)[:, None] * inv
    cos, sin = jnp.cos(ang), jnp.sin(ang)
    reps = LANES // HEAD_DIM
    cos_t = jnp.tile(jnp.concatenate([cos, cos], axis=1), (1, reps))
    sin_t = jnp.tile(jnp.concatenate([-sin, sin], axis=1), (1, reps))
    return cos_t, sin_t


def _largest_tile(n, cap):
    t = min(n, cap)
    while n % t:
        t //= 2
    return t


def _trunk_layer(x3, lw, alpha, cbuf8, h0, s5r0, s5i0, *, prompt, pos0=0, cache=None, prev_kv=()):
    bsz, t, d = x3.shape
    n = bsz * t
    d_rnn = h0.shape[-1]
    d_ssm = lw["s5_d"].shape[-1]
    d_attn = ATTN_HEADS * HEAD_DIM
    x2 = x3.reshape(n, d)
    tm_rope = MOBA_BLOCK
    tn_proj = 6 * MXU_DIM
    assert n % tm_rope == 0
    if prompt:
        assert t % MOBA_BLOCK == 0 and MOBA_BLOCK < t <= GATE_SLOTS * MOBA_BLOCK
        cos_t, sin_t = _rope_tables(0, t)
        proj2, qt, krt, vrt, kb, vt, selt = _proj_rope(x2, lw["w_in"], lw["layer"], cos_t, sin_t, tm_rope, tn_proj,
                                                       True, t // MOBA_BLOCK, prev_kv)
    else:
        assert tm_rope % t == 0
        cos_t, sin_t = _rope_tables(pos0, t)
        cos_t = jnp.tile(cos_t, (tm_rope // t, 1))
        sin_t = jnp.tile(sin_t, (tm_rope // t, 1))
        proj2, qr, kr, vr = _proj_rope(x2, lw["w_in"], lw["layer"], cos_t, sin_t, tm_rope, tn_proj, False, 1)
    proj3 = proj2.reshape(bsz, t, -1)

    if prompt:
        bb, tc = 1, MOBA_BLOCK
    else:
        bb, tc = _largest_tile(bsz, 32), t
    assert tc % SUBLANES == 0 and tc & (tc - 1) == 0 and t % tc == 0
    ya, conv_new, h_new = _rglru(proj3, cbuf8, h0, lw, bb, tc)
    ys, s5r, s5i = _s5(proj3, s5r0, s5i0, lw, bb, tc, (d_rnn + 3 * d) // d_ssm)

    if prompt:
        yc = _attn_prompt(qt, kb, vt, selt, bsz, t)
        new_k, new_v = krt, vrt
    else:
        cache_kt, cache_vt, page_table, layer = cache
        yc = _attn_sample(qr.reshape(bsz, t, d_attn), kr.reshape(bsz, t, d_attn), vr.reshape(bsz, t, d_attn),
                          cache_kt, cache_vt, page_table, layer).reshape(n, d_attn)
        new_k = kr.reshape(bsz, t, ATTN_HEADS, HEAD_DIM)
        new_v = vr.reshape(bsz, t, ATTN_HEADS, HEAD_DIM)

    gate_col0 = d_rnn // d
    x_out = _mixer(x2, ya.reshape(n, d_rnn), ys.reshape(n, d_ssm), yc, proj2, lw, _largest_tile(n, 256),
                   gate_col0, alpha)
    return (x_out.reshape(bsz, t, d), new_k, new_v, conv_new, h_new.reshape(bsz, d_rnn), s5r, s5i)


def kernel(x_prompt, x_sample, cache_k, cache_v, state_conv, state_rglru, state_s5_re, state_s5_im, page_table,
           w_in, conv_w, conv_b, rg_w_a, rg_b_a, rg_w_x, rg_b_x, rg_lambda,
           s5_lambda_re, s5_lambda_im, s5_b_re, s5_b_im, s5_c_re, s5_c_im, s5_d, s5_log_step, s5_w_glu, s5_b_glu,
           w_br_rnn, w_br_ssm, w_br_attn, w_out, ln1_g, ln1_b, w_ffn_in, w_ffn_out, ln2_g, ln2_b):
    params = dict(w_in=w_in, conv_w=conv_w, conv_b=conv_b, rg_w_a=rg_w_a, rg_b_a=rg_b_a, rg_w_x=rg_w_x,
                  rg_b_x=rg_b_x, rg_lambda=rg_lambda, s5_lambda_re=s5_lambda_re, s5_lambda_im=s5_lambda_im,
                  s5_b_re=s5_b_re, s5_b_im=s5_b_im, s5_c_re=s5_c_re, s5_c_im=s5_c_im, s5_d=s5_d,
                  s5_log_step=s5_log_step, s5_w_glu=s5_w_glu, s5_b_glu=s5_b_glu, w_br_rnn=w_br_rnn,
                  w_br_ssm=w_br_ssm, w_br_attn=w_br_attn, w_out=w_out, ln1_g=ln1_g, ln1_b=ln1_b,
                  w_ffn_in=w_ffn_in, w_ffn_out=w_ffn_out, ln2_g=ln2_g, ln2_b=ln2_b)
    depth = w_in.shape[0]
    alpha = (2.0 * depth) ** 0.25
    bp = x_prompt.shape[0]
    bs = x_sample.shape[0]
    d_rnn = state_rglru.shape[-1]
    groups, n_p = state_s5_re.shape[-2:]
    n_state = groups * n_p
    n_pages = page_table.shape[1]
    page = cache_k.shape[2]
    past_len = n_pages * page
    d_attn = ATTN_HEADS * HEAD_DIM
    assert past_len % MOBA_BLOCK == 0 and x_sample.shape[1] <= MOBA_BLOCK
    cache_kt = cache_k.transpose(0, 1, 3, 4, 2)
    cache_vt = cache_v.transpose(0, 1, 3, 4, 2)

    zeros_p = lambda *s: jnp.zeros((bp,) + s, F32)
    yp, ys = x_prompt, x_sample
    outs_p, outs_s = [], []
    big = {k: params[k].astype(BF16) for k in BIG_WEIGHTS}
    big["w_in"] = _permute_w_in(big["w_in"], d_rnn, n_state // SSM_STATE * SSM_GROUP, d_attn)
    for l in range(depth):
        lw = _prep_layer(l, params, big)
        prev_kv = tuple(o[0] for o in outs_p) + tuple(o[1] for o in outs_p) if l == depth - 1 else ()
        res = _trunk_layer(yp, lw, alpha, zeros_p(SUBLANES, d_rnn), zeros_p(1, d_rnn),
                           zeros_p(1, n_state), zeros_p(1, n_state), prompt=True, prev_kv=prev_kv)
        yp = res[0]
        outs_p.append(res[1:])
        cbuf8 = jnp.pad(state_conv[l], ((0, 0), (SUBLANES - (CONV_WIDTH - 1), 0), (0, 0)))
        res = _trunk_layer(ys, lw, alpha, cbuf8, state_rglru[l].reshape(bs, 1, d_rnn),
                           state_s5_re[l].reshape(bs, 1, n_state), state_s5_im[l].reshape(bs, 1, n_state),
                           prompt=False, pos0=past_len, cache=(cache_kt, cache_vt, page_table, l))
        ys = res[0]
        outs_s.append(res[1:])

    def stack(outs, i, shape=None):
        arrs = [o[i] if shape is None else o[i].reshape(shape) for o in outs]
        return jnp.stack(arrs)

    def kv_prompt(i):
        stacked = outs_p[-1][i] if depth > 1 else outs_p[-1][i][None]
        t = stacked.shape[-1]
        return stacked.reshape(depth, bp, ATTN_HEADS, HEAD_DIM, t).transpose(0, 1, 4, 2, 3)

    return (yp, ys,
            kv_prompt(0), kv_prompt(1), stack(outs_p, 2), stack(outs_p, 3),
            stack(outs_p, 4, (bp, groups, n_p)), stack(outs_p, 5, (bp, groups, n_p)),
            stack(outs_s, 0), stack(outs_s, 1), stack(outs_s, 2), stack(outs_s, 3),
            stack(outs_s, 4, (bs, groups, n_p)), stack(outs_s, 5, (bs, groups, n_p)))
```
